```python
import math
import jax, jax.numpy as jnp
from jax import lax
import numpy as np

D_MODEL = 1024
BATCH = 4
SEQ = 8192
DEPTH = 4

CHUNK = 64
Q_BLOCK = 128
D_MIX = D_MODEL
CONV_CH = D_MIX // 4
CONV_WIDTH = 31
DIFF_HEADS = 4
DIFF_DV = D_MIX // 8
DIFF_DH = DIFF_DV // 2
GLA_HEADS = 4
GLA_DV = D_MIX // 16
GLA_DK = GLA_DV // 2
GLA_GATE_RANK = 16
GLA_TAU = 16.0
ROPE_THETA = 500000.0
ROT_DIMS = DIFF_DH // 4
D_FF = 2816
N_EXPERTS = 8
TOP_K = 2
D_FF_EXPERT = 3584
EXPERT_BLOCK = 256
PLE_DIM = 256
EPS = 1e-6

DIFF_QK = DIFF_HEADS * 2 * DIFF_DH
DIFF_V = DIFF_HEADS * DIFF_DV
GLA_QK = GLA_HEADS * GLA_DK
GLA_V = GLA_HEADS * GLA_DV
IN_SPLITS = (CONV_CH, CONV_CH, DIFF_QK, DIFF_QK, DIFF_V, GLA_QK, GLA_QK, GLA_V, GLA_V, GLA_GATE_RANK)
D_IN = CONV_CH * 2 + DIFF_QK * 2 + DIFF_V + GLA_QK * 2 + GLA_V * 2 + GLA_GATE_RANK

kernel_name = "hybrid_conv_diffattn_gla_moe_trunk"


def rmsnorm(x, g):
    xf = x.astype(jnp.float32)
    y = xf * lax.rsqrt(jnp.mean(xf * xf, axis=-1, keepdims=True) + EPS)
    return (y * g.astype(jnp.float32)).astype(x.dtype)


def layernorm(x, g, b):
    xf = x.astype(jnp.float32)
    mu = jnp.mean(xf, axis=-1, keepdims=True)
    var = jnp.mean(jnp.square(xf - mu), axis=-1, keepdims=True)
    y = (xf - mu) * lax.rsqrt(var + EPS)
    return (y * g.astype(jnp.float32) + b.astype(jnp.float32)).astype(x.dtype)


def rope_partial(x, cos, sin):
    half = ROT_DIMS // 2
    bshape = (x.shape[1],) + (1,) * (x.ndim - 3) + (half,)
    c = cos.reshape(bshape)
    s = sin.reshape(bshape)
    xf = x.astype(jnp.float32)
    x1 = xf[..., :half]
    x2 = xf[..., half:ROT_DIMS]
    out = jnp.concatenate([x1 * c - x2 * s, x2 * c + x1 * s, xf[..., ROT_DIMS:]], axis=-1)
    return out.astype(x.dtype)


def conformer_conv(a, g, w, b, ln_g, ln_b):
    u = a * jax.nn.sigmoid(g)
    y = lax.conv_general_dilated(u, w[:, None, :].astype(u.dtype), window_strides=(1,),
                                 padding=[(CONV_WIDTH - 1, 0)],
                                 dimension_numbers=('NWC', 'WIO', 'NWC'),
                                 feature_group_count=CONV_CH) + b
    return jax.nn.silu(layernorm(y, ln_g, ln_b))


def diff_attention(q, k, v, lam, subln_g, lam_init):
    bsz, seq = q.shape[0], q.shape[1]
    nqb = seq // Q_BLOCK
    qb = jnp.moveaxis(q.reshape(bsz, nqb, Q_BLOCK, DIFF_HEADS, 2, DIFF_DH), 1, 0)
    k_chunk = jnp.arange(seq) // CHUNK
    scale = DIFF_DH ** -0.5

    def one_block(args):
        q_blk, bi = args
        s = jnp.einsum('bqhcd,bkhcd->bhcqk', q_blk, k).astype(jnp.float32) * scale
        q_chunk = (bi * Q_BLOCK + jnp.arange(Q_BLOCK)) // CHUNK
        allowed = k_chunk[None, :] <= q_chunk[:, None]
        s = jnp.where(allowed, s, -jnp.inf)
        pr = jax.nn.softmax(s, axis=-1)
        a = pr[:, :, 0] - lam * pr[:, :, 1]
        return jnp.einsum('bhqk,bkhv->bqhv', a.astype(v.dtype), v)

    o = lax.map(one_block, (qb, jnp.arange(nqb)))
    o = jnp.moveaxis(o, 0, 1).reshape(bsz, seq, DIFF_HEADS, DIFF_DV)
    o = rmsnorm(o, subln_g) * (1.0 - lam_init)
    return o.reshape(bsz, seq, DIFF_V)


def gla_mixer(q, k, v, r, gz, w_g2, b_g, norm_g):
    bsz, seq = q.shape[0], q.shape[1]
    nc = seq // CHUNK
    dt = v.dtype
    shp = (bsz, nc, CHUNK, GLA_HEADS)
    log_a = jax.nn.log_sigmoid((gz @ w_g2 + b_g).astype(jnp.float32)) / GLA_TAU
    log_a = log_a.reshape(shp + (GLA_DK,))
    qc = q.astype(jnp.float32).reshape(shp + (GLA_DK,)) * (GLA_DK ** -0.5)
    kc = k.astype(jnp.float32).reshape(shp + (GLA_DK,))
    vc = v.astype(jnp.float32).reshape(shp + (GLA_DV,))
    cum = jnp.cumsum(log_a, axis=2)
    tot = cum[:, :, -1]
    kv = jnp.einsum('bnchk,bnchv->nbhkv', kc * jnp.exp(tot[:, :, None] - cum), vc)
    decay = jnp.moveaxis(jnp.exp(tot), 1, 0)

    def step(state, inp):
        d, kv_c = inp
        state = d[..., None] * state + kv_c
        return state, state

    init = jnp.zeros((bsz, GLA_HEADS, GLA_DK, GLA_DV), jnp.float32)
    _, states = lax.scan(step, init, (decay, kv))
    o = jnp.einsum('bnchk,nbhkv->bnchv', qc, states).reshape(bsz, seq, GLA_HEADS, GLA_DV)
    gate = jax.nn.silu(r.astype(jnp.float32)).reshape(bsz, seq, GLA_HEADS, GLA_DV)
    o = rmsnorm(o, norm_g) * gate
    return o.reshape(bsz, seq, GLA_V).astype(dt)


def swiglu(h, w_gu, w_down):
    g, u = jnp.split(h @ w_gu, 2, axis=-1)
    return (jax.nn.silu(g) * u) @ w_down


def moe_swiglu(h, router_w, w_gu, w_down):
    bsz, seq, d = h.shape
    t = bsz * seq
    xf = h.reshape(t, d)
    logits = (xf @ router_w).astype(jnp.float32)
    top_val, top_idx = lax.top_k(logits, TOP_K)
    gates = jax.nn.softmax(top_val, axis=-1)
    flat_e = top_idx.reshape(-1)
    flat_tok = jnp.repeat(jnp.arange(t, dtype=jnp.int32), TOP_K)
    flat_w = gates.reshape(-1)
    order = jnp.argsort(flat_e, stable=True)
    sorted_e = flat_e[order]
    counts = jnp.bincount(flat_e, length=N_EXPERTS)
    padded = ((counts + EXPERT_BLOCK - 1) // EXPERT_BLOCK) * EXPERT_BLOCK
    pend = jnp.cumsum(padded)
    pstart = pend - padded
    ustart = jnp.cumsum(counts) - counts
    n_assign = t * TOP_K
    dest = pstart[sorted_e] + jnp.arange(n_assign) - ustart[sorted_e]
    m_rows = n_assign + N_EXPERTS * EXPERT_BLOCK
    row_tok = jnp.zeros((m_rows,), jnp.int32).at[dest].set(flat_tok[order])
    row_w = jnp.zeros((m_rows,), jnp.float32).at[dest].set(flat_w[order])
    nb = m_rows // EXPERT_BLOCK
    block_e = jnp.minimum(jnp.searchsorted(pend, jnp.arange(nb) * EXPERT_BLOCK, side='right'),
                          N_EXPERTS - 1)
    xs = xf[row_tok].reshape(nb, EXPERT_BLOCK, d)

    def expert_block(args):
        xb, e = args
        return swiglu(xb, w_gu[e], w_down[e])

    ys = lax.map(expert_block, (xs, block_e)).reshape(m_rows, d)
    out = jnp.zeros((t, d), jnp.float32).at[row_tok].add(ys.astype(jnp.float32) * row_w[:, None])
    return out.astype(h.dtype).reshape(bsz, seq, d)


def setup_inputs(seed: int = 0) -> dict:
    key = jax.random.key(seed)
    ks = jax.random.split(key, 24)
    n_dense = (DEPTH + 1) // 2
    n_moe = DEPTH // 2

    def nrm(k, shape, s):
        return jax.random.normal(k, shape, jnp.float32) * s

    return {
        "x": nrm(ks[0], (BATCH, SEQ, D_MODEL), 1.0),
        "p": nrm(ks[1], (DEPTH, BATCH, SEQ, PLE_DIM), 1.0),
        "norm_mix_g": 1.0 + nrm(ks[2], (DEPTH, D_MODEL), 0.02),
        "w_in": nrm(ks[3], (DEPTH, D_MODEL, D_IN), D_MODEL ** -0.5),
        "conv_w": nrm(ks[4], (DEPTH, CONV_WIDTH, CONV_CH), CONV_WIDTH ** -0.5),
        "conv_b": nrm(ks[5], (DEPTH, CONV_CH), 0.02),
        "conv_ln_g": 1.0 + nrm(ks[6], (DEPTH, CONV_CH), 0.02),
        "conv_ln_b": nrm(ks[7], (DEPTH, CONV_CH), 0.02),
        "diff_lambda": nrm(ks[8], (DEPTH, 4, DIFF_DH), 0.1),
        "diff_subln_g": 1.0 + nrm(ks[9], (DEPTH, DIFF_DV), 0.02),
        "gla_w_gate2": nrm(ks[10], (DEPTH, GLA_GATE_RANK, GLA_QK), GLA_GATE_RANK ** -0.5),
        "gla_b_gate": nrm(ks[11], (DEPTH, GLA_QK), 0.1),
        "gla_norm_g": 1.0 + nrm(ks[12], (DEPTH, GLA_DV), 0.02),
        "w_out": nrm(ks[13], (DEPTH, D_MIX, D_MODEL), D_MIX ** -0.5),
        "norm_ffn_g": 1.0 + nrm(ks[14], (DEPTH, D_MODEL), 0.02),
        "ffn_w_gu": nrm(ks[15], (n_dense, D_MODEL, 2 * D_FF), D_MODEL ** -0.5),
        "ffn_w_down": nrm(ks[16], (n_dense, D_FF, D_MODEL), D_FF ** -0.5),
        "router_w": nrm(ks[17], (n_moe, D_MODEL, N_EXPERTS), D_MODEL ** -0.5),
        "moe_w_gu": nrm(ks[18], (n_moe, N_EXPERTS, D_MODEL, 2 * D_FF_EXPERT), D_MODEL ** -0.5),
        "moe_w_down": nrm(ks[19], (n_moe, N_EXPERTS, D_FF_EXPERT, D_MODEL), D_FF_EXPERT ** -0.5),
        "ple_w_up": nrm(ks[20], (DEPTH, PLE_DIM, D_MODEL), PLE_DIM ** -0.5),
        "ple_w_gate": nrm(ks[21], (DEPTH, D_MODEL, D_MODEL), D_MODEL ** -0.5),
        "ple_norm_g": 1.0 + nrm(ks[22], (DEPTH, D_MODEL), 0.02),
        "final_norm_g": 1.0 + nrm(ks[23], (D_MODEL,), 0.02),
    }


def reference(x, p, norm_mix_g, w_in, conv_w, conv_b, conv_ln_g, conv_ln_b, diff_lambda,
              diff_subln_g, gla_w_gate2, gla_b_gate, gla_norm_g, w_out, norm_ffn_g,
              ffn_w_gu, ffn_w_down, router_w, moe_w_gu, moe_w_down, ple_w_up, ple_w_gate,
              ple_norm_g, final_norm_g):
    bsz, seq, _ = x.shape
    pos = jnp.arange(seq, dtype=jnp.float32)
    inv_freq = ROPE_THETA ** (-jnp.arange(0, ROT_DIMS, 2, dtype=jnp.float32) / ROT_DIMS)
    ang = pos[:, None] * inv_freq[None, :]
    cos, sin = jnp.cos(ang), jnp.sin(ang)
    split_at = [int(v) for v in np.cumsum(IN_SPLITS)[:-1]]

    h = x
    for i in range(DEPTH):
        hn = rmsnorm(h, norm_mix_g[i])
        z = hn @ w_in[i]
        (c_a, c_g, d_q, d_k, d_v, g_q, g_k, g_v, g_r, g_z) = jnp.split(z, split_at, axis=-1)

        y_conv = conformer_conv(c_a, c_g, conv_w[i], conv_b[i], conv_ln_g[i], conv_ln_b[i])

        lam_init = 0.8 - 0.6 * math.exp(-0.3 * i)
        lp = diff_lambda[i].astype(jnp.float32)
        lam = jnp.exp(jnp.sum(lp[0] * lp[1])) - jnp.exp(jnp.sum(lp[2] * lp[3])) + lam_init
        dq = rope_partial(d_q.reshape(bsz, seq, DIFF_HEADS, 2, DIFF_DH), cos, sin)
        dk = rope_partial(d_k.reshape(bsz, seq, DIFF_HEADS, 2, DIFF_DH), cos, sin)
        dv = d_v.reshape(bsz, seq, DIFF_HEADS, DIFF_DV)
        y_diff = diff_attention(dq, dk, dv, lam, diff_subln_g[i], lam_init)

        y_gla = gla_mixer(g_q, g_k, g_v, g_r, g_z, gla_w_gate2[i], gla_b_gate[i], gla_norm_g[i])

        h = h + jnp.concatenate([y_conv, y_diff, y_gla], axis=-1) @ w_out[i]

        hn = rmsnorm(h, norm_ffn_g[i])
        if i % 2 == 0:
            h = h + swiglu(hn, ffn_w_gu[i // 2], ffn_w_down[i // 2])
        else:
            h = h + moe_swiglu(hn, router_w[i // 2], moe_w_gu[i // 2], moe_w_down[i // 2])

        gate = jax.nn.sigmoid(rmsnorm(h, ple_norm_g[i]) @ ple_w_gate[i])
        h = h + gate * (p[i] @ ple_w_up[i])

    return rmsnorm(h, final_norm_g)
```

```python
import functools
import math

import jax
import jax.numpy as jnp
import numpy as np
from jax import lax
from jax.experimental import pallas as pl
from jax.experimental.pallas import tpu as pltpu

F32 = jnp.float32
BF16 = jnp.bfloat16

D_MODEL = 1024
CHUNK = 64
CONV_CH = 256
CONV_WIDTH = 31
DIFF_HEADS = 4
DIFF_DV = 128
DIFF_DH = 64
GLA_HEADS = 4
GLA_DV = 64
GLA_DK = 32
GLA_GATE_RANK = 16
GLA_TAU = 16.0
ROPE_THETA = 500000.0
ROT_DIMS = 16
D_FF = 2816
N_EXPERTS = 8
D_FF_EXPERT = 3584
PLE_DIM = 256
EPS = 1e-6
DIFF_W = 512
GLA_QK = GLA_HEADS * GLA_DK
GLA_V = GLA_HEADS * GLA_DV
D_IN = 2832
D_IN_PAD = 2944

LANES = 128
SUBLANES = 8
VMEM_LIMIT = 56 * 1024 * 1024

TM = 512
TQ = 512
CONV_TS = 512
CONV_HALO = 32
GLA_TG = 512
FFN_FB = 1408
MOE_BLK = 512
MOE_FB = 896
ROW_SPLIT = D_MODEL // LANES
TMD = 256


def _cparams(sem):
    return pltpu.CompilerParams(dimension_semantics=sem, vmem_limit_bytes=VMEM_LIMIT)


def _rms(x, g):
    ms = jnp.mean(x * x, axis=-1, keepdims=True)
    return x * lax.rsqrt(ms + EPS) * g


def _sigmoid(x):
    return 1.0 / (1.0 + jnp.exp(-x))


def _silu(x):
    return x * _sigmoid(x)


def _rope(x, ra, rb, rc):
    outs = []
    for c in range(x.shape[1] // LANES):
        xc = x[:, c * LANES:(c + 1) * LANES]
        outs.append(xc * ra + pltpu.roll(xc, ROT_DIMS // 2, 1) * rb
                    + pltpu.roll(xc, LANES - ROT_DIMS // 2, 1) * rc)
    return jnp.concatenate(outs, axis=1)


def _inproj_kernel(h_ref, g_ref, w_ref, ra_ref, rb_ref, rc_ref, wg2_ref, bg_ref,
                   u_ref, q_ref, k_ref, v_ref, gq_ref, gk_ref, gv_ref, gr_ref, la_ref):
    hn = _rms(h_ref[...], g_ref[...]).astype(BF16)

    def proj(a, b):
        return jnp.dot(hn, w_ref[:, a:b], preferred_element_type=F32)

    ra, rb, rc = ra_ref[...], rb_ref[...], rc_ref[...]
    u_ref[...] = proj(0, 256) * _sigmoid(proj(256, 512))
    q_ref[...] = (_rope(proj(512, 1024), ra, rb, rc) * (DIFF_DH ** -0.5)).astype(BF16)
    k_ref[...] = _rope(proj(1024, 1536), ra, rb, rc).astype(BF16)
    v_ref[...] = proj(1536, 2048).astype(BF16)
    gq_ref[...] = proj(2048, 2176) * (GLA_DK ** -0.5)
    gk_ref[...] = proj(2176, 2304)
    gv_ref[...] = proj(2304, 2560)
    gr_ref[...] = _silu(proj(2560, 2816))
    gz = proj(2816, D_IN_PAD)
    ga = jnp.dot(gz, wg2_ref[...], preferred_element_type=F32,
                 precision=lax.Precision.HIGHEST) + bg_ref[...]
    la_ref[...] = (jnp.minimum(ga, 0.0) - jnp.log(1.0 + jnp.exp(-jnp.abs(ga)))) * (1.0 / GLA_TAU)


def _inproj(h, g, w, ropes, wg2, bg, seq):
    t = h.shape[0]
    nseq = seq // TM
    row = lambda w_: pl.BlockSpec((TM, w_), lambda i: (i, 0))
    const = lambda a, b: pl.BlockSpec((a, b), lambda i: (0, 0))
    rope_spec = pl.BlockSpec((TM, LANES), lambda i: (i % nseq, 0))
    outs = [(CONV_CH, F32), (DIFF_W, BF16), (DIFF_W, BF16), (DIFF_W, BF16),
            (GLA_QK, F32), (GLA_QK, F32), (GLA_V, F32), (GLA_V, F32), (GLA_QK, F32)]
    return pl.pallas_call(
        _inproj_kernel,
        grid=(t // TM,),
        in_specs=[row(D_MODEL), const(1, D_MODEL), const(D_MODEL, D_IN_PAD),
                  rope_spec, rope_spec, rope_spec, const(LANES, LANES), const(1, LANES)],
        out_specs=[row(w_) for w_, _ in outs],
        out_shape=[jax.ShapeDtypeStruct((t, w_), dt) for w_, dt in outs],
        compiler_params=_cparams(("parallel",)),
        name="inproj",
    )(h, g, w, *ropes, wg2, bg)


def _conv_kernel(u_ref, w_ref, b_ref, lg_ref, lb_ref, o_ref, win):
    @pl.when(pl.program_id(1) == 0)
    def _():
        win[0:CONV_HALO, :] = jnp.zeros((CONV_HALO, CONV_CH), F32)

    win[CONV_HALO:, :] = u_ref[...]
    rows = 64
    base = CONV_HALO - (CONV_WIDTH - 1)
    for r0 in range(0, CONV_TS, rows):
        acc = jnp.zeros((rows, CONV_CH), F32) + b_ref[...]
        for j in range(CONV_WIDTH):
            acc = acc + win[base + r0 + j: base + r0 + j + rows, :] * w_ref[j:j + 1, :]
        mu = jnp.mean(acc, axis=-1, keepdims=True)
        xc = acc - mu
        var = jnp.mean(xc * xc, axis=-1, keepdims=True)
        y = xc * lax.rsqrt(var + EPS) * lg_ref[...] + lb_ref[...]
        o_ref[r0:r0 + rows, :] = _silu(y).astype(BF16)
    win[0:CONV_HALO, :] = win[CONV_TS:CONV_TS + CONV_HALO, :]


def _conv(u, w, b, lg, lb, bsz, seq):
    nt = seq // CONV_TS
    const = lambda a, c: pl.BlockSpec((a, c), lambda bi, ti: (0, 0))
    return pl.pallas_call(
        _conv_kernel,
        grid=(bsz, nt),
        in_specs=[pl.BlockSpec((CONV_TS, CONV_CH), lambda bi, ti: (bi * nt + ti, 0)),
                  const(CONV_WIDTH, CONV_CH), const(1, CONV_CH), const(1, CONV_CH), const(1, CONV_CH)],
        out_specs=pl.BlockSpec((CONV_TS, CONV_CH), lambda bi, ti: (bi * nt + ti, 0)),
        out_shape=jax.ShapeDtypeStruct(u.shape, BF16),
        scratch_shapes=[pltpu.VMEM((CONV_TS + CONV_HALO, CONV_CH), F32)],
        compiler_params=_cparams(("arbitrary", "arbitrary")),
        name="conv",
    )(u, w, b, lg, lb)


def _attn_kernel(qt_ref, kt_ref, lam_ref, sg_ref, q_ref, k_ref, v_ref, o_ref,
                 m_scr, l_scr, acc_scr, *, lam_init):
    step = pl.program_id(2)
    qi = qt_ref[step]
    kj = kt_ref[step]

    @pl.when(kj == 0)
    def _():
        m_scr[...] = jnp.full(m_scr.shape, -jnp.inf, F32)
        l_scr[...] = jnp.zeros(l_scr.shape, F32)
        acc_scr[...] = jnp.zeros(acc_scr.shape, F32)

    def update(masked):
        q = q_ref[...]
        k = k_ref[...]
        v = v_ref[...]
        lane = lax.broadcasted_iota(jnp.int32, q.shape, 1)
        if masked:
            rq = lax.broadcasted_iota(jnp.int32, (TQ, TQ), 0) // CHUNK
            ck = lax.broadcasted_iota(jnp.int32, (TQ, TQ), 1) // CHUNK
            allowed = ck <= rq
        for c in range(2):
            sel = (lane < DIFF_DH) if c == 0 else (lane >= DIFF_DH)
            qc = jnp.where(sel, q, jnp.zeros_like(q))
            s = lax.dot_general(qc, k, (((1,), (1,)), ((), ())), preferred_element_type=F32)
            if masked:
                s = jnp.where(allowed, s, -jnp.inf)
            m_old = m_scr[c]
            m_new = jnp.maximum(m_old, jnp.max(s, axis=-1, keepdims=True))
            alpha = jnp.exp(m_old - m_new)
            p = jnp.exp(s - m_new)
            l_scr[c] = alpha * l_scr[c] + jnp.sum(p, axis=-1, keepdims=True)
            acc_scr[c] = alpha * acc_scr[c] + jnp.dot(p.astype(BF16), v, preferred_element_type=F32)
            m_scr[c] = m_new

    @pl.when(kj < qi)
    def _():
        update(False)

    @pl.when(kj == qi)
    def _():
        update(True)
        lp = lam_ref[...]
        lam = (jnp.exp(jnp.sum(lp[0:1] * lp[1:2], axis=-1, keepdims=True))
               - jnp.exp(jnp.sum(lp[2:3] * lp[3:4], axis=-1, keepdims=True)) + lam_init)
        o = acc_scr[0] / l_scr[0] - lam * (acc_scr[1] / l_scr[1])
        o_ref[...] = (_rms(o, sg_ref[...]) * (1.0 - lam_init)).astype(BF16)


def _attn(q, k, v, lam_p, sg, bsz, seq, lam_init):
    nq = seq // TQ
    pairs = [(a, b) for a in range(nq) for b in range(a + 1)]
    qt = jnp.asarray(np.array([a for a, _ in pairs], np.int32))
    kt = jnp.asarray(np.array([b for _, b in pairs], np.int32))
    qspec = pl.BlockSpec((TQ, LANES), lambda b, h, s, qt_, kt_: (b * nq + qt_[s], h))
    kspec = pl.BlockSpec((TQ, LANES), lambda b, h, s, qt_, kt_: (b * nq + kt_[s], h))
    const = lambda a, c: pl.BlockSpec((a, c), lambda b, h, s, qt_, kt_: (0, 0))
    grid_spec = pltpu.PrefetchScalarGridSpec(
        num_scalar_prefetch=2,
        grid=(bsz, DIFF_HEADS, len(pairs)),
        in_specs=[const(4, DIFF_DH), const(1, DIFF_DV), qspec, kspec, kspec],
        out_specs=qspec,
        scratch_shapes=[pltpu.VMEM((2, TQ, 1), F32), pltpu.VMEM((2, TQ, 1), F32),
                        pltpu.VMEM((2, TQ, DIFF_DV), F32)],
    )
    return pl.pallas_call(
        functools.partial(_attn_kernel, lam_init=lam_init),
        grid_spec=grid_spec,
        out_shape=jax.ShapeDtypeStruct(q.shape, BF16),
        compiler_params=_cparams(("parallel", "parallel", "arbitrary")),
        name="diffattn",
    )(qt, kt, lam_p, sg, q, k, v)


def _gla_kernel(q_ref, k_ref, v_ref, r_ref, la_ref, tri_ref, gmat_ref, ng_ref, o_ref, st):
    @pl.when(pl.program_id(1) == 0)
    def _():
        st[...] = jnp.zeros(st.shape, F32)

    hi = lax.Precision.HIGHEST
    cum = jnp.dot(tri_ref[...], la_ref[...], preferred_element_type=F32, precision=hi)
    rv = lax.broadcasted_iota(jnp.int32, (GLA_V, GLA_QK), 0) // GLA_DV
    ck = lax.broadcasted_iota(jnp.int32, (GLA_V, GLA_QK), 1) // GLA_DK
    same_head = rv == ck
    state = st[...]
    for c in range(GLA_TG // CHUNK):
        sl = slice(c * CHUNK, (c + 1) * CHUNK)
        cum_c = cum[sl]
        tot = cum_c[CHUNK - 1:CHUNK]
        kdec = k_ref[sl, :] * jnp.exp(tot - cum_c)
        kv_t = lax.dot_general(v_ref[sl, :], kdec, (((0,), (0,)), ((), ())),
                               preferred_element_type=F32, precision=hi)
        state = state * jnp.exp(tot) + jnp.where(same_head, kv_t, 0.0)
        o = lax.dot_general(q_ref[sl, :], state, (((1,), (1,)), ((), ())),
                            preferred_element_type=F32, precision=hi)
        ms = jnp.dot(o * o, gmat_ref[...], preferred_element_type=F32, precision=hi)
        o_ref[sl, :] = (o * lax.rsqrt(ms + EPS) * ng_ref[...] * r_ref[sl, :]).astype(BF16)
    st[...] = state


def _gla(gq, gk, gv, gr, la, tri, gmat, ng, bsz, seq):
    nt = seq // GLA_TG
    row = lambda w_: pl.BlockSpec((GLA_TG, w_), lambda bi, ti: (bi * nt + ti, 0))
    const = lambda a, c: pl.BlockSpec((a, c), lambda bi, ti: (0, 0))
    return pl.pallas_call(
        _gla_kernel,
        grid=(bsz, nt),
        in_specs=[row(GLA_QK), row(GLA_QK), row(GLA_V), row(GLA_V), row(GLA_QK),
                  const(GLA_TG, GLA_TG), const(GLA_V, GLA_V), const(1, GLA_V)],
        out_specs=row(GLA_V),
        out_shape=jax.ShapeDtypeStruct(gv.shape, BF16),
        scratch_shapes=[pltpu.VMEM((GLA_V, GLA_QK), F32)],
        compiler_params=_cparams(("arbitrary", "arbitrary")),
        name="gla",
    )(gq, gk, gv, gr, la, tri, gmat, ng)


def _outproj_kernel(h_ref, yc_ref, yd_ref, yg_ref, w_ref, o_ref):
    acc = jnp.dot(yc_ref[...], w_ref[0:256, :], preferred_element_type=F32)
    acc = acc + jnp.dot(yd_ref[...], w_ref[256:768, :], preferred_element_type=F32)
    acc = acc + jnp.dot(yg_ref[...], w_ref[768:1024, :], preferred_element_type=F32)
    o_ref[...] = h_ref[...] + acc


def _outproj(h, yc, yd, yg, w):
    t = h.shape[0]
    row = lambda w_: pl.BlockSpec((TM, w_), lambda i: (i, 0))
    return pl.pallas_call(
        _outproj_kernel,
        grid=(t // TM,),
        in_specs=[row(D_MODEL), row(CONV_CH), row(DIFF_W), row(GLA_V),
                  pl.BlockSpec((D_MODEL, D_MODEL), lambda i: (0, 0))],
        out_specs=row(D_MODEL),
        out_shape=jax.ShapeDtypeStruct(h.shape, F32),
        compiler_params=_cparams(("parallel",)),
        name="outproj",
    )(h, yc, yd, yg, w)


def _ffn_kernel(be_ref, nu_ref, x_ref, g_ref, wg_ref, wu_ref, wd_ref, o_ref, xb, acc,
                *, norm_residual):
    i = pl.program_id(0)
    j = pl.program_id(1)
    nj = pl.num_programs(1)

    @pl.when(i < nu_ref[0])
    def _():
        @pl.when(j == 0)
        def _():
            x = x_ref[...]
            if norm_residual:
                x = _rms(x, g_ref[...])
            xb[...] = x.astype(BF16)
            acc[...] = jnp.zeros(acc.shape, F32)

        x = xb[...]
        g = jnp.dot(x, wg_ref[0], preferred_element_type=F32)
        u = jnp.dot(x, wu_ref[0], preferred_element_type=F32)
        a = (_silu(g) * u).astype(BF16)
        acc[...] += jnp.dot(a, wd_ref[0], preferred_element_type=F32)

    @pl.when(j == nj - 1)
    def _():
        if norm_residual:
            o_ref[...] = x_ref[...] + acc[...]
        else:
            o_ref[...] = jnp.where(i < nu_ref[0], acc[...], 0.0)


def _ffn(x, g, w_gu, w_down, block_e, n_used, blk, fb, norm_residual):
    rows = x.shape[0]
    f = w_down.shape[1]
    nj = f // fb

    def last(i, nu):
        return jnp.minimum(i, nu[0] - 1)

    xspec = pl.BlockSpec((blk, D_MODEL), lambda i, j, be, nu: (last(i, nu), 0))
    wg = pl.BlockSpec((1, D_MODEL, fb), lambda i, j, be, nu: (be[i], 0, j))
    wu = pl.BlockSpec((1, D_MODEL, fb), lambda i, j, be, nu: (be[i], 0, j + nj))
    wd = pl.BlockSpec((1, fb, D_MODEL), lambda i, j, be, nu: (be[i], j, 0))
    grid_spec = pltpu.PrefetchScalarGridSpec(
        num_scalar_prefetch=2,
        grid=(rows // blk, nj),
        in_specs=[xspec, pl.BlockSpec((1, D_MODEL), lambda i, j, be, nu: (0, 0)), wg, wu, wd],
        out_specs=pl.BlockSpec((blk, D_MODEL), lambda i, j, be, nu: (i, 0)),
        scratch_shapes=[pltpu.VMEM((blk, D_MODEL), BF16), pltpu.VMEM((blk, D_MODEL), F32)],
    )
    return pl.pallas_call(
        functools.partial(_ffn_kernel, norm_residual=norm_residual),
        grid_spec=grid_spec,
        out_shape=jax.ShapeDtypeStruct((rows, D_MODEL), F32),
        compiler_params=_cparams(("arbitrary", "arbitrary")),
        name="ffn" if norm_residual else "experts",
    )(block_e, n_used, x, g, w_gu, w_gu, w_down)


def _router_kernel(h_ref, g_ref, rw_ref, tri_ref, hn_ref, info_ref, cnt_ref, carry):
    @pl.when(pl.program_id(0) == 0)
    def _():
        carry[...] = jnp.zeros(carry.shape, F32)

    hn = _rms(h_ref[...], g_ref[...])
    hn_ref[...] = hn
    logits = jnp.dot(hn, rw_ref[...], preferred_element_type=F32, precision=lax.Precision.HIGHEST)
    lane = lax.broadcasted_iota(jnp.int32, logits.shape, 1)
    lg = jnp.where(lane < N_EXPERTS, logits, -jnp.inf)
    v1 = jnp.max(lg, axis=-1, keepdims=True)
    i1 = jnp.min(jnp.where(lg == v1, lane, LANES), axis=-1, keepdims=True)
    lg2 = jnp.where(lane == i1, -jnp.inf, lg)
    v2 = jnp.max(lg2, axis=-1, keepdims=True)
    i2 = jnp.min(jnp.where(lg2 == v2, lane, LANES), axis=-1, keepdims=True)
    e2 = jnp.exp(v2 - v1)
    g1 = 1.0 / (1.0 + e2)
    g2 = e2 / (1.0 + e2)
    pick1 = lane == i1
    pick2 = lane == i2
    onehot = jnp.where(pick1 | pick2, 1.0, 0.0)
    before = jnp.dot(tri_ref[...], onehot.astype(BF16), preferred_element_type=F32) + carry[...]
    r1 = jnp.sum(jnp.where(pick1, before, 0.0), axis=-1, keepdims=True)
    r2 = jnp.sum(jnp.where(pick2, before, 0.0), axis=-1, keepdims=True)
    carry[...] += jnp.sum(onehot, axis=0, keepdims=True)
    info = jnp.where(lane == 0, i1.astype(F32), 0.0)
    info = jnp.where(lane == 1, i2.astype(F32), info)
    info = jnp.where(lane == 2, r1, info)
    info = jnp.where(lane == 3, r2, info)
    info = jnp.where(lane == 4, g1, info)
    info = jnp.where(lane == 5, g2, info)
    info_ref[...] = info
    cnt_ref[...] = jnp.broadcast_to(carry[...], cnt_ref.shape)


def _router(h, g, rw, tri):
    t = h.shape[0]
    row = lambda w_: pl.BlockSpec((TM, w_), lambda i: (i, 0))
    const = lambda a, c: pl.BlockSpec((a, c), lambda i: (0, 0))
    return pl.pallas_call(
        _router_kernel,
        grid=(t // TM,),
        in_specs=[row(D_MODEL), const(1, D_MODEL), const(D_MODEL, LANES), const(TM, TM)],
        out_specs=[row(D_MODEL), row(LANES), const(SUBLANES, LANES)],
        out_shape=[jax.ShapeDtypeStruct((t, D_MODEL), F32), jax.ShapeDtypeStruct((t, LANES), F32),
                   jax.ShapeDtypeStruct((SUBLANES, LANES), F32)],
        scratch_shapes=[pltpu.VMEM((1, LANES), F32)],
        compiler_params=_cparams(("arbitrary",)),
        name="router",
    )(h, g, rw, tri)


def _row_copy(src, dst, sem):
    return pltpu.make_async_copy(src, dst, sem)


def _dispatch_kernel(dest_ref, hn_ref, xs_in_ref, xs_ref, sem):
    del xs_in_ref

    def issue(t, carry):
        _row_copy(hn_ref.at[t], xs_ref.at[dest_ref[0, 0, 2 * t]], sem).start()
        _row_copy(hn_ref.at[t], xs_ref.at[dest_ref[0, 0, 2 * t + 1]], sem).start()
        return carry

    lax.fori_loop(0, TMD, issue, 0)

    def drain(t, carry):
        _row_copy(hn_ref.at[0], xs_ref.at[0], sem).wait()
        _row_copy(hn_ref.at[0], xs_ref.at[0], sem).wait()
        return carry

    lax.fori_loop(0, TMD, drain, 0)


def _dispatch(dest, hn3, xs_zero):
    t = hn3.shape[0]
    return pl.pallas_call(
        _dispatch_kernel,
        grid=(t // TMD,),
        in_specs=[pl.BlockSpec((1, 1, 2 * TMD), lambda i: (i, 0, 0), memory_space=pltpu.SMEM),
                  pl.BlockSpec((TMD, ROW_SPLIT, LANES), lambda i: (i, 0, 0)),
                  pl.BlockSpec(memory_space=pl.ANY)],
        out_specs=pl.BlockSpec(memory_space=pl.ANY),
        out_shape=jax.ShapeDtypeStruct(xs_zero.shape, F32),
        scratch_shapes=[pltpu.SemaphoreType.DMA(())],
        input_output_aliases={2: 0},
        compiler_params=_cparams(("arbitrary",)),
        name="dispatch",
    )(dest, hn3, xs_zero)


def _combine_kernel(dest_ref, h_ref, info_ref, ys_ref, o_ref, buf0, buf1, sem):
    def issue(t, carry):
        _row_copy(ys_ref.at[dest_ref[0, 0, 2 * t]], buf0.at[t], sem).start()
        _row_copy(ys_ref.at[dest_ref[0, 0, 2 * t + 1]], buf1.at[t], sem).start()
        return carry

    lax.fori_loop(0, TMD, issue, 0)

    def drain(t, carry):
        _row_copy(ys_ref.at[0], buf0.at[0], sem).wait()
        _row_copy(ys_ref.at[0], buf1.at[0], sem).wait()
        return carry

    lax.fori_loop(0, TMD, drain, 0)
    g1 = info_ref[:, 4:5]
    g2 = info_ref[:, 5:6]
    for s in range(ROW_SPLIT):
        cols = slice(s * LANES, (s + 1) * LANES)
        o_ref[:, cols] = h_ref[:, cols] + (g1 * buf0[:, s, :] + g2 * buf1[:, s, :])


def _combine(dest, h, info, ys3):
    t = h.shape[0]
    row = lambda w_: pl.BlockSpec((TMD, w_), lambda i: (i, 0))
    return pl.pallas_call(
        _combine_kernel,
        grid=(t // TMD,),
        in_specs=[pl.BlockSpec((1, 1, 2 * TMD), lambda i: (i, 0, 0), memory_space=pltpu.SMEM),
                  row(D_MODEL), row(LANES), pl.BlockSpec(memory_space=pl.ANY)],
        out_specs=row(D_MODEL),
        out_shape=jax.ShapeDtypeStruct(h.shape, F32),
        scratch_shapes=[pltpu.VMEM((TMD, ROW_SPLIT, LANES), F32),
                        pltpu.VMEM((TMD, ROW_SPLIT, LANES), F32),
                        pltpu.SemaphoreType.DMA(())],
        compiler_params=_cparams(("arbitrary",)),
        name="combine",
    )(dest, h, info, ys3)


def _moe(h1, g, rw, w_gu, w_down, tri_strict):
    t = h1.shape[0]
    hn, info, cnt = _router(h1, g, rw, tri_strict)
    counts = cnt[0, :N_EXPERTS].astype(jnp.int32)
    padded = ((counts + MOE_BLK - 1) // MOE_BLK) * MOE_BLK
    pend = jnp.cumsum(padded)
    pstart = pend - padded
    e = info[:, 0:2].astype(jnp.int32)
    r = info[:, 2:4].astype(jnp.int32)
    onehot = e[:, :, None] == jnp.arange(N_EXPERTS, dtype=jnp.int32)[None, None, :]
    dest = r + jnp.sum(jnp.where(onehot, pstart[None, None, :], 0), axis=-1)
    dest = dest.reshape(t // TMD, 1, 2 * TMD)
    m_rows = 2 * t + N_EXPERTS * MOE_BLK
    nb = m_rows // MOE_BLK
    blk_start = jnp.arange(nb, dtype=jnp.int32) * MOE_BLK
    block_e = jnp.minimum(jnp.sum(blk_start[:, None] >= pend[None, :], axis=-1), N_EXPERTS - 1)
    n_used = (pend[-1:] // MOE_BLK).astype(jnp.int32)

    xs3 = _dispatch(dest, hn.reshape(t, ROW_SPLIT, LANES), jnp.zeros((m_rows, ROW_SPLIT, LANES), F32))
    ys = _ffn(xs3.reshape(m_rows, D_MODEL), g, w_gu, w_down, block_e.astype(jnp.int32), n_used,
              MOE_BLK, MOE_FB, norm_residual=False)
    return _combine(dest, h1, info, ys.reshape(m_rows, ROW_SPLIT, LANES))


def _ple_kernel(h_ref, p_ref, g_ref, wg_ref, wu_ref, fg_ref, o_ref, *, final):
    h = h_ref[...]
    hn = _rms(h, g_ref[...]).astype(BF16)
    gate = _sigmoid(jnp.dot(hn, wg_ref[...], preferred_element_type=F32))
    up = jnp.dot(p_ref[...].astype(BF16), wu_ref[...], preferred_element_type=F32)
    out = h + gate * up
    if final:
        out = _rms(out, fg_ref[...])
    o_ref[...] = out


def _ple(h, p, g, wg, wu, fg, final):
    t = h.shape[0]
    row = lambda w_: pl.BlockSpec((TM, w_), lambda i: (i, 0))
    const = lambda a, c: pl.BlockSpec((a, c), lambda i: (0, 0))
    return pl.pallas_call(
        functools.partial(_ple_kernel, final=final),
        grid=(t // TM,),
        in_specs=[row(D_MODEL), row(PLE_DIM), const(1, D_MODEL), const(D_MODEL, D_MODEL),
                  const(PLE_DIM, D_MODEL), const(1, D_MODEL)],
        out_specs=row(D_MODEL),
        out_shape=jax.ShapeDtypeStruct(h.shape, F32),
        compiler_params=_cparams(("parallel",)),
        name="ple",
    )(h, p, g, wg, wu, fg)


def _rope_tables(seq):
    half = ROT_DIMS // 2
    pos = jnp.arange(seq, dtype=F32)
    inv_freq = ROPE_THETA ** (-jnp.arange(0, ROT_DIMS, 2, dtype=F32) / ROT_DIMS)
    ang = pos[:, None] * inv_freq[None, :]
    cos, sin = jnp.cos(ang), jnp.sin(ang)
    lane = np.arange(LANES) % DIFF_DH
    fidx = lane % half
    first = jnp.asarray(lane < half)[None, :]
    second = jnp.asarray((lane >= half) & (lane < ROT_DIMS))[None, :]
    ra = jnp.where(first | second, cos[:, fidx], 1.0)
    rb = jnp.where(second, sin[:, fidx], 0.0)
    rc = jnp.where(first, -sin[:, fidx], 0.0)
    return ra, rb, rc


def kernel(x, p, norm_mix_g, w_in, conv_w, conv_b, conv_ln_g, conv_ln_b, diff_lambda, diff_subln_g,
           gla_w_gate2, gla_b_gate, gla_norm_g, w_out, norm_ffn_g, ffn_w_gu, ffn_w_down, router_w,
           moe_w_gu, moe_w_down, ple_w_up, ple_w_gate, ple_norm_g, final_norm_g):
    bsz, seq, d = x.shape
    depth = w_in.shape[0]
    t = bsz * seq
    assert d == D_MODEL and seq % TM == 0 and seq % TQ == 0 and t % TMD == 0

    ropes = _rope_tables(seq)
    idx = np.arange(GLA_TG)
    tri_chunk = jnp.asarray(((idx[:, None] >= idx[None, :])
                             & (idx[:, None] // CHUNK == idx[None, :] // CHUNK)).astype(np.float32))
    vi = np.arange(GLA_V)
    gmat = jnp.asarray((vi[:, None] // GLA_DV == vi[None, :] // GLA_DV).astype(np.float32) / GLA_DV)
    ti = np.arange(TM)
    tri_strict = jnp.asarray((ti[:, None] > ti[None, :]).astype(np.float32)).astype(BF16)
    zero_e = jnp.zeros((t // TM,), jnp.int32)
    all_used = jnp.full((1,), t // TM, jnp.int32)

    h = x.reshape(t, D_MODEL)
    for i in range(depth):
        lam_init = 0.8 - 0.6 * math.exp(-0.3 * i)
        w_in_i = jnp.pad(w_in[i], ((0, 0), (0, D_IN_PAD - D_IN))).astype(BF16)
        wg2 = jnp.pad(gla_w_gate2[i], ((0, LANES - GLA_GATE_RANK), (0, 0)))
        u, dq, dk, dv, gq, gk, gv, gr, la = _inproj(
            h, norm_mix_g[i][None, :], w_in_i, ropes, wg2, gla_b_gate[i][None, :], seq)
        y_conv = _conv(u, conv_w[i], conv_b[i][None, :], conv_ln_g[i][None, :],
                       conv_ln_b[i][None, :], bsz, seq)
        y_diff = _attn(dq, dk, dv, diff_lambda[i], diff_subln_g[i][None, :], bsz, seq, lam_init)
        y_gla = _gla(gq, gk, gv, gr, la, tri_chunk, gmat,
                     jnp.tile(gla_norm_g[i], GLA_HEADS)[None, :], bsz, seq)
        h = _outproj(h, y_conv, y_diff, y_gla, w_out[i].astype(BF16))

        g_ffn = norm_ffn_g[i][None, :]
        if i % 2 == 0:
            j = i // 2
            h = _ffn(h, g_ffn, ffn_w_gu[j][None].astype(BF16), ffn_w_down[j][None].astype(BF16),
                     zero_e, all_used, TM, FFN_FB, norm_residual=True)
        else:
            j = i // 2
            rw = jnp.pad(router_w[j], ((0, 0), (0, LANES - N_EXPERTS)))
            h = _moe(h, g_ffn, rw, moe_w_gu[j].astype(BF16), moe_w_down[j].astype(BF16), tri_strict)

        h = _ple(h, p[i].reshape(t, PLE_DIM), ple_norm_g[i][None, :], ple_w_gate[i].astype(BF16),
                 ple_w_up[i].astype(BF16), final_norm_g[None, :], final=(i == depth - 1))
    return h.reshape(bsz, seq, D_MODEL)
```

```python
import functools
import math

import jax
import jax.numpy as jnp
import numpy as np
from jax import lax
from jax.experimental import pallas as pl
from jax.experimental.pallas import tpu as pltpu

F32 = jnp.float32
BF16 = jnp.bfloat16

D_MODEL = 1024
CHUNK = 64
CONV_CH = 256
CONV_WIDTH = 31
DIFF_HEADS = 4
DIFF_DV = 128
DIFF_DH = 64
GLA_HEADS = 4
GLA_DV = 64
GLA_DK = 32
GLA_GATE_RANK = 16
GLA_TAU = 16.0
ROPE_THETA = 500000.0
ROT_DIMS = 16
D_FF = 2816
N_EXPERTS = 8
D_FF_EXPERT = 3584
PLE_DIM = 256
EPS = 1e-6
DIFF_W = 512
GLA_QK = GLA_HEADS * GLA_DK
GLA_V = GLA_HEADS * GLA_DV
D_IN = 2832
D_IN_PAD = 2944

LANES = 128
SUBLANES = 8
VMEM_LIMIT = 56 * 1024 * 1024
LOG2E = math.log2(math.e)

TM = 512
TQ = 512
CONV_TS = 512
CONV_HALO = 32
CONV_SH_ROWS = CONV_TS + CONV_HALO - SUBLANES
GLA_TG = 512
FFN_FB = 1408
MOE_BLK = 512
MOE_FB = 896
TMD = 256
DMA_UNROLL = 8


def _cparams(sem):
    return pltpu.CompilerParams(dimension_semantics=sem, vmem_limit_bytes=VMEM_LIMIT)


def _rms(x, g):
    ms = jnp.mean(x * x, axis=-1, keepdims=True)
    return x * lax.rsqrt(ms + EPS) * g


def _sigmoid(x):
    return 1.0 / (1.0 + jnp.exp(-x))


def _silu(x):
    return x * _sigmoid(x)


def _rope(x, ra, rb, rc):
    outs = []
    for c in range(x.shape[1] // LANES):
        xc = x[:, c * LANES:(c + 1) * LANES]
        outs.append(xc * ra + pltpu.roll(xc, ROT_DIMS // 2, 1) * rb
                    + pltpu.roll(xc, LANES - ROT_DIMS // 2, 1) * rc)
    return jnp.concatenate(outs, axis=1)


def _inproj_kernel(h_ref, g_ref, w_ref, ra_ref, rb_ref, rc_ref, wg2_ref, bg_ref,
                   u_ref, q_ref, k_ref, v_ref, gq_ref, gk_ref, gv_ref, gr_ref, la_ref):
    hn = _rms(h_ref[...], g_ref[...]).astype(BF16)

    def proj(a, b):
        return jnp.dot(hn, w_ref[:, a:b], preferred_element_type=F32)

    ra, rb, rc = ra_ref[...], rb_ref[...], rc_ref[...]
    u_ref[...] = proj(0, 256) * _sigmoid(proj(256, 512))
    q_ref[...] = (_rope(proj(512, 1024), ra, rb, rc) * (LOG2E * DIFF_DH ** -0.5)).astype(BF16)
    k_ref[...] = _rope(proj(1024, 1536), ra, rb, rc).astype(BF16)
    v_ref[...] = proj(1536, 2048).astype(BF16)
    gq_ref[...] = proj(2048, 2176) * (GLA_DK ** -0.5)
    gk_ref[...] = proj(2176, 2304)
    gv_ref[...] = proj(2304, 2560)
    gr_ref[...] = _silu(proj(2560, 2816))
    gz = proj(2816, D_IN_PAD)
    ga = jnp.dot(gz, wg2_ref[...], preferred_element_type=F32,
                 precision=lax.Precision.HIGHEST) + bg_ref[...]
    la_ref[...] = (jnp.minimum(ga, 0.0) - jnp.log(1.0 + jnp.exp(-jnp.abs(ga)))) * (1.0 / GLA_TAU)


def _inproj(h, g, w, ropes, wg2, bg, seq):
    t = h.shape[0]
    nseq = seq // TM
    row = lambda w_: pl.BlockSpec((TM, w_), lambda i: (i, 0))
    const = lambda a, b: pl.BlockSpec((a, b), lambda i: (0, 0))
    rope_spec = pl.BlockSpec((TM, LANES), lambda i: (i % nseq, 0))
    outs = [(CONV_CH, F32), (DIFF_W, BF16), (DIFF_W, BF16), (DIFF_W, BF16),
            (GLA_QK, F32), (GLA_QK, F32), (GLA_V, F32), (GLA_V, F32), (GLA_QK, F32)]
    return pl.pallas_call(
        _inproj_kernel,
        grid=(t // TM,),
        in_specs=[row(D_MODEL), const(1, D_MODEL), const(D_MODEL, D_IN_PAD),
                  rope_spec, rope_spec, rope_spec, const(LANES, LANES), const(1, LANES)],
        out_specs=[row(w_) for w_, _ in outs],
        out_shape=[jax.ShapeDtypeStruct((t, w_), dt) for w_, dt in outs],
        compiler_params=_cparams(("parallel",)),
        name="inproj",
    )(h, g, w, *ropes, wg2, bg)


def _conv_kernel(u_ref, w_ref, b_ref, lg_ref, lb_ref, o_ref, win, shifted):
    @pl.when(pl.program_id(1) == 0)
    def _():
        win[0:CONV_HALO, :] = jnp.zeros((CONV_HALO, CONV_CH), F32)

    win[CONV_HALO:, :] = u_ref[...]
    for r in range(1, SUBLANES):
        shifted[r - 1] = win[r:r + CONV_SH_ROWS, :]
    rows = 64
    base = CONV_HALO - (CONV_WIDTH - 1)
    for r0 in range(0, CONV_TS, rows):
        acc = jnp.zeros((rows, CONV_CH), F32) + b_ref[...]
        for j in range(CONV_WIDTH):
            res = (base + j) % SUBLANES
            start = r0 + (base + j) - res
            if res == 0:
                tap = win[start:start + rows, :]
            else:
                tap = shifted[res - 1, start:start + rows, :]
            acc = acc + tap * w_ref[j:j + 1, :]
        mu = jnp.mean(acc, axis=-1, keepdims=True)
        xc = acc - mu
        var = jnp.mean(xc * xc, axis=-1, keepdims=True)
        y = xc * lax.rsqrt(var + EPS) * lg_ref[...] + lb_ref[...]
        o_ref[r0:r0 + rows, :] = _silu(y).astype(BF16)
    win[0:CONV_HALO, :] = win[CONV_TS:CONV_TS + CONV_HALO, :]


def _conv(u, w, b, lg, lb, bsz, seq):
    nt = seq // CONV_TS
    const = lambda a, c: pl.BlockSpec((a, c), lambda bi, ti: (0, 0))
    return pl.pallas_call(
        _conv_kernel,
        grid=(bsz, nt),
        in_specs=[pl.BlockSpec((CONV_TS, CONV_CH), lambda bi, ti: (bi * nt + ti, 0)),
                  const(CONV_WIDTH, CONV_CH), const(1, CONV_CH), const(1, CONV_CH), const(1, CONV_CH)],
        out_specs=pl.BlockSpec((CONV_TS, CONV_CH), lambda bi, ti: (bi * nt + ti, 0)),
        out_shape=jax.ShapeDtypeStruct(u.shape, BF16),
        scratch_shapes=[pltpu.VMEM((CONV_TS + CONV_HALO, CONV_CH), F32),
                        pltpu.VMEM((SUBLANES - 1, CONV_SH_ROWS, CONV_CH), F32)],
        compiler_params=_cparams(("arbitrary", "arbitrary")),
        name="conv",
    )(u, w, b, lg, lb)


def _attn_kernel(qt_ref, kt_ref, lam_ref, sg_ref, q_ref, k_ref, v_ref, o_ref, m_scr, acc_scr,
                 *, lam_init):
    step = pl.program_id(1)
    qi = qt_ref[step]
    kj = kt_ref[step]

    @pl.when(kj == 0)
    def _():
        m_scr[...] = jnp.full(m_scr.shape, -jnp.inf, F32)
        acc_scr[...] = jnp.zeros(acc_scr.shape, F32)

    def update(masked):
        if masked:
            rq = lax.broadcasted_iota(jnp.int32, (TQ, TQ), 0) // CHUNK
            ck = lax.broadcasted_iota(jnp.int32, (TQ, TQ), 1) // CHUNK
            allowed = ck <= rq
        for h in range(DIFF_HEADS):
            hs = slice(h * LANES, (h + 1) * LANES)
            q = q_ref[:, hs]
            k = k_ref[:, hs]
            v = v_ref[:, hs]
            vext = jnp.concatenate([v, jnp.ones_like(v)], axis=1)
            lane = lax.broadcasted_iota(jnp.int32, q.shape, 1)
            for c in range(2):
                sel = (lane < DIFF_DH) if c == 0 else (lane >= DIFF_DH)
                qc = jnp.where(sel, q, jnp.zeros_like(q))
                s = lax.dot_general(qc, k, (((1,), (1,)), ((), ())), preferred_element_type=F32)
                if masked:
                    s = jnp.where(allowed, s, -jnp.inf)
                m_old = m_scr[2 * h + c]
                m_new = jnp.maximum(m_old, jnp.max(s, axis=-1, keepdims=True))
                alpha = jnp.exp2(m_old - m_new)
                p = jnp.exp2(s - jnp.tile(m_new, (1, TQ // LANES)))
                pv = jnp.dot(p.astype(BF16), vext, preferred_element_type=F32)
                acc_scr[2 * h + c] = jnp.tile(alpha, (1, 2)) * acc_scr[2 * h + c] + pv
                m_scr[2 * h + c] = m_new

    @pl.when(kj < qi)
    def _():
        update(False)

    @pl.when(kj == qi)
    def _():
        update(True)
        lp = lam_ref[...]
        lam = (jnp.exp(jnp.sum(lp[0:1] * lp[1:2], axis=-1, keepdims=True))
               - jnp.exp(jnp.sum(lp[2:3] * lp[3:4], axis=-1, keepdims=True)) + lam_init)
        for h in range(DIFF_HEADS):
            a0 = acc_scr[2 * h]
            a1 = acc_scr[2 * h + 1]
            o = a0[:, :DIFF_DV] / a0[:, DIFF_DV:] - lam * (a1[:, :DIFF_DV] / a1[:, DIFF_DV:])
            o_ref[:, h * LANES:(h + 1) * LANES] = (_rms(o, sg_ref[...]) * (1.0 - lam_init)).astype(BF16)


def _attn(q, k, v, lam_p, sg, bsz, seq, lam_init):
    nq = seq // TQ
    pairs = [(a, b) for a in range(nq) for b in range(a + 1)]
    qt = jnp.asarray(np.array([a for a, _ in pairs], np.int32))
    kt = jnp.asarray(np.array([b for _, b in pairs], np.int32))
    qspec = pl.BlockSpec((TQ, DIFF_W), lambda b, s, qt_, kt_: (b * nq + qt_[s], 0))
    kspec = pl.BlockSpec((TQ, DIFF_W), lambda b, s, qt_, kt_: (b * nq + kt_[s], 0))
    const = lambda a, c: pl.BlockSpec((a, c), lambda b, s, qt_, kt_: (0, 0))
    grid_spec = pltpu.PrefetchScalarGridSpec(
        num_scalar_prefetch=2,
        grid=(bsz, len(pairs)),
        in_specs=[const(4, DIFF_DH), const(1, DIFF_DV), qspec, kspec, kspec],
        out_specs=qspec,
        scratch_shapes=[pltpu.VMEM((2 * DIFF_HEADS, TQ, LANES), F32),
                        pltpu.VMEM((2 * DIFF_HEADS, TQ, 2 * DIFF_DV), F32)],
    )
    return pl.pallas_call(
        functools.partial(_attn_kernel, lam_init=lam_init),
        grid_spec=grid_spec,
        out_shape=jax.ShapeDtypeStruct(q.shape, BF16),
        compiler_params=_cparams(("parallel", "arbitrary")),
        name="diffattn",
    )(qt, kt, lam_p, sg, q, k, v)


def _gla_kernel(q_ref, k_ref, v_ref, r_ref, la_ref, tri_ref, gmat_ref, mask_ref, ng_ref, o_ref, st):
    @pl.when(pl.program_id(1) == 0)
    def _():
        st[...] = jnp.zeros(st.shape, F32)

    la = la_ref[...]
    la_hi = la.astype(BF16)
    la_lo = (la - la_hi.astype(F32)).astype(BF16)
    cum = (jnp.dot(tri_ref[...], la_hi, preferred_element_type=F32)
           + jnp.dot(tri_ref[...], la_lo, preferred_element_type=F32))
    same_head = mask_ref[...] > 0.0
    state = st[...]
    for c in range(GLA_TG // CHUNK):
        sl = slice(c * CHUNK, (c + 1) * CHUNK)
        cum_c = cum[sl]
        tot = cum_c[CHUNK - 1:CHUNK]
        kdec = (k_ref[sl, :] * jnp.exp(tot - cum_c)).astype(BF16)
        v_t = v_ref[sl, :].T.astype(BF16)
        kv_t = jnp.dot(v_t, kdec, preferred_element_type=F32)
        state = state * jnp.exp(tot) + jnp.where(same_head, kv_t, 0.0)
        o = lax.dot_general(q_ref[sl, :].astype(BF16), state.astype(BF16), (((1,), (1,)), ((), ())),
                            preferred_element_type=F32)
        ms = jnp.dot((o * o).astype(BF16), gmat_ref[...], preferred_element_type=F32)
        o_ref[sl, :] = (o * lax.rsqrt(ms + EPS) * ng_ref[...] * r_ref[sl, :]).astype(BF16)
    st[...] = state


def _gla(gq, gk, gv, gr, la, tri, gmat, head_mask, ng, bsz, seq):
    nt = seq // GLA_TG
    row = lambda w_: pl.BlockSpec((GLA_TG, w_), lambda bi, ti: (bi * nt + ti, 0))
    const = lambda a, c: pl.BlockSpec((a, c), lambda bi, ti: (0, 0))
    return pl.pallas_call(
        _gla_kernel,
        grid=(bsz, nt),
        in_specs=[row(GLA_QK), row(GLA_QK), row(GLA_V), row(GLA_V), row(GLA_QK),
                  const(GLA_TG, GLA_TG), const(GLA_V, GLA_V), const(GLA_V, GLA_QK), const(1, GLA_V)],
        out_specs=row(GLA_V),
        out_shape=jax.ShapeDtypeStruct(gv.shape, BF16),
        scratch_shapes=[pltpu.VMEM((GLA_V, GLA_QK), F32)],
        compiler_params=_cparams(("arbitrary", "arbitrary")),
        name="gla",
    )(gq, gk, gv, gr, la, tri, gmat, head_mask, ng)


def _outproj_kernel(h_ref, yc_ref, yd_ref, yg_ref, w_ref, o_ref):
    acc = jnp.dot(yc_ref[...], w_ref[0:256, :], preferred_element_type=F32)
    acc = acc + jnp.dot(yd_ref[...], w_ref[256:768, :], preferred_element_type=F32)
    acc = acc + jnp.dot(yg_ref[...], w_ref[768:1024, :], preferred_element_type=F32)
    o_ref[...] = h_ref[...] + acc


def _outproj(h, yc, yd, yg, w):
    t = h.shape[0]
    row = lambda w_: pl.BlockSpec((TM, w_), lambda i: (i, 0))
    return pl.pallas_call(
        _outproj_kernel,
        grid=(t // TM,),
        in_specs=[row(D_MODEL), row(CONV_CH), row(DIFF_W), row(GLA_V),
                  pl.BlockSpec((D_MODEL, D_MODEL), lambda i: (0, 0))],
        out_specs=row(D_MODEL),
        out_shape=jax.ShapeDtypeStruct(h.shape, F32),
        compiler_params=_cparams(("parallel",)),
        name="outproj",
    )(h, yc, yd, yg, w)


def _ffn_kernel(be_ref, nu_ref, x_ref, g_ref, wg_ref, wu_ref, wd_ref, o_ref, xb, acc,
                *, norm_residual):
    i = pl.program_id(0)
    j = pl.program_id(1)
    nj = pl.num_programs(1)

    @pl.when(i < nu_ref[0])
    def _():
        @pl.when(j == 0)
        def _():
            x = x_ref[...]
            if norm_residual:
                x = _rms(x, g_ref[...])
            xb[...] = x.astype(BF16)
            acc[...] = jnp.zeros(acc.shape, F32)

        x = xb[...]
        g = jnp.dot(x, wg_ref[0], preferred_element_type=F32)
        u = jnp.dot(x, wu_ref[0], preferred_element_type=F32)
        a = (_silu(g) * u).astype(BF16)
        acc[...] += jnp.dot(a, wd_ref[0], preferred_element_type=F32)

    @pl.when(j == nj - 1)
    def _():
        if norm_residual:
            o_ref[...] = x_ref[...] + acc[...]
        else:
            o_ref[...] = jnp.where(i < nu_ref[0], acc[...], 0.0)


def _ffn(x, g, w_gu, w_down, block_e, n_used, blk, fb, norm_residual):
    rows = x.shape[0]
    f = w_down.shape[1]
    nj = f // fb

    def last(i, nu):
        return jnp.minimum(i, nu[0] - 1)

    def hid(i, j, nu):
        return jnp.where(i < nu[0], j, nj - 1)

    xspec = pl.BlockSpec((blk, D_MODEL), lambda i, j, be, nu: (last(i, nu), 0))
    wg = pl.BlockSpec((1, D_MODEL, fb), lambda i, j, be, nu: (be[last(i, nu)], 0, hid(i, j, nu)))
    wu = pl.BlockSpec((1, D_MODEL, fb), lambda i, j, be, nu: (be[last(i, nu)], 0, hid(i, j, nu) + nj))
    wd = pl.BlockSpec((1, fb, D_MODEL), lambda i, j, be, nu: (be[last(i, nu)], hid(i, j, nu), 0))
    grid_spec = pltpu.PrefetchScalarGridSpec(
        num_scalar_prefetch=2,
        grid=(rows // blk, nj),
        in_specs=[xspec, pl.BlockSpec((1, D_MODEL), lambda i, j, be, nu: (0, 0)), wg, wu, wd],
        out_specs=pl.BlockSpec((blk, D_MODEL), lambda i, j, be, nu: (i, 0)),
        scratch_shapes=[pltpu.VMEM((blk, D_MODEL), BF16), pltpu.VMEM((blk, D_MODEL), F32)],
    )
    return pl.pallas_call(
        functools.partial(_ffn_kernel, norm_residual=norm_residual),
        grid_spec=grid_spec,
        out_shape=jax.ShapeDtypeStruct((rows, D_MODEL), F32),
        compiler_params=_cparams(("arbitrary", "arbitrary")),
        name="ffn" if norm_residual else "experts",
    )(block_e, n_used, x, g, w_gu, w_gu, w_down)


def _router_kernel(h_ref, g_ref, rw_ref, tri_ref, hn_ref, info_ref, cnt_ref, carry):
    @pl.when(pl.program_id(0) == 0)
    def _():
        carry[...] = jnp.zeros(carry.shape, F32)

    hn = _rms(h_ref[...], g_ref[...])
    hn_ref[...] = hn
    logits = jnp.dot(hn, rw_ref[...], preferred_element_type=F32, precision=lax.Precision.HIGHEST)
    lane = lax.broadcasted_iota(jnp.int32, logits.shape, 1)
    lg = jnp.where(lane < N_EXPERTS, logits, -jnp.inf)
    v1 = jnp.max(lg, axis=-1, keepdims=True)
    i1 = jnp.min(jnp.where(lg == v1, lane, LANES), axis=-1, keepdims=True)
    lg2 = jnp.where(lane == i1, -jnp.inf, lg)
    v2 = jnp.max(lg2, axis=-1, keepdims=True)
    i2 = jnp.min(jnp.where(lg2 == v2, lane, LANES), axis=-1, keepdims=True)
    e2 = jnp.exp(v2 - v1)
    g1 = 1.0 / (1.0 + e2)
    g2 = e2 / (1.0 + e2)
    pick1 = lane == i1
    pick2 = lane == i2
    onehot = jnp.where(pick1 | pick2, 1.0, 0.0)
    before = jnp.dot(tri_ref[...], onehot.astype(BF16), preferred_element_type=F32) + carry[...]
    r1 = jnp.sum(jnp.where(pick1, before, 0.0), axis=-1, keepdims=True)
    r2 = jnp.sum(jnp.where(pick2, before, 0.0), axis=-1, keepdims=True)
    carry[...] += jnp.sum(onehot, axis=0, keepdims=True)
    info = jnp.where(lane == 0, i1.astype(F32), 0.0)
    info = jnp.where(lane == 1, i2.astype(F32), info)
    info = jnp.where(lane == 2, r1, info)
    info = jnp.where(lane == 3, r2, info)
    info = jnp.where(lane == 4, g1, info)
    info = jnp.where(lane == 5, g2, info)
    info_ref[...] = info
    cnt_ref[...] = jnp.broadcast_to(carry[...], cnt_ref.shape)


def _router(h, g, rw, tri):
    t = h.shape[0]
    row = lambda w_: pl.BlockSpec((TM, w_), lambda i: (i, 0))
    const = lambda a, c: pl.BlockSpec((a, c), lambda i: (0, 0))
    return pl.pallas_call(
        _router_kernel,
        grid=(t // TM,),
        in_specs=[row(D_MODEL), const(1, D_MODEL), const(D_MODEL, LANES), const(TM, TM)],
        out_specs=[row(D_MODEL), row(LANES), const(SUBLANES, LANES)],
        out_shape=[jax.ShapeDtypeStruct((t, D_MODEL), F32), jax.ShapeDtypeStruct((t, LANES), F32),
                   jax.ShapeDtypeStruct((SUBLANES, LANES), F32)],
        scratch_shapes=[pltpu.VMEM((1, LANES), F32)],
        compiler_params=_cparams(("arbitrary",)),
        name="router",
    )(h, g, rw, tri)


def _row_copy(src, dst, sem):
    return pltpu.make_async_copy(src, dst, sem)


def _dispatch_kernel(dest_ref, hn_ref, xs_in_ref, xs_ref, sem):
    del xs_in_ref

    def issue(t, carry):
        src = hn_ref.at[pl.ds(t, 1), :]
        _row_copy(src, xs_ref.at[pl.ds(dest_ref[0, 0, 2 * t], 1), :], sem).start()
        _row_copy(src, xs_ref.at[pl.ds(dest_ref[0, 0, 2 * t + 1], 1), :], sem).start()
        return carry

    lax.fori_loop(0, TMD, issue, 0, unroll=DMA_UNROLL)

    def drain(t, carry):
        _row_copy(hn_ref.at[pl.ds(0, 1), :], xs_ref.at[pl.ds(0, 1), :], sem).wait()
        _row_copy(hn_ref.at[pl.ds(0, 1), :], xs_ref.at[pl.ds(0, 1), :], sem).wait()
        return carry

    lax.fori_loop(0, TMD, drain, 0, unroll=DMA_UNROLL)


def _dispatch(dest, hn, xs_zero):
    t = hn.shape[0]
    return pl.pallas_call(
        _dispatch_kernel,
        grid=(t // TMD,),
        in_specs=[pl.BlockSpec((1, 1, 2 * TMD), lambda i: (i, 0, 0), memory_space=pltpu.SMEM),
                  pl.BlockSpec((TMD, D_MODEL), lambda i: (i, 0)),
                  pl.BlockSpec(memory_space=pl.ANY)],
        out_specs=pl.BlockSpec(memory_space=pl.ANY),
        out_shape=jax.ShapeDtypeStruct(xs_zero.shape, F32),
        scratch_shapes=[pltpu.SemaphoreType.DMA(())],
        input_output_aliases={2: 0},
        compiler_params=_cparams(("arbitrary",)),
        name="dispatch",
    )(dest, hn, xs_zero)


def _combine_kernel(dcur_ref, dnext_ref, h_ref, info_ref, ys_ref, o_ref, buf, sems):
    i = pl.program_id(0)
    slot = i % 2

    def gather(dref, slot_):
        def issue(t, carry):
            for pick in range(2):
                _row_copy(ys_ref.at[pl.ds(dref[0, 0, 2 * t + pick], 1), :],
                          buf.at[slot_, pick, pl.ds(t, 1), :], sems.at[slot_]).start()
            return carry

        lax.fori_loop(0, TMD, issue, 0, unroll=DMA_UNROLL)

    @pl.when(i == 0)
    def _():
        gather(dcur_ref, 0)

    @pl.when(i + 1 < pl.num_programs(0))
    def _():
        gather(dnext_ref, 1 - slot)

    def drain(t, carry):
        for pick in range(2):
            _row_copy(ys_ref.at[pl.ds(0, 1), :], buf.at[slot, pick, pl.ds(0, 1), :], sems.at[slot]).wait()
        return carry

    lax.fori_loop(0, TMD, drain, 0, unroll=DMA_UNROLL)
    g1 = info_ref[:, 4:5]
    g2 = info_ref[:, 5:6]
    o_ref[...] = h_ref[...] + (g1 * buf[slot, 0] + g2 * buf[slot, 1])


def _combine(dest, h, info, ys):
    t = h.shape[0]
    n = t // TMD
    row = lambda w_: pl.BlockSpec((TMD, w_), lambda i: (i, 0))
    return pl.pallas_call(
        _combine_kernel,
        grid=(n,),
        in_specs=[pl.BlockSpec((1, 1, 2 * TMD), lambda i: (i, 0, 0), memory_space=pltpu.SMEM),
                  pl.BlockSpec((1, 1, 2 * TMD), lambda i: (jnp.minimum(i + 1, n - 1), 0, 0),
                               memory_space=pltpu.SMEM),
                  row(D_MODEL), row(LANES), pl.BlockSpec(memory_space=pl.ANY)],
        out_specs=row(D_MODEL),
        out_shape=jax.ShapeDtypeStruct(h.shape, F32),
        scratch_shapes=[pltpu.VMEM((2, 2, TMD, D_MODEL), F32), pltpu.SemaphoreType.DMA((2,))],
        compiler_params=_cparams(("arbitrary",)),
        name="combine",
    )(dest, dest, h, info, ys)


def _moe(h1, g, rw, w_gu, w_down, tri_strict):
    t = h1.shape[0]
    hn, info, cnt = _router(h1, g, rw, tri_strict)
    counts = cnt[0, :N_EXPERTS].astype(jnp.int32)
    padded = ((counts + MOE_BLK - 1) // MOE_BLK) * MOE_BLK
    pend = jnp.cumsum(padded)
    pstart = pend - padded
    e = info[:, 0:2].astype(jnp.int32)
    r = info[:, 2:4].astype(jnp.int32)
    onehot = e[:, :, None] == jnp.arange(N_EXPERTS, dtype=jnp.int32)[None, None, :]
    dest = r + jnp.sum(jnp.where(onehot, pstart[None, None, :], 0), axis=-1)
    dest = dest.reshape(t // TMD, 1, 2 * TMD)
    m_rows = 2 * t + N_EXPERTS * MOE_BLK
    nb = m_rows // MOE_BLK
    blk_start = jnp.arange(nb, dtype=jnp.int32) * MOE_BLK
    block_e = jnp.minimum(jnp.sum(blk_start[:, None] >= pend[None, :], axis=-1), N_EXPERTS - 1)
    n_used = (pend[-1:] // MOE_BLK).astype(jnp.int32)

    xs = _dispatch(dest, hn, jnp.zeros((m_rows, D_MODEL), F32))
    ys = _ffn(xs, g, w_gu, w_down, block_e.astype(jnp.int32), n_used, MOE_BLK, MOE_FB, norm_residual=False)
    return _combine(dest, h1, info, ys)


def _ple_kernel(h_ref, p_ref, g_ref, wg_ref, wu_ref, fg_ref, o_ref, *, final):
    h = h_ref[...]
    hn = _rms(h, g_ref[...]).astype(BF16)
    gate = _sigmoid(jnp.dot(hn, wg_ref[...], preferred_element_type=F32))
    up = jnp.dot(p_ref[...].astype(BF16), wu_ref[...], preferred_element_type=F32)
    out = h + gate * up
    if final:
        out = _rms(out, fg_ref[...])
    o_ref[...] = out


def _ple(h, p, g, wg, wu, fg, final):
    t = h.shape[0]
    row = lambda w_: pl.BlockSpec((TM, w_), lambda i: (i, 0))
    const = lambda a, c: pl.BlockSpec((a, c), lambda i: (0, 0))
    return pl.pallas_call(
        functools.partial(_ple_kernel, final=final),
        grid=(t // TM,),
        in_specs=[row(D_MODEL), row(PLE_DIM), const(1, D_MODEL), const(D_MODEL, D_MODEL),
                  const(PLE_DIM, D_MODEL), const(1, D_MODEL)],
        out_specs=row(D_MODEL),
        out_shape=jax.ShapeDtypeStruct(h.shape, F32),
        compiler_params=_cparams(("parallel",)),
        name="ple",
    )(h, p, g, wg, wu, fg)


def _rope_tables(seq):
    half = ROT_DIMS // 2
    pos = jnp.arange(seq, dtype=F32)
    inv_freq = ROPE_THETA ** (-jnp.arange(0, ROT_DIMS, 2, dtype=F32) / ROT_DIMS)
    ang = pos[:, None] * inv_freq[None, :]
    cos, sin = jnp.cos(ang), jnp.sin(ang)
    lane = np.arange(LANES) % DIFF_DH
    fidx = lane % half
    first = jnp.asarray(lane < half)[None, :]
    second = jnp.asarray((lane >= half) & (lane < ROT_DIMS))[None, :]
    ra = jnp.where(first | second, cos[:, fidx], 1.0)
    rb = jnp.where(second, sin[:, fidx], 0.0)
    rc = jnp.where(first, -sin[:, fidx], 0.0)
    return ra, rb, rc


def kernel(x, p, norm_mix_g, w_in, conv_w, conv_b, conv_ln_g, conv_ln_b, diff_lambda, diff_subln_g,
           gla_w_gate2, gla_b_gate, gla_norm_g, w_out, norm_ffn_g, ffn_w_gu, ffn_w_down, router_w,
           moe_w_gu, moe_w_down, ple_w_up, ple_w_gate, ple_norm_g, final_norm_g):
    bsz, seq, d = x.shape
    depth = w_in.shape[0]
    t = bsz * seq
    assert d == D_MODEL and seq % TM == 0 and seq % TQ == 0 and t % TMD == 0

    ropes = _rope_tables(seq)
    idx = np.arange(GLA_TG)
    tri_chunk = jnp.asarray(((idx[:, None] >= idx[None, :])
                             & (idx[:, None] // CHUNK == idx[None, :] // CHUNK)).astype(np.float32)).astype(BF16)
    vi = np.arange(GLA_V)
    ki = np.arange(GLA_QK)
    gmat = jnp.asarray((vi[:, None] // GLA_DV == vi[None, :] // GLA_DV).astype(np.float32)
                       / GLA_DV).astype(BF16)
    head_mask = jnp.asarray((vi[:, None] // GLA_DV == ki[None, :] // GLA_DK).astype(np.float32))
    ti = np.arange(TM)
    tri_strict = jnp.asarray((ti[:, None] > ti[None, :]).astype(np.float32)).astype(BF16)
    zero_e = jnp.zeros((t // TM,), jnp.int32)
    all_used = jnp.full((1,), t // TM, jnp.int32)

    h = x.reshape(t, D_MODEL)
    for i in range(depth):
        lam_init = 0.8 - 0.6 * math.exp(-0.3 * i)
        w_in_i = jnp.pad(w_in[i], ((0, 0), (0, D_IN_PAD - D_IN))).astype(BF16)
        wg2 = jnp.pad(gla_w_gate2[i], ((0, LANES - GLA_GATE_RANK), (0, 0)))
        u, dq, dk, dv, gq, gk, gv, gr, la = _inproj(
            h, norm_mix_g[i][None, :], w_in_i, ropes, wg2, gla_b_gate[i][None, :], seq)
        y_conv = _conv(u, conv_w[i], conv_b[i][None, :], conv_ln_g[i][None, :],
                       conv_ln_b[i][None, :], bsz, seq)
        y_diff = _attn(dq, dk, dv, diff_lambda[i], diff_subln_g[i][None, :], bsz, seq, lam_init)
        y_gla = _gla(gq, gk, gv, gr, la, tri_chunk, gmat, head_mask,
                     jnp.tile(gla_norm_g[i], GLA_HEADS)[None, :], bsz, seq)
        h = _outproj(h, y_conv, y_diff, y_gla, w_out[i].astype(BF16))

        g_ffn = norm_ffn_g[i][None, :]
        if i % 2 == 0:
            j = i // 2
            h = _ffn(h, g_ffn, ffn_w_gu[j][None].astype(BF16), ffn_w_down[j][None].astype(BF16),
                     zero_e, all_used, TM, FFN_FB, norm_residual=True)
        else:
            j = i // 2
            rw = jnp.pad(router_w[j], ((0, 0), (0, LANES - N_EXPERTS)))
            h = _moe(h, g_ffn, rw, moe_w_gu[j].astype(BF16), moe_w_down[j].astype(BF16), tri_strict)

        h = _ple(h, p[i].reshape(t, PLE_DIM), ple_norm_g[i][None, :], ple_w_gate[i].astype(BF16),
                 ple_w_up[i].astype(BF16), final_norm_g[None, :], final=(i == depth - 1))
    return h.reshape(bsz, seq, D_MODEL)
```

```python
import functools
import math

import jax
import jax.numpy as jnp
import numpy as np
from jax import lax
from jax.experimental import pallas as pl
from jax.experimental.pallas import tpu as pltpu

F32 = jnp.float32
BF16 = jnp.bfloat16

D_MODEL = 1024
CHUNK = 64
CONV_CH = 256
CONV_WIDTH = 31
DIFF_HEADS = 4
DIFF_DV = 128
DIFF_DH = 64
GLA_HEADS = 4
GLA_DV = 64
GLA_DK = 32
GLA_GATE_RANK = 16
GLA_TAU = 16.0
ROPE_THETA = 500000.0
ROT_DIMS = 16
D_FF = 2816
N_EXPERTS = 8
D_FF_EXPERT = 3584
PLE_DIM = 256
EPS = 1e-6
DIFF_W = 512
GLA_QK = GLA_HEADS * GLA_DK
GLA_V = GLA_HEADS * GLA_DV
D_IN = 2832
D_IN_PAD = 2944

LANES = 128
SUBLANES = 8
VMEM_LIMIT = 56 * 1024 * 1024
LOG2E = math.log2(math.e)

TM = 512
TK = 512
TQ = 2 * TK
CONV_TS = 512
CONV_HALO = 32
CONV_SH_ROWS = CONV_TS + CONV_HALO - SUBLANES
GLA_TG = 512
FFN_FB = 1408
MOE_BLK = 1024
MOE_FB = 896
TMD = 256
DMA_UNROLL = 8


def _cparams(sem):
    return pltpu.CompilerParams(dimension_semantics=sem, vmem_limit_bytes=VMEM_LIMIT)


def _rms(x, g):
    ms = jnp.mean(x * x, axis=-1, keepdims=True)
    return x * lax.rsqrt(ms + EPS) * g


def _sigmoid(x):
    return 1.0 / (1.0 + jnp.exp(-x))


def _silu(x):
    return x * _sigmoid(x)


def _split_bf16(x):
    hi = x.astype(BF16)
    return hi, (x - hi.astype(F32)).astype(BF16)


def _dot_f32(a, b):
    a_hi, a_lo = _split_bf16(a)
    b_hi, b_lo = _split_bf16(b)
    dot = functools.partial(jnp.dot, preferred_element_type=F32)
    return dot(a_hi, b_hi) + (dot(a_hi, b_lo) + dot(a_lo, b_hi))


def _rope(x, ra, rb, rc):
    outs = []
    for c in range(x.shape[1] // LANES):
        xc = x[:, c * LANES:(c + 1) * LANES]
        outs.append(xc * ra + pltpu.roll(xc, ROT_DIMS // 2, 1) * rb
                    + pltpu.roll(xc, LANES - ROT_DIMS // 2, 1) * rc)
    return jnp.concatenate(outs, axis=1)


def _inproj_kernel(h_ref, g_ref, w_ref, ra_ref, rb_ref, rc_ref, wg2_ref, bg_ref,
                   u_ref, q_ref, k_ref, v_ref, gq_ref, gk_ref, gv_ref, gr_ref, la_ref):
    hn = _rms(h_ref[...], g_ref[...]).astype(BF16)

    def proj(a, b):
        return jnp.dot(hn, w_ref[0, :, a:b], preferred_element_type=F32)

    ra, rb, rc = ra_ref[...], rb_ref[...], rc_ref[...]
    u_ref[...] = proj(0, 256) * _sigmoid(proj(256, 512))
    q_ref[...] = (_rope(proj(512, 1024), ra, rb, rc) * (LOG2E * DIFF_DH ** -0.5)).astype(BF16)
    k_ref[...] = _rope(proj(1024, 1536), ra, rb, rc).astype(BF16)
    v_ref[...] = proj(1536, 2048).astype(BF16)
    gq_ref[...] = proj(2048, 2176) * (GLA_DK ** -0.5)
    gk_ref[...] = proj(2176, 2304)
    gv_ref[...] = proj(2304, 2560)
    gr_ref[...] = _silu(proj(2560, 2816))
    gz = proj(2816, D_IN_PAD)
    ga = _dot_f32(gz, wg2_ref[...]) + bg_ref[...]
    la_ref[...] = (jnp.minimum(ga, 0.0) - jnp.log(1.0 + jnp.exp(-jnp.abs(ga)))) * (1.0 / GLA_TAU)


def _layer_spec(layer, a, b):
    return pl.BlockSpec((1, a, b), lambda i: (layer, 0, 0))


def _inproj(h, g, w_stack, layer, ropes, wg2, bg, seq):
    t = h.shape[0]
    nseq = seq // TM
    row = lambda w_: pl.BlockSpec((TM, w_), lambda i: (i, 0))
    const = lambda a, b: pl.BlockSpec((a, b), lambda i: (0, 0))
    rope_spec = pl.BlockSpec((TM, LANES), lambda i: (i % nseq, 0))
    outs = [(CONV_CH, F32), (DIFF_W, BF16), (DIFF_W, BF16), (DIFF_W, BF16),
            (GLA_QK, F32), (GLA_QK, F32), (GLA_V, F32), (GLA_V, F32), (GLA_QK, F32)]
    return pl.pallas_call(
        _inproj_kernel,
        grid=(t // TM,),
        in_specs=[row(D_MODEL), const(1, D_MODEL), _layer_spec(layer, D_MODEL, D_IN_PAD),
                  rope_spec, rope_spec, rope_spec, const(LANES, LANES), const(1, LANES)],
        out_specs=[row(w_) for w_, _ in outs],
        out_shape=[jax.ShapeDtypeStruct((t, w_), dt) for w_, dt in outs],
        compiler_params=_cparams(("parallel",)),
        name="inproj",
    )(h, g, w_stack, *ropes, wg2, bg)


def _conv_kernel(u_ref, w_ref, b_ref, lg_ref, lb_ref, o_ref, win, shifted):
    @pl.when(pl.program_id(1) == 0)
    def _():
        win[0:CONV_HALO, :] = jnp.zeros((CONV_HALO, CONV_CH), F32)

    win[CONV_HALO:, :] = u_ref[...]
    for r in range(1, SUBLANES):
        shifted[r - 1] = win[r:r + CONV_SH_ROWS, :]
    rows = 64
    base = CONV_HALO - (CONV_WIDTH - 1)
    for r0 in range(0, CONV_TS, rows):
        acc = jnp.zeros((rows, CONV_CH), F32) + b_ref[...]
        for j in range(CONV_WIDTH):
            res = (base + j) % SUBLANES
            start = r0 + (base + j) - res
            if res == 0:
                tap = win[start:start + rows, :]
            else:
                tap = shifted[res - 1, start:start + rows, :]
            acc = acc + tap * w_ref[j:j + 1, :]
        mu = jnp.mean(acc, axis=-1, keepdims=True)
        xc = acc - mu
        var = jnp.mean(xc * xc, axis=-1, keepdims=True)
        y = xc * lax.rsqrt(var + EPS) * lg_ref[...] + lb_ref[...]
        o_ref[r0:r0 + rows, :] = _silu(y).astype(BF16)
    win[0:CONV_HALO, :] = win[CONV_TS:CONV_TS + CONV_HALO, :]


def _conv(u, w, b, lg, lb, bsz, seq):
    nt = seq // CONV_TS
    const = lambda a, c: pl.BlockSpec((a, c), lambda bi, ti: (0, 0))
    return pl.pallas_call(
        _conv_kernel,
        grid=(bsz, nt),
        in_specs=[pl.BlockSpec((CONV_TS, CONV_CH), lambda bi, ti: (bi * nt + ti, 0)),
                  const(CONV_WIDTH, CONV_CH), const(1, CONV_CH), const(1, CONV_CH), const(1, CONV_CH)],
        out_specs=pl.BlockSpec((CONV_TS, CONV_CH), lambda bi, ti: (bi * nt + ti, 0)),
        out_shape=jax.ShapeDtypeStruct(u.shape, BF16),
        scratch_shapes=[pltpu.VMEM((CONV_TS + CONV_HALO, CONV_CH), F32),
                        pltpu.VMEM((SUBLANES - 1, CONV_SH_ROWS, CONV_CH), F32)],
        compiler_params=_cparams(("arbitrary", "arbitrary")),
        name="conv",
    )(u, w, b, lg, lb)


def _attn_kernel(qt_ref, kt_ref, lam_ref, sg_ref, q_ref, k_ref, v_ref, o_ref, m_scr, acc_scr,
                 *, lam_init):
    step = pl.program_id(1)
    qi = qt_ref[step]
    kj = kt_ref[step]

    @pl.when(kj == 0)
    def _():
        m_scr[...] = jnp.full(m_scr.shape, -jnp.inf, F32)
        acc_scr[...] = jnp.zeros(acc_scr.shape, F32)

    def update(sub, masked):
        rows = slice(sub * TK, (sub + 1) * TK)
        if masked:
            rq = lax.broadcasted_iota(jnp.int32, (TK, TK), 0) // CHUNK
            ck = lax.broadcasted_iota(jnp.int32, (TK, TK), 1) // CHUNK
            allowed = ck <= rq
        for h in range(DIFF_HEADS):
            hs = slice(h * LANES, (h + 1) * LANES)
            q = q_ref[rows, hs]
            k = k_ref[:, hs]
            v = v_ref[:, hs]
            vext = jnp.concatenate([v, jnp.ones_like(v)], axis=1)
            lane = lax.broadcasted_iota(jnp.int32, q.shape, 1)
            for c in range(2):
                sel = (lane < DIFF_DH) if c == 0 else (lane >= DIFF_DH)
                qc = jnp.where(sel, q, jnp.zeros_like(q))
                s = lax.dot_general(qc, k, (((1,), (1,)), ((), ())), preferred_element_type=F32)
                if masked:
                    s = jnp.where(allowed, s, -jnp.inf)
                m_old = m_scr[2 * h + c, rows]
                m_new = jnp.maximum(m_old, jnp.max(s, axis=-1, keepdims=True))
                alpha = jnp.exp2(m_old - m_new)
                p = jnp.exp2(s - jnp.tile(m_new, (1, TK // LANES)))
                pv = jnp.dot(p.astype(BF16), vext, preferred_element_type=F32)
                acc_scr[2 * h + c, rows] = jnp.tile(alpha, (1, 2)) * acc_scr[2 * h + c, rows] + pv
                m_scr[2 * h + c, rows] = m_new

    @pl.when(kj < 2 * qi)
    def _():
        update(0, False)
        update(1, False)

    @pl.when(kj == 2 * qi)
    def _():
        update(0, True)
        update(1, False)

    @pl.when(kj == 2 * qi + 1)
    def _():
        update(1, True)
        lp = lam_ref[...]
        lam = (jnp.exp(jnp.sum(lp[0:1] * lp[1:2], axis=-1, keepdims=True))
               - jnp.exp(jnp.sum(lp[2:3] * lp[3:4], axis=-1, keepdims=True)) + lam_init)
        for h in range(DIFF_HEADS):
            a0 = acc_scr[2 * h]
            a1 = acc_scr[2 * h + 1]
            o = a0[:, :DIFF_DV] / a0[:, DIFF_DV:] - lam * (a1[:, :DIFF_DV] / a1[:, DIFF_DV:])
            o_ref[:, h * LANES:(h + 1) * LANES] = (_rms(o, sg_ref[...]) * (1.0 - lam_init)).astype(BF16)


def _attn(q, k, v, lam_p, sg, bsz, seq, lam_init):
    nq = seq // TQ
    nk = seq // TK
    pairs = [(a, b) for a in range(nq) for b in range(2 * a + 2)]
    qt = jnp.asarray(np.array([a for a, _ in pairs], np.int32))
    kt = jnp.asarray(np.array([b for _, b in pairs], np.int32))
    qspec = pl.BlockSpec((TQ, DIFF_W), lambda b, s, qt_, kt_: (b * nq + qt_[s], 0))
    kspec = pl.BlockSpec((TK, DIFF_W), lambda b, s, qt_, kt_: (b * nk + kt_[s], 0))
    const = lambda a, c: pl.BlockSpec((a, c), lambda b, s, qt_, kt_: (0, 0))
    grid_spec = pltpu.PrefetchScalarGridSpec(
        num_scalar_prefetch=2,
        grid=(bsz, len(pairs)),
        in_specs=[const(4, DIFF_DH), const(1, DIFF_DV), qspec, kspec, kspec],
        out_specs=qspec,
        scratch_shapes=[pltpu.VMEM((2 * DIFF_HEADS, TQ, LANES), F32),
                        pltpu.VMEM((2 * DIFF_HEADS, TQ, 2 * DIFF_DV), F32)],
    )
    return pl.pallas_call(
        functools.partial(_attn_kernel, lam_init=lam_init),
        grid_spec=grid_spec,
        out_shape=jax.ShapeDtypeStruct(q.shape, BF16),
        compiler_params=_cparams(("parallel", "arbitrary")),
        name="diffattn",
    )(qt, kt, lam_p, sg, q, k, v)


def _gla_kernel(q_ref, k_ref, v_ref, r_ref, la_ref, tri_ref, gmat_ref, mask_ref, ng_ref, o_ref,
                st, kv_scr, st_scr, o_scr):
    @pl.when(pl.program_id(1) == 0)
    def _():
        st[...] = jnp.zeros(st.shape, F32)

    la = la_ref[...]
    la_hi = la.astype(BF16)
    la_lo = (la - la_hi.astype(F32)).astype(BF16)
    cum = (jnp.dot(tri_ref[...], la_hi, preferred_element_type=F32)
           + jnp.dot(tri_ref[...], la_lo, preferred_element_type=F32))
    same_head = mask_ref[...] > 0.0
    nch = GLA_TG // CHUNK
    tots = []
    for c in range(nch):
        sl = slice(c * CHUNK, (c + 1) * CHUNK)
        cum_c = cum[sl]
        tot = cum_c[CHUNK - 1:CHUNK]
        tots.append(tot)
        kdec = (k_ref[sl, :] * jnp.exp(tot - cum_c)).astype(BF16)
        v_t = v_ref[sl, :].T.astype(BF16)
        kv_scr[c] = jnp.dot(v_t, kdec, preferred_element_type=F32)
    state = st[...]
    for c in range(nch):
        state = state * jnp.exp(tots[c]) + jnp.where(same_head, kv_scr[c], 0.0)
        st_scr[c] = state.astype(BF16)
    st[...] = state
    for c in range(nch):
        sl = slice(c * CHUNK, (c + 1) * CHUNK)
        o_scr[sl, :] = lax.dot_general(q_ref[sl, :].astype(BF16), st_scr[c], (((1,), (1,)), ((), ())),
                                       preferred_element_type=F32)
    o = o_scr[...]
    ms = jnp.dot((o * o).astype(BF16), gmat_ref[...], preferred_element_type=F32)
    o_ref[...] = (o * lax.rsqrt(ms + EPS) * ng_ref[...] * r_ref[...]).astype(BF16)


def _gla(gq, gk, gv, gr, la, tri, gmat, head_mask, ng, bsz, seq):
    nt = seq // GLA_TG
    row = lambda w_: pl.BlockSpec((GLA_TG, w_), lambda bi, ti: (bi * nt + ti, 0))
    const = lambda a, c: pl.BlockSpec((a, c), lambda bi, ti: (0, 0))
    return pl.pallas_call(
        _gla_kernel,
        grid=(bsz, nt),
        in_specs=[row(GLA_QK), row(GLA_QK), row(GLA_V), row(GLA_V), row(GLA_QK),
                  const(GLA_TG, GLA_TG), const(GLA_V, GLA_V), const(GLA_V, GLA_QK), const(1, GLA_V)],
        out_specs=row(GLA_V),
        out_shape=jax.ShapeDtypeStruct(gv.shape, BF16),
        scratch_shapes=[pltpu.VMEM((GLA_V, GLA_QK), F32),
                        pltpu.VMEM((GLA_TG // CHUNK, GLA_V, GLA_QK), F32),
                        pltpu.VMEM((GLA_TG // CHUNK, GLA_V, GLA_QK), BF16),
                        pltpu.VMEM((GLA_TG, GLA_V), F32)],
        compiler_params=_cparams(("arbitrary", "arbitrary")),
        name="gla",
    )(gq, gk, gv, gr, la, tri, gmat, head_mask, ng)


def _outproj_kernel(h_ref, yc_ref, yd_ref, yg_ref, w_ref, o_ref):
    acc = jnp.dot(yc_ref[...], w_ref[0, 0:256, :], preferred_element_type=F32)
    acc = acc + jnp.dot(yd_ref[...], w_ref[0, 256:768, :], preferred_element_type=F32)
    acc = acc + jnp.dot(yg_ref[...], w_ref[0, 768:1024, :], preferred_element_type=F32)
    o_ref[...] = h_ref[...] + acc


def _outproj(h, yc, yd, yg, w_stack, layer):
    t = h.shape[0]
    row = lambda w_: pl.BlockSpec((TM, w_), lambda i: (i, 0))
    return pl.pallas_call(
        _outproj_kernel,
        grid=(t // TM,),
        in_specs=[row(D_MODEL), row(CONV_CH), row(DIFF_W), row(GLA_V),
                  _layer_spec(layer, D_MODEL, D_MODEL)],
        out_specs=row(D_MODEL),
        out_shape=jax.ShapeDtypeStruct(h.shape, F32),
        compiler_params=_cparams(("parallel",)),
        name="outproj",
    )(h, yc, yd, yg, w_stack)


def _ffn_kernel(be_ref, nu_ref, x_ref, g_ref, wg_ref, wu_ref, wd_ref, o_ref, xb, acc,
                *, norm_residual):
    i = pl.program_id(0)
    j = pl.program_id(1)
    nj = pl.num_programs(1)

    @pl.when(i < nu_ref[0])
    def _():
        @pl.when(j == 0)
        def _():
            x = x_ref[...]
            if norm_residual:
                x = _rms(x, g_ref[...])
            xb[...] = x.astype(BF16)
            acc[...] = jnp.zeros(acc.shape, F32)

        x = xb[...]
        g = jnp.dot(x, wg_ref[0, 0], preferred_element_type=F32)
        u = jnp.dot(x, wu_ref[0, 0], preferred_element_type=F32)
        a = (_silu(g) * u).astype(BF16)
        acc[...] += jnp.dot(a, wd_ref[0, 0], preferred_element_type=F32)

    @pl.when(j == nj - 1)
    def _():
        if norm_residual:
            o_ref[...] = x_ref[...] + acc[...]
        else:
            o_ref[...] = jnp.where(i < nu_ref[0], acc[...], 0.0)


def _ffn(x, g, w_gu, w_down, layer, block_e, n_used, blk, fb, norm_residual):
    rows = x.shape[0]
    f = w_down.shape[2]
    nj = f // fb

    def last(i, nu):
        return jnp.minimum(i, nu[0] - 1)

    def hid(i, j, nu):
        return jnp.where(i < nu[0], j, nj - 1)

    xspec = pl.BlockSpec((blk, D_MODEL), lambda i, j, be, nu: (last(i, nu), 0))
    wg = pl.BlockSpec((1, 1, D_MODEL, fb),
                      lambda i, j, be, nu: (layer, be[last(i, nu)], 0, hid(i, j, nu)))
    wu = pl.BlockSpec((1, 1, D_MODEL, fb),
                      lambda i, j, be, nu: (layer, be[last(i, nu)], 0, hid(i, j, nu) + nj))
    wd = pl.BlockSpec((1, 1, fb, D_MODEL),
                      lambda i, j, be, nu: (layer, be[last(i, nu)], hid(i, j, nu), 0))
    grid_spec = pltpu.PrefetchScalarGridSpec(
        num_scalar_prefetch=2,
        grid=(rows // blk, nj),
        in_specs=[xspec, pl.BlockSpec((1, D_MODEL), lambda i, j, be, nu: (0, 0)), wg, wu, wd],
        out_specs=pl.BlockSpec((blk, D_MODEL), lambda i, j, be, nu: (i, 0)),
        scratch_shapes=[pltpu.VMEM((blk, D_MODEL), BF16), pltpu.VMEM((blk, D_MODEL), F32)],
    )
    return pl.pallas_call(
        functools.partial(_ffn_kernel, norm_residual=norm_residual),
        grid_spec=grid_spec,
        out_shape=jax.ShapeDtypeStruct((rows, D_MODEL), F32),
        compiler_params=_cparams(("arbitrary", "arbitrary")),
        name="ffn" if norm_residual else "experts",
    )(block_e, n_used, x, g, w_gu, w_gu, w_down)


def _router_kernel(h_ref, g_ref, rw_ref, tri_ref, hn_ref, info_ref, cnt_ref, carry):
    @pl.when(pl.program_id(0) == 0)
    def _():
        carry[...] = jnp.zeros(carry.shape, F32)

    hn = _rms(h_ref[...], g_ref[...])
    hn_ref[...] = hn
    logits = _dot_f32(hn, rw_ref[...])
    lane = lax.broadcasted_iota(jnp.int32, logits.shape, 1)
    lg = jnp.where(lane < N_EXPERTS, logits, -jnp.inf)
    v1 = jnp.max(lg, axis=-1, keepdims=True)
    i1 = jnp.min(jnp.where(lg == v1, lane, LANES), axis=-1, keepdims=True)
    lg2 = jnp.where(lane == i1, -jnp.inf, lg)
    v2 = jnp.max(lg2, axis=-1, keepdims=True)
    i2 = jnp.min(jnp.where(lg2 == v2, lane, LANES), axis=-1, keepdims=True)
    e2 = jnp.exp(v2 - v1)
    g1 = 1.0 / (1.0 + e2)
    g2 = e2 / (1.0 + e2)
    pick1 = lane == i1
    pick2 = lane == i2
    onehot = jnp.where(pick1 | pick2, 1.0, 0.0)
    before = jnp.dot(tri_ref[...], onehot.astype(BF16), preferred_element_type=F32) + carry[...]
    r1 = jnp.sum(jnp.where(pick1, before, 0.0), axis=-1, keepdims=True)
    r2 = jnp.sum(jnp.where(pick2, before, 0.0), axis=-1, keepdims=True)
    carry[...] += jnp.sum(onehot, axis=0, keepdims=True)
    info = jnp.where(lane == 0, i1.astype(F32), 0.0)
    info = jnp.where(lane == 1, i2.astype(F32), info)
    info = jnp.where(lane == 2, r1, info)
    info = jnp.where(lane == 3, r2, info)
    info = jnp.where(lane == 4, g1, info)
    info = jnp.where(lane == 5, g2, info)
    info_ref[...] = info
    cnt_ref[...] = jnp.broadcast_to(carry[...], cnt_ref.shape)


def _router(h, g, rw, tri):
    t = h.shape[0]
    row = lambda w_: pl.BlockSpec((TM, w_), lambda i: (i, 0))
    const = lambda a, c: pl.BlockSpec((a, c), lambda i: (0, 0))
    return pl.pallas_call(
        _router_kernel,
        grid=(t // TM,),
        in_specs=[row(D_MODEL), const(1, D_MODEL), const(D_MODEL, LANES), const(TM, TM)],
        out_specs=[row(D_MODEL), row(LANES), const(SUBLANES, LANES)],
        out_shape=[jax.ShapeDtypeStruct((t, D_MODEL), F32), jax.ShapeDtypeStruct((t, LANES), F32),
                   jax.ShapeDtypeStruct((SUBLANES, LANES), F32)],
        scratch_shapes=[pltpu.VMEM((1, LANES), F32)],
        compiler_params=_cparams(("arbitrary",)),
        name="router",
    )(h, g, rw, tri)


def _row_copy(src, dst, sem):
    return pltpu.make_async_copy(src, dst, sem)


def _dispatch_kernel(dest_ref, hn_ref, xs_in_ref, xs_ref, sem):
    del xs_in_ref

    def issue(t, carry):
        src = hn_ref.at[pl.ds(t, 1), :]
        _row_copy(src, xs_ref.at[pl.ds(dest_ref[0, 0, 2 * t], 1), :], sem).start()
        _row_copy(src, xs_ref.at[pl.ds(dest_ref[0, 0, 2 * t + 1], 1), :], sem).start()
        return carry

    lax.fori_loop(0, TMD, issue, 0, unroll=DMA_UNROLL)

    def drain(t, carry):
        _row_copy(hn_ref.at[pl.ds(0, 1), :], xs_ref.at[pl.ds(0, 1), :], sem).wait()
        _row_copy(hn_ref.at[pl.ds(0, 1), :], xs_ref.at[pl.ds(0, 1), :], sem).wait()
        return carry

    lax.fori_loop(0, TMD, drain, 0, unroll=DMA_UNROLL)


def _dispatch(dest, hn, xs_zero):
    t = hn.shape[0]
    return pl.pallas_call(
        _dispatch_kernel,
        grid=(t // TMD,),
        in_specs=[pl.BlockSpec((1, 1, 2 * TMD), lambda i: (i, 0, 0), memory_space=pltpu.SMEM),
                  pl.BlockSpec((TMD, D_MODEL), lambda i: (i, 0)),
                  pl.BlockSpec(memory_space=pl.ANY)],
        out_specs=pl.BlockSpec(memory_space=pl.ANY),
        out_shape=jax.ShapeDtypeStruct(xs_zero.shape, F32),
        scratch_shapes=[pltpu.SemaphoreType.DMA(())],
        input_output_aliases={2: 0},
        compiler_params=_cparams(("arbitrary",)),
        name="dispatch",
    )(dest, hn, xs_zero)


def _combine_kernel(dcur_ref, dnext_ref, h_ref, info_ref, ys_ref, o_ref, buf, sems):
    i = pl.program_id(0)
    slot = i % 2

    def gather(dref, slot_):
        def issue(t, carry):
            for pick in range(2):
                _row_copy(ys_ref.at[pl.ds(dref[0, 0, 2 * t + pick], 1), :],
                          buf.at[slot_, pick, pl.ds(t, 1), :], sems.at[slot_]).start()
            return carry

        lax.fori_loop(0, TMD, issue, 0, unroll=DMA_UNROLL)

    @pl.when(i == 0)
    def _():
        gather(dcur_ref, 0)

    @pl.when(i + 1 < pl.num_programs(0))
    def _():
        gather(dnext_ref, 1 - slot)

    def drain(t, carry):
        for pick in range(2):
            _row_copy(ys_ref.at[pl.ds(0, 1), :], buf.at[slot, pick, pl.ds(0, 1), :], sems.at[slot]).wait()
        return carry

    lax.fori_loop(0, TMD, drain, 0, unroll=DMA_UNROLL)
    g1 = info_ref[:, 4:5]
    g2 = info_ref[:, 5:6]
    o_ref[...] = h_ref[...] + (g1 * buf[slot, 0] + g2 * buf[slot, 1])


def _combine(dest, h, info, ys):
    t = h.shape[0]
    n = t // TMD
    row = lambda w_: pl.BlockSpec((TMD, w_), lambda i: (i, 0))
    return pl.pallas_call(
        _combine_kernel,
        grid=(n,),
        in_specs=[pl.BlockSpec((1, 1, 2 * TMD), lambda i: (i, 0, 0), memory_space=pltpu.SMEM),
                  pl.BlockSpec((1, 1, 2 * TMD), lambda i: (jnp.minimum(i + 1, n - 1), 0, 0),
                               memory_space=pltpu.SMEM),
                  row(D_MODEL), row(LANES), pl.BlockSpec(memory_space=pl.ANY)],
        out_specs=row(D_MODEL),
        out_shape=jax.ShapeDtypeStruct(h.shape, F32),
        scratch_shapes=[pltpu.VMEM((2, 2, TMD, D_MODEL), F32), pltpu.SemaphoreType.DMA((2,))],
        compiler_params=_cparams(("arbitrary",)),
        name="combine",
    )(dest, dest, h, info, ys)


def _moe(h1, g, rw, w_gu, w_down, layer, tri_strict):
    t = h1.shape[0]
    hn, info, cnt = _router(h1, g, rw, tri_strict)
    counts = cnt[0, :N_EXPERTS].astype(jnp.int32)
    padded = ((counts + MOE_BLK - 1) // MOE_BLK) * MOE_BLK
    pend = jnp.cumsum(padded)
    pstart = pend - padded
    e = info[:, 0:2].astype(jnp.int32)
    r = info[:, 2:4].astype(jnp.int32)
    onehot = e[:, :, None] == jnp.arange(N_EXPERTS, dtype=jnp.int32)[None, None, :]
    dest = r + jnp.sum(jnp.where(onehot, pstart[None, None, :], 0), axis=-1)
    dest = dest.reshape(t // TMD, 1, 2 * TMD)
    m_rows = 2 * t + N_EXPERTS * MOE_BLK
    nb = m_rows // MOE_BLK
    blk_start = jnp.arange(nb, dtype=jnp.int32) * MOE_BLK
    block_e = jnp.minimum(jnp.sum(blk_start[:, None] >= pend[None, :], axis=-1), N_EXPERTS - 1)
    n_used = (pend[-1:] // MOE_BLK).astype(jnp.int32)

    xs = _dispatch(dest, hn, jnp.zeros((m_rows, D_MODEL), F32))
    ys = _ffn(xs, g, w_gu, w_down, layer, block_e.astype(jnp.int32), n_used, MOE_BLK, MOE_FB,
              norm_residual=False)
    return _combine(dest, h1, info, ys)


def _ple_kernel(h_ref, p_ref, g_ref, wg_ref, wu_ref, fg_ref, o_ref, *, final):
    h = h_ref[...]
    hn = _rms(h, g_ref[...]).astype(BF16)
    gate = _sigmoid(jnp.dot(hn, wg_ref[0], preferred_element_type=F32))
    up = jnp.dot(p_ref[0].astype(BF16), wu_ref[0], preferred_element_type=F32)
    out = h + gate * up
    if final:
        out = _rms(out, fg_ref[...])
    o_ref[...] = out


def _ple(h, p_stack, g, wg_stack, wu_stack, layer, fg, final):
    t = h.shape[0]
    row = lambda w_: pl.BlockSpec((TM, w_), lambda i: (i, 0))
    const = lambda a, c: pl.BlockSpec((a, c), lambda i: (0, 0))
    return pl.pallas_call(
        functools.partial(_ple_kernel, final=final),
        grid=(t // TM,),
        in_specs=[row(D_MODEL), pl.BlockSpec((1, TM, PLE_DIM), lambda i: (layer, i, 0)),
                  const(1, D_MODEL), _layer_spec(layer, D_MODEL, D_MODEL),
                  _layer_spec(layer, PLE_DIM, D_MODEL), const(1, D_MODEL)],
        out_specs=row(D_MODEL),
        out_shape=jax.ShapeDtypeStruct(h.shape, F32),
        compiler_params=_cparams(("parallel",)),
        name="ple",
    )(h, p_stack, g, wg_stack, wu_stack, fg)


def _rope_tables(seq):
    half = ROT_DIMS // 2
    pos = jnp.arange(seq, dtype=F32)
    inv_freq = ROPE_THETA ** (-jnp.arange(0, ROT_DIMS, 2, dtype=F32) / ROT_DIMS)
    ang = pos[:, None] * inv_freq[None, :]
    cos, sin = jnp.cos(ang), jnp.sin(ang)
    lane = np.arange(LANES) % DIFF_DH
    fidx = lane % half
    first = jnp.asarray(lane < half)[None, :]
    second = jnp.asarray((lane >= half) & (lane < ROT_DIMS))[None, :]
    ra = jnp.where(first | second, cos[:, fidx], 1.0)
    rb = jnp.where(second, sin[:, fidx], 0.0)
    rc = jnp.where(first, -sin[:, fidx], 0.0)
    return ra, rb, rc


def kernel(x, p, norm_mix_g, w_in, conv_w, conv_b, conv_ln_g, conv_ln_b, diff_lambda, diff_subln_g,
           gla_w_gate2, gla_b_gate, gla_norm_g, w_out, norm_ffn_g, ffn_w_gu, ffn_w_down, router_w,
           moe_w_gu, moe_w_down, ple_w_up, ple_w_gate, ple_norm_g, final_norm_g):
    bsz, seq, d = x.shape
    depth = w_in.shape[0]
    t = bsz * seq
    assert d == D_MODEL and seq % TM == 0 and seq % TQ == 0 and t % TMD == 0

    ropes = _rope_tables(seq)
    idx = np.arange(GLA_TG)
    tri_chunk = jnp.asarray(((idx[:, None] >= idx[None, :])
                             & (idx[:, None] // CHUNK == idx[None, :] // CHUNK)).astype(np.float32)).astype(BF16)
    vi = np.arange(GLA_V)
    ki = np.arange(GLA_QK)
    gmat = jnp.asarray((vi[:, None] // GLA_DV == vi[None, :] // GLA_DV).astype(np.float32)
                       / GLA_DV).astype(BF16)
    head_mask = jnp.asarray((vi[:, None] // GLA_DV == ki[None, :] // GLA_DK).astype(np.float32))
    ti = np.arange(TM)
    tri_strict = jnp.asarray((ti[:, None] > ti[None, :]).astype(np.float32)).astype(BF16)
    zero_e = jnp.zeros((t // TM,), jnp.int32)
    all_used = jnp.full((1,), t // TM, jnp.int32)

    w_in_b = jnp.pad(w_in, ((0, 0), (0, 0), (0, D_IN_PAD - D_IN))).astype(BF16)
    w_out_b = w_out.astype(BF16)
    ffn_gu_b = ffn_w_gu.astype(BF16)[:, None]
    ffn_down_b = ffn_w_down.astype(BF16)[:, None]
    moe_gu_b = moe_w_gu.astype(BF16)
    moe_down_b = moe_w_down.astype(BF16)
    ple_gate_b = ple_w_gate.astype(BF16)
    ple_up_b = ple_w_up.astype(BF16)
    p_rows = p.reshape(depth, t, PLE_DIM)

    h = x.reshape(t, D_MODEL)
    for i in range(depth):
        lam_init = 0.8 - 0.6 * math.exp(-0.3 * i)
        wg2 = jnp.pad(gla_w_gate2[i], ((0, LANES - GLA_GATE_RANK), (0, 0)))
        u, dq, dk, dv, gq, gk, gv, gr, la = _inproj(
            h, norm_mix_g[i][None, :], w_in_b, i, ropes, wg2, gla_b_gate[i][None, :], seq)
        y_conv = _conv(u, conv_w[i], conv_b[i][None, :], conv_ln_g[i][None, :],
                       conv_ln_b[i][None, :], bsz, seq)
        y_diff = _attn(dq, dk, dv, diff_lambda[i], diff_subln_g[i][None, :], bsz, seq, lam_init)
        y_gla = _gla(gq, gk, gv, gr, la, tri_chunk, gmat, head_mask,
                     jnp.tile(gla_norm_g[i], GLA_HEADS)[None, :], bsz, seq)
        h = _outproj(h, y_conv, y_diff, y_gla, w_out_b, i)

        g_ffn = norm_ffn_g[i][None, :]
        j = i // 2
        if i % 2 == 0:
            h = _ffn(h, g_ffn, ffn_gu_b, ffn_down_b, j, zero_e, all_used, TM, FFN_FB, norm_residual=True)
        else:
            rw = jnp.pad(router_w[j], ((0, 0), (0, LANES - N_EXPERTS)))
            h = _moe(h, g_ffn, rw, moe_gu_b, moe_down_b, j, tri_strict)

        h = _ple(h, p_rows, ple_norm_g[i][None, :], ple_gate_b, ple_up_b, i, final_norm_g[None, :],
                 final=(i == depth - 1))
    return h.reshape(bsz, seq, D_MODEL)
```

```python
import functools
import math

import jax
import jax.numpy as jnp
import numpy as np
from jax import lax
from jax.experimental import pallas as pl
from jax.experimental.pallas import tpu as pltpu

F32 = jnp.float32
BF16 = jnp.bfloat16

D_MODEL = 1024
CHUNK = 64
CONV_CH = 256
CONV_WIDTH = 31
DIFF_HEADS = 4
DIFF_DV = 128
DIFF_DH = 64
GLA_HEADS = 4
GLA_DV = 64
GLA_DK = 32
GLA_GATE_RANK = 16
GLA_TAU = 16.0
ROPE_THETA = 500000.0
ROT_DIMS = 16
D_FF = 2816
N_EXPERTS = 8
D_FF_EXPERT = 3584
PLE_DIM = 256
EPS = 1e-6
DIFF_W = 512
GLA_QK = GLA_HEADS * GLA_DK
GLA_V = GLA_HEADS * GLA_DV
D_IN = 2832
D_IN_PAD = 2944

LANES = 128
SUBLANES = 8
VMEM_LIMIT = 56 * 1024 * 1024
LOG2E = math.log2(math.e)

TM = 512
TK = 512
TQ = 2 * TK
CONV_TS = 512
CONV_HALO = 32
CONV_SH_ROWS = CONV_TS + CONV_HALO - SUBLANES
GLA_TG = 512
FFN_FB = 1408
MOE_BLK = 1024
MOE_FB = 896
ROW_TILE = D_MODEL // LANES
TMD = 256
DMA_UNROLL = 8


def _cparams(sem):
    return pltpu.CompilerParams(dimension_semantics=sem, vmem_limit_bytes=VMEM_LIMIT)


def _rms(x, g):
    ms = jnp.mean(x * x, axis=-1, keepdims=True)
    return x * lax.rsqrt(ms + EPS) * g


def _sigmoid(x):
    return 1.0 / (1.0 + jnp.exp(-x))


def _silu(x):
    return x * _sigmoid(x)


def _split_bf16(x):
    hi = x.astype(BF16)
    return hi, (x - hi.astype(F32)).astype(BF16)


def _dot_f32(a, b):
    a_hi, a_lo = _split_bf16(a)
    b_hi, b_lo = _split_bf16(b)
    dot = functools.partial(jnp.dot, preferred_element_type=F32)
    return dot(a_hi, b_hi) + (dot(a_hi, b_lo) + dot(a_lo, b_hi))


def _rope(x, ra, rb, rc):
    outs = []
    for c in range(x.shape[1] // LANES):
        xc = x[:, c * LANES:(c + 1) * LANES]
        outs.append(xc * ra + pltpu.roll(xc, ROT_DIMS // 2, 1) * rb
                    + pltpu.roll(xc, LANES - ROT_DIMS // 2, 1) * rc)
    return jnp.concatenate(outs, axis=1)


def _inproj_kernel(h_ref, g_ref, w_ref, ra_ref, rb_ref, rc_ref, wg2_ref, bg_ref,
                   u_ref, q_ref, k_ref, v_ref, gq_ref, gk_ref, gv_ref, gr_ref, la_ref):
    hn = _rms(h_ref[...], g_ref[...]).astype(BF16)

    def proj(a, b):
        return jnp.dot(hn, w_ref[0, :, a:b], preferred_element_type=F32)

    ra, rb, rc = ra_ref[...], rb_ref[...], rc_ref[...]
    u_ref[...] = proj(0, 256) * _sigmoid(proj(256, 512))
    q_ref[...] = (_rope(proj(512, 1024), ra, rb, rc) * (LOG2E * DIFF_DH ** -0.5)).astype(BF16)
    k_ref[...] = _rope(proj(1024, 1536), ra, rb, rc).astype(BF16)
    v_ref[...] = proj(1536, 2048).astype(BF16)
    gq_ref[...] = proj(2048, 2176) * (GLA_DK ** -0.5)
    gk_ref[...] = proj(2176, 2304)
    gv_ref[...] = proj(2304, 2560)
    gr_ref[...] = _silu(proj(2560, 2816))
    gz = proj(2816, D_IN_PAD)
    ga = _dot_f32(gz, wg2_ref[...]) + bg_ref[...]
    la_ref[...] = (jnp.minimum(ga, 0.0) - jnp.log(1.0 + jnp.exp(-jnp.abs(ga)))) * (1.0 / GLA_TAU)


def _layer_spec(layer, a, b):
    return pl.BlockSpec((1, a, b), lambda i: (layer, 0, 0))


def _inproj(h, g, w_stack, layer, ropes, wg2, bg, seq):
    t = h.shape[0]
    nseq = seq // TM
    row = lambda w_: pl.BlockSpec((TM, w_), lambda i: (i, 0))
    const = lambda a, b: pl.BlockSpec((a, b), lambda i: (0, 0))
    rope_spec = pl.BlockSpec((TM, LANES), lambda i: (i % nseq, 0))
    outs = [(CONV_CH, F32), (DIFF_W, BF16), (DIFF_W, BF16), (DIFF_W, BF16),
            (GLA_QK, F32), (GLA_QK, F32), (GLA_V, F32), (GLA_V, F32), (GLA_QK, F32)]
    return pl.pallas_call(
        _inproj_kernel,
        grid=(t // TM,),
        in_specs=[row(D_MODEL), const(1, D_MODEL), _layer_spec(layer, D_MODEL, D_IN_PAD),
                  rope_spec, rope_spec, rope_spec, const(LANES, LANES), const(1, LANES)],
        out_specs=[row(w_) for w_, _ in outs],
        out_shape=[jax.ShapeDtypeStruct((t, w_), dt) for w_, dt in outs],
        compiler_params=_cparams(("parallel",)),
        name="inproj",
    )(h, g, w_stack, *ropes, wg2, bg)


def _conv_kernel(u_ref, w_ref, b_ref, lg_ref, lb_ref, o_ref, win, shifted):
    @pl.when(pl.program_id(1) == 0)
    def _():
        win[0:CONV_HALO, :] = jnp.zeros((CONV_HALO, CONV_CH), F32)

    win[CONV_HALO:, :] = u_ref[...]
    for r in range(1, SUBLANES):
        shifted[r - 1] = win[r:r + CONV_SH_ROWS, :]
    rows = 64
    base = CONV_HALO - (CONV_WIDTH - 1)
    for r0 in range(0, CONV_TS, rows):
        acc = jnp.zeros((rows, CONV_CH), F32) + b_ref[...]
        for j in range(CONV_WIDTH):
            res = (base + j) % SUBLANES
            start = r0 + (base + j) - res
            if res == 0:
                tap = win[start:start + rows, :]
            else:
                tap = shifted[res - 1, start:start + rows, :]
            acc = acc + tap * w_ref[j:j + 1, :]
        mu = jnp.mean(acc, axis=-1, keepdims=True)
        xc = acc - mu
        var = jnp.mean(xc * xc, axis=-1, keepdims=True)
        y = xc * lax.rsqrt(var + EPS) * lg_ref[...] + lb_ref[...]
        o_ref[r0:r0 + rows, :] = _silu(y).astype(BF16)
    win[0:CONV_HALO, :] = win[CONV_TS:CONV_TS + CONV_HALO, :]


def _conv(u, w, b, lg, lb, bsz, seq):
    nt = seq // CONV_TS
    const = lambda a, c: pl.BlockSpec((a, c), lambda bi, ti: (0, 0))
    return pl.pallas_call(
        _conv_kernel,
        grid=(bsz, nt),
        in_specs=[pl.BlockSpec((CONV_TS, CONV_CH), lambda bi, ti: (bi * nt + ti, 0)),
                  const(CONV_WIDTH, CONV_CH), const(1, CONV_CH), const(1, CONV_CH), const(1, CONV_CH)],
        out_specs=pl.BlockSpec((CONV_TS, CONV_CH), lambda bi, ti: (bi * nt + ti, 0)),
        out_shape=jax.ShapeDtypeStruct(u.shape, BF16),
        scratch_shapes=[pltpu.VMEM((CONV_TS + CONV_HALO, CONV_CH), F32),
                        pltpu.VMEM((SUBLANES - 1, CONV_SH_ROWS, CONV_CH), F32)],
        compiler_params=_cparams(("arbitrary", "arbitrary")),
        name="conv",
    )(u, w, b, lg, lb)


def _attn_kernel(qt_ref, kt_ref, lam_ref, sg_ref, q_ref, k_ref, v_ref, o_ref, m_scr, acc_scr,
                 *, lam_init):
    step = pl.program_id(1)
    qi = qt_ref[step]
    kj = kt_ref[step]

    @pl.when(kj == 0)
    def _():
        m_scr[...] = jnp.full(m_scr.shape, -jnp.inf, F32)
        acc_scr[...] = jnp.zeros(acc_scr.shape, F32)

    def update(sub, masked):
        rows = slice(sub * TK, (sub + 1) * TK)
        if masked:
            rq = lax.broadcasted_iota(jnp.int32, (TK, TK), 0) // CHUNK
            ck = lax.broadcasted_iota(jnp.int32, (TK, TK), 1) // CHUNK
            allowed = ck <= rq
        for h in range(DIFF_HEADS):
            hs = slice(h * LANES, (h + 1) * LANES)
            q = q_ref[rows, hs]
            k = k_ref[:, hs]
            v = v_ref[:, hs]
            vext = jnp.concatenate([v, jnp.ones_like(v)], axis=1)
            lane = lax.broadcasted_iota(jnp.int32, q.shape, 1)
            for c in range(2):
                sel = (lane < DIFF_DH) if c == 0 else (lane >= DIFF_DH)
                qc = jnp.where(sel, q, jnp.zeros_like(q))
                s = lax.dot_general(qc, k, (((1,), (1,)), ((), ())), preferred_element_type=F32)
                if masked:
                    s = jnp.where(allowed, s, -jnp.inf)
                m_old = m_scr[2 * h + c, rows]
                m_new = jnp.maximum(m_old, jnp.max(s, axis=-1, keepdims=True))
                alpha = jnp.exp2(m_old - m_new)
                p = jnp.exp2(s - jnp.tile(m_new, (1, TK // LANES)))
                pv = jnp.dot(p.astype(BF16), vext, preferred_element_type=F32)
                acc_scr[2 * h + c, rows] = jnp.tile(alpha, (1, 2)) * acc_scr[2 * h + c, rows] + pv
                m_scr[2 * h + c, rows] = m_new

    @pl.when(kj < 2 * qi)
    def _():
        update(0, False)
        update(1, False)

    @pl.when(kj == 2 * qi)
    def _():
        update(0, True)
        update(1, False)

    @pl.when(kj == 2 * qi + 1)
    def _():
        update(1, True)
        lp = lam_ref[...]
        lam = (jnp.exp(jnp.sum(lp[0:1] * lp[1:2], axis=-1, keepdims=True))
               - jnp.exp(jnp.sum(lp[2:3] * lp[3:4], axis=-1, keepdims=True)) + lam_init)
        for h in range(DIFF_HEADS):
            a0 = acc_scr[2 * h]
            a1 = acc_scr[2 * h + 1]
            o = a0[:, :DIFF_DV] / a0[:, DIFF_DV:] - lam * (a1[:, :DIFF_DV] / a1[:, DIFF_DV:])
            o_ref[:, h * LANES:(h + 1) * LANES] = (_rms(o, sg_ref[...]) * (1.0 - lam_init)).astype(BF16)


def _attn(q, k, v, lam_p, sg, bsz, seq, lam_init):
    nq = seq // TQ
    nk = seq // TK
    pairs = [(a, b) for a in range(nq) for b in range(2 * a + 2)]
    qt = jnp.asarray(np.array([a for a, _ in pairs], np.int32))
    kt = jnp.asarray(np.array([b for _, b in pairs], np.int32))
    qspec = pl.BlockSpec((TQ, DIFF_W), lambda b, s, qt_, kt_: (b * nq + qt_[s], 0))
    kspec = pl.BlockSpec((TK, DIFF_W), lambda b, s, qt_, kt_: (b * nk + kt_[s], 0))
    const = lambda a, c: pl.BlockSpec((a, c), lambda b, s, qt_, kt_: (0, 0))
    grid_spec = pltpu.PrefetchScalarGridSpec(
        num_scalar_prefetch=2,
        grid=(bsz, len(pairs)),
        in_specs=[const(4, DIFF_DH), const(1, DIFF_DV), qspec, kspec, kspec],
        out_specs=qspec,
        scratch_shapes=[pltpu.VMEM((2 * DIFF_HEADS, TQ, LANES), F32),
                        pltpu.VMEM((2 * DIFF_HEADS, TQ, 2 * DIFF_DV), F32)],
    )
    return pl.pallas_call(
        functools.partial(_attn_kernel, lam_init=lam_init),
        grid_spec=grid_spec,
        out_shape=jax.ShapeDtypeStruct(q.shape, BF16),
        compiler_params=_cparams(("parallel", "arbitrary")),
        name="diffattn",
    )(qt, kt, lam_p, sg, q, k, v)


def _gla_kernel(q_ref, k_ref, v_ref, r_ref, la_ref, tri_ref, gmat_ref, mask_ref, ng_ref, o_ref,
                st, kv_scr, st_scr, o_scr):
    @pl.when(pl.program_id(1) == 0)
    def _():
        st[...] = jnp.zeros(st.shape, F32)

    la = la_ref[...]
    la_hi = la.astype(BF16)
    la_lo = (la - la_hi.astype(F32)).astype(BF16)
    cum = (jnp.dot(tri_ref[...], la_hi, preferred_element_type=F32)
           + jnp.dot(tri_ref[...], la_lo, preferred_element_type=F32))
    same_head = mask_ref[...] > 0.0
    nch = GLA_TG // CHUNK
    tots = []
    for c in range(nch):
        sl = slice(c * CHUNK, (c + 1) * CHUNK)
        cum_c = cum[sl]
        tot = cum_c[CHUNK - 1:CHUNK]
        tots.append(tot)
        kdec = (k_ref[sl, :] * jnp.exp(tot - cum_c)).astype(BF16)
        v_t = v_ref[sl, :].T.astype(BF16)
        kv_scr[c] = jnp.dot(v_t, kdec, preferred_element_type=F32)
    state = st[...]
    for c in range(nch):
        state = state * jnp.exp(tots[c]) + jnp.where(same_head, kv_scr[c], 0.0)
        st_scr[c] = state.astype(BF16)
    st[...] = state
    for c in range(nch):
        sl = slice(c * CHUNK, (c + 1) * CHUNK)
        o_scr[sl, :] = lax.dot_general(q_ref[sl, :].astype(BF16), st_scr[c], (((1,), (1,)), ((), ())),
                                       preferred_element_type=F32)
    o = o_scr[...]
    ms = jnp.dot((o * o).astype(BF16), gmat_ref[...], preferred_element_type=F32)
    o_ref[...] = (o * lax.rsqrt(ms + EPS) * ng_ref[...] * r_ref[...]).astype(BF16)


def _gla(gq, gk, gv, gr, la, tri, gmat, head_mask, ng, bsz, seq):
    nt = seq // GLA_TG
    row = lambda w_: pl.BlockSpec((GLA_TG, w_), lambda bi, ti: (bi * nt + ti, 0))
    const = lambda a, c: pl.BlockSpec((a, c), lambda bi, ti: (0, 0))
    return pl.pallas_call(
        _gla_kernel,
        grid=(bsz, nt),
        in_specs=[row(GLA_QK), row(GLA_QK), row(GLA_V), row(GLA_V), row(GLA_QK),
                  const(GLA_TG, GLA_TG), const(GLA_V, GLA_V), const(GLA_V, GLA_QK), const(1, GLA_V)],
        out_specs=row(GLA_V),
        out_shape=jax.ShapeDtypeStruct(gv.shape, BF16),
        scratch_shapes=[pltpu.VMEM((GLA_V, GLA_QK), F32),
                        pltpu.VMEM((GLA_TG // CHUNK, GLA_V, GLA_QK), F32),
                        pltpu.VMEM((GLA_TG // CHUNK, GLA_V, GLA_QK), BF16),
                        pltpu.VMEM((GLA_TG, GLA_V), F32)],
        compiler_params=_cparams(("arbitrary", "arbitrary")),
        name="gla",
    )(gq, gk, gv, gr, la, tri, gmat, head_mask, ng)


def _mix_residual(h_ref, yc_ref, yd_ref, yg_ref, wo_ref):
    acc = jnp.dot(yc_ref[...], wo_ref[0, 0:256, :], preferred_element_type=F32)
    acc = acc + jnp.dot(yd_ref[...], wo_ref[0, 256:768, :], preferred_element_type=F32)
    acc = acc + jnp.dot(yg_ref[...], wo_ref[0, 768:1024, :], preferred_element_type=F32)
    return h_ref[...] + acc


def _swiglu_step(xb, wg_ref, wu_ref, wd_ref, acc):
    x = xb[...]
    g = jnp.dot(x, wg_ref[0, 0], preferred_element_type=F32)
    u = jnp.dot(x, wu_ref[0, 0], preferred_element_type=F32)
    a = (_silu(g) * u).astype(BF16)
    acc[...] += jnp.dot(a, wd_ref[0, 0], preferred_element_type=F32)


def _ffn_kernel(h_ref, yc_ref, yd_ref, yg_ref, wo_ref, g_ref, wg_ref, wu_ref, wd_ref, o_ref,
                h1, xb, acc):
    j = pl.program_id(1)

    @pl.when(j == 0)
    def _():
        x = _mix_residual(h_ref, yc_ref, yd_ref, yg_ref, wo_ref)
        h1[...] = x
        xb[...] = _rms(x, g_ref[...]).astype(BF16)
        acc[...] = jnp.zeros(acc.shape, F32)

    _swiglu_step(xb, wg_ref, wu_ref, wd_ref, acc)

    @pl.when(j == pl.num_programs(1) - 1)
    def _():
        o_ref[...] = h1[...] + acc[...]


def _ffn(h, yc, yd, yg, wo_stack, layer, g, w_gu, w_down, ffn_layer):
    t = h.shape[0]
    nj = D_FF // FFN_FB
    row = lambda w_: pl.BlockSpec((TM, w_), lambda i, j: (i, 0))
    return pl.pallas_call(
        _ffn_kernel,
        grid=(t // TM, nj),
        in_specs=[row(D_MODEL), row(CONV_CH), row(DIFF_W), row(GLA_V),
                  pl.BlockSpec((1, D_MODEL, D_MODEL), lambda i, j: (layer, 0, 0)),
                  pl.BlockSpec((1, D_MODEL), lambda i, j: (0, 0)),
                  pl.BlockSpec((1, 1, D_MODEL, FFN_FB), lambda i, j: (ffn_layer, 0, 0, j)),
                  pl.BlockSpec((1, 1, D_MODEL, FFN_FB), lambda i, j: (ffn_layer, 0, 0, j + nj)),
                  pl.BlockSpec((1, 1, FFN_FB, D_MODEL), lambda i, j: (ffn_layer, 0, j, 0))],
        out_specs=row(D_MODEL),
        out_shape=jax.ShapeDtypeStruct((t, D_MODEL), F32),
        scratch_shapes=[pltpu.VMEM((TM, D_MODEL), F32), pltpu.VMEM((TM, D_MODEL), BF16),
                        pltpu.VMEM((TM, D_MODEL), F32)],
        compiler_params=_cparams(("parallel", "arbitrary")),
        name="ffn",
    )(h, yc, yd, yg, wo_stack, g, w_gu, w_gu, w_down)


def _tile_rows_load(ref, n):
    return jnp.concatenate([ref[pl.ds(s, n, stride=ROW_TILE), :] for s in range(ROW_TILE)], axis=1)


def _tile_rows_store(ref, x, n):
    for s in range(ROW_TILE):
        ref[pl.ds(s, n, stride=ROW_TILE), :] = x[:, s * LANES:(s + 1) * LANES]


def _experts_kernel(be_ref, nu_ref, x_ref, wg_ref, wu_ref, wd_ref, o_ref, xb, acc):
    i = pl.program_id(0)
    j = pl.program_id(1)
    used = i < nu_ref[0]

    @pl.when(used)
    def _():
        @pl.when(j == 0)
        def _():
            xb[...] = _tile_rows_load(x_ref, MOE_BLK).astype(BF16)
            acc[...] = jnp.zeros(acc.shape, F32)

        _swiglu_step(xb, wg_ref, wu_ref, wd_ref, acc)

    @pl.when(j == pl.num_programs(1) - 1)
    def _():
        _tile_rows_store(o_ref, jnp.where(used, acc[...], 0.0), MOE_BLK)


def _experts(xs, w_gu, w_down, layer, block_e, n_used):
    nb = xs.shape[0] // (MOE_BLK * ROW_TILE)
    nj = D_FF_EXPERT // MOE_FB

    def last(i, nu):
        return jnp.minimum(i, nu[0] - 1)

    def hid(i, j, nu):
        return jnp.where(i < nu[0], j, nj - 1)

    xspec = pl.BlockSpec((MOE_BLK * ROW_TILE, LANES), lambda i, j, be, nu: (last(i, nu), 0))
    wg = pl.BlockSpec((1, 1, D_MODEL, MOE_FB),
                      lambda i, j, be, nu: (layer, be[last(i, nu)], 0, hid(i, j, nu)))
    wu = pl.BlockSpec((1, 1, D_MODEL, MOE_FB),
                      lambda i, j, be, nu: (layer, be[last(i, nu)], 0, hid(i, j, nu) + nj))
    wd = pl.BlockSpec((1, 1, MOE_FB, D_MODEL),
                      lambda i, j, be, nu: (layer, be[last(i, nu)], hid(i, j, nu), 0))
    grid_spec = pltpu.PrefetchScalarGridSpec(
        num_scalar_prefetch=2,
        grid=(nb, nj),
        in_specs=[xspec, wg, wu, wd],
        out_specs=pl.BlockSpec((MOE_BLK * ROW_TILE, LANES), lambda i, j, be, nu: (i, 0)),
        scratch_shapes=[pltpu.VMEM((MOE_BLK, D_MODEL), BF16), pltpu.VMEM((MOE_BLK, D_MODEL), F32)],
    )
    return pl.pallas_call(
        _experts_kernel,
        grid_spec=grid_spec,
        out_shape=jax.ShapeDtypeStruct(xs.shape, F32),
        compiler_params=_cparams(("arbitrary", "arbitrary")),
        name="experts",
    )(block_e, n_used, xs, w_gu, w_gu, w_down)


def _router_kernel(h_ref, yc_ref, yd_ref, yg_ref, wo_ref, g_ref, rw_ref, tri_ref,
                   h1_ref, hn_ref, info_ref, cnt_ref, carry):
    @pl.when(pl.program_id(0) == 0)
    def _():
        carry[...] = jnp.zeros(carry.shape, F32)

    h1 = _mix_residual(h_ref, yc_ref, yd_ref, yg_ref, wo_ref)
    h1_ref[...] = h1
    hn = _rms(h1, g_ref[...])
    _tile_rows_store(hn_ref, hn, TM)
    logits = _dot_f32(hn, rw_ref[...])
    lane = lax.broadcasted_iota(jnp.int32, logits.shape, 1)
    lg = jnp.where(lane < N_EXPERTS, logits, -jnp.inf)
    v1 = jnp.max(lg, axis=-1, keepdims=True)
    i1 = jnp.min(jnp.where(lg == v1, lane, LANES), axis=-1, keepdims=True)
    lg2 = jnp.where(lane == i1, -jnp.inf, lg)
    v2 = jnp.max(lg2, axis=-1, keepdims=True)
    i2 = jnp.min(jnp.where(lg2 == v2, lane, LANES), axis=-1, keepdims=True)
    e2 = jnp.exp(v2 - v1)
    g1 = 1.0 / (1.0 + e2)
    g2 = e2 / (1.0 + e2)
    pick1 = lane == i1
    pick2 = lane == i2
    onehot = jnp.where(pick1 | pick2, 1.0, 0.0)
    before = jnp.dot(tri_ref[...], onehot.astype(BF16), preferred_element_type=F32) + carry[...]
    r1 = jnp.sum(jnp.where(pick1, before, 0.0), axis=-1, keepdims=True)
    r2 = jnp.sum(jnp.where(pick2, before, 0.0), axis=-1, keepdims=True)
    carry[...] += jnp.sum(onehot, axis=0, keepdims=True)
    info = jnp.where(lane == 0, i1.astype(F32), 0.0)
    info = jnp.where(lane == 1, i2.astype(F32), info)
    info = jnp.where(lane == 2, r1, info)
    info = jnp.where(lane == 3, r2, info)
    info = jnp.where(lane == 4, g1, info)
    info = jnp.where(lane == 5, g2, info)
    info_ref[...] = info
    cnt_ref[...] = jnp.broadcast_to(carry[...], cnt_ref.shape)


def _router(h, yc, yd, yg, wo_stack, layer, g, rw, tri):
    t = h.shape[0]
    row = lambda w_: pl.BlockSpec((TM, w_), lambda i: (i, 0))
    const = lambda a, c: pl.BlockSpec((a, c), lambda i: (0, 0))
    return pl.pallas_call(
        _router_kernel,
        grid=(t // TM,),
        in_specs=[row(D_MODEL), row(CONV_CH), row(DIFF_W), row(GLA_V),
                  _layer_spec(layer, D_MODEL, D_MODEL),
                  const(1, D_MODEL), const(D_MODEL, LANES), const(TM, TM)],
        out_specs=[row(D_MODEL), pl.BlockSpec((TM * ROW_TILE, LANES), lambda i: (i, 0)), row(LANES),
                   const(SUBLANES, LANES)],
        out_shape=[jax.ShapeDtypeStruct((t, D_MODEL), F32),
                   jax.ShapeDtypeStruct((t * ROW_TILE, LANES), F32),
                   jax.ShapeDtypeStruct((t, LANES), F32),
                   jax.ShapeDtypeStruct((SUBLANES, LANES), F32)],
        scratch_shapes=[pltpu.VMEM((1, LANES), F32)],
        compiler_params=_cparams(("arbitrary",)),
        name="router",
    )(h, yc, yd, yg, wo_stack, g, rw, tri)


def _row_copy(src, dst, sem):
    return pltpu.make_async_copy(src, dst, sem)


def _tile_at(ref, start):
    if not isinstance(start, int):
        start = pl.multiple_of(start, ROW_TILE)
    return ref.at[pl.ds(start, ROW_TILE), :]


def _dispatch_kernel(dest_ref, hn_ref, xs_in_ref, xs_ref, sem):
    del xs_in_ref

    def issue(t, carry):
        src = _tile_at(hn_ref, t * ROW_TILE)
        _row_copy(src, _tile_at(xs_ref, dest_ref[0, 0, 2 * t]), sem).start()
        _row_copy(src, _tile_at(xs_ref, dest_ref[0, 0, 2 * t + 1]), sem).start()
        return carry

    lax.fori_loop(0, TMD, issue, 0, unroll=DMA_UNROLL)

    def drain(t, carry):
        _row_copy(_tile_at(hn_ref, 0), _tile_at(xs_ref, 0), sem).wait()
        _row_copy(_tile_at(hn_ref, 0), _tile_at(xs_ref, 0), sem).wait()
        return carry

    lax.fori_loop(0, TMD, drain, 0, unroll=DMA_UNROLL)


def _dispatch(dest, hn, xs_zero):
    t = hn.shape[0] // ROW_TILE
    return pl.pallas_call(
        _dispatch_kernel,
        grid=(t // TMD,),
        in_specs=[pl.BlockSpec((1, 1, 2 * TMD), lambda i: (i, 0, 0), memory_space=pltpu.SMEM),
                  pl.BlockSpec((TMD * ROW_TILE, LANES), lambda i: (i, 0)),
                  pl.BlockSpec(memory_space=pl.ANY)],
        out_specs=pl.BlockSpec(memory_space=pl.ANY),
        out_shape=jax.ShapeDtypeStruct(xs_zero.shape, F32),
        scratch_shapes=[pltpu.SemaphoreType.DMA(())],
        input_output_aliases={2: 0},
        compiler_params=_cparams(("arbitrary",)),
        name="dispatch",
    )(dest, hn, xs_zero)


def _ple_math(h, p_blk, g_ref, wg_ref, wu_ref, fg_ref, final):
    hn = _rms(h, g_ref[...]).astype(BF16)
    gate = _sigmoid(jnp.dot(hn, wg_ref[0], preferred_element_type=F32))
    up = jnp.dot(p_blk.astype(BF16), wu_ref[0], preferred_element_type=F32)
    out = h + gate * up
    if final:
        out = _rms(out, fg_ref[...])
    return out


def _combine_kernel(dcur_ref, dnext_ref, h_ref, info_ref, p_ref, g_ref, wg_ref, wu_ref, fg_ref, ys_ref,
                    o_ref, buf, sems, *, final):
    i = pl.program_id(0)
    slot = i % 2

    def gather(dref, slot_):
        def issue(t, carry):
            for pick in range(2):
                _row_copy(_tile_at(ys_ref, dref[0, 0, 2 * t + pick]),
                          _tile_at(buf.at[slot_, pick], t * ROW_TILE), sems.at[slot_]).start()
            return carry

        lax.fori_loop(0, TMD, issue, 0, unroll=DMA_UNROLL)

    @pl.when(i == 0)
    def _():
        gather(dcur_ref, 0)

    @pl.when(i + 1 < pl.num_programs(0))
    def _():
        gather(dnext_ref, 1 - slot)

    def drain(t, carry):
        for pick in range(2):
            _row_copy(_tile_at(ys_ref, 0), _tile_at(buf.at[slot, pick], 0), sems.at[slot]).wait()
        return carry

    lax.fori_loop(0, TMD, drain, 0, unroll=DMA_UNROLL)
    g1 = info_ref[:, 4:5]
    g2 = info_ref[:, 5:6]
    y1 = _tile_rows_load(buf.at[slot, 0], TMD)
    y2 = _tile_rows_load(buf.at[slot, 1], TMD)
    h2 = h_ref[...] + (g1 * y1 + g2 * y2)
    o_ref[...] = _ple_math(h2, p_ref[0], g_ref, wg_ref, wu_ref, fg_ref, final)


def _combine(dest, h, info, ys, p_stack, g, wg_stack, wu_stack, layer, fg, final):
    t = h.shape[0]
    n = t // TMD
    row = lambda w_: pl.BlockSpec((TMD, w_), lambda i: (i, 0))
    const = lambda a, c: pl.BlockSpec((a, c), lambda i: (0, 0))
    return pl.pallas_call(
        functools.partial(_combine_kernel, final=final),
        grid=(n,),
        in_specs=[pl.BlockSpec((1, 1, 2 * TMD), lambda i: (i, 0, 0), memory_space=pltpu.SMEM),
                  pl.BlockSpec((1, 1, 2 * TMD), lambda i: (jnp.minimum(i + 1, n - 1), 0, 0),
                               memory_space=pltpu.SMEM),
                  row(D_MODEL), row(LANES),
                  pl.BlockSpec((1, TMD, PLE_DIM), lambda i: (layer, i, 0)),
                  const(1, D_MODEL), _layer_spec(layer, D_MODEL, D_MODEL),
                  _layer_spec(layer, PLE_DIM, D_MODEL), const(1, D_MODEL),
                  pl.BlockSpec(memory_space=pl.ANY)],
        out_specs=row(D_MODEL),
        out_shape=jax.ShapeDtypeStruct(h.shape, F32),
        scratch_shapes=[pltpu.VMEM((2, 2, TMD * ROW_TILE, LANES), F32), pltpu.SemaphoreType.DMA((2,))],
        compiler_params=_cparams(("arbitrary",)),
        name="combine",
    )(dest, dest, h, info, p_stack, g, wg_stack, wu_stack, fg, ys)


def _moe(h, yc, yd, yg, wo_stack, layer, g, rw, w_gu, w_down, moe_layer, tri_strict, ple_args):
    t = h.shape[0]
    h1, hn, info, cnt = _router(h, yc, yd, yg, wo_stack, layer, g, rw, tri_strict)
    counts = cnt[0, :N_EXPERTS].astype(jnp.int32)
    padded = ((counts + MOE_BLK - 1) // MOE_BLK) * MOE_BLK
    pend = jnp.cumsum(padded)
    pstart = pend - padded
    e = info[:, 0:2].astype(jnp.int32)
    r = info[:, 2:4].astype(jnp.int32)
    onehot = e[:, :, None] == jnp.arange(N_EXPERTS, dtype=jnp.int32)[None, None, :]
    dest = r + jnp.sum(jnp.where(onehot, pstart[None, None, :], 0), axis=-1)
    dest = (dest * ROW_TILE).reshape(t // TMD, 1, 2 * TMD)
    m_rows = 2 * t + N_EXPERTS * MOE_BLK
    nb = m_rows // MOE_BLK
    blk_start = jnp.arange(nb, dtype=jnp.int32) * MOE_BLK
    block_e = jnp.minimum(jnp.sum(blk_start[:, None] >= pend[None, :], axis=-1), N_EXPERTS - 1)
    n_used = (pend[-1:] // MOE_BLK).astype(jnp.int32)

    xs = _dispatch(dest, hn, jnp.zeros((m_rows * ROW_TILE, LANES), F32))
    ys = _experts(xs, w_gu, w_down, moe_layer, block_e.astype(jnp.int32), n_used)
    return _combine(dest, h1, info, ys, *ple_args)


def _ple_kernel(h_ref, p_ref, g_ref, wg_ref, wu_ref, fg_ref, o_ref, *, final):
    o_ref[...] = _ple_math(h_ref[...], p_ref[0], g_ref, wg_ref, wu_ref, fg_ref, final)


def _ple(h, p_stack, g, wg_stack, wu_stack, layer, fg, final):
    t = h.shape[0]
    row = lambda w_: pl.BlockSpec((TM, w_), lambda i: (i, 0))
    const = lambda a, c: pl.BlockSpec((a, c), lambda i: (0, 0))
    return pl.pallas_call(
        functools.partial(_ple_kernel, final=final),
        grid=(t // TM,),
        in_specs=[row(D_MODEL), pl.BlockSpec((1, TM, PLE_DIM), lambda i: (layer, i, 0)),
                  const(1, D_MODEL), _layer_spec(layer, D_MODEL, D_MODEL),
                  _layer_spec(layer, PLE_DIM, D_MODEL), const(1, D_MODEL)],
        out_specs=row(D_MODEL),
        out_shape=jax.ShapeDtypeStruct(h.shape, F32),
        compiler_params=_cparams(("parallel",)),
        name="ple",
    )(h, p_stack, g, wg_stack, wu_stack, fg)


def _rope_tables(seq):
    half = ROT_DIMS // 2
    pos = jnp.arange(seq, dtype=F32)
    inv_freq = ROPE_THETA ** (-jnp.arange(0, ROT_DIMS, 2, dtype=F32) / ROT_DIMS)
    ang = pos[:, None] * inv_freq[None, :]
    cos, sin = jnp.cos(ang), jnp.sin(ang)
    lane = np.arange(LANES) % DIFF_DH
    fidx = lane % half
    first = jnp.asarray(lane < half)[None, :]
    second = jnp.asarray((lane >= half) & (lane < ROT_DIMS))[None, :]
    ra = jnp.where(first | second, cos[:, fidx], 1.0)
    rb = jnp.where(second, sin[:, fidx], 0.0)
    rc = jnp.where(first, -sin[:, fidx], 0.0)
    return ra, rb, rc


def kernel(x, p, norm_mix_g, w_in, conv_w, conv_b, conv_ln_g, conv_ln_b, diff_lambda, diff_subln_g,
           gla_w_gate2, gla_b_gate, gla_norm_g, w_out, norm_ffn_g, ffn_w_gu, ffn_w_down, router_w,
           moe_w_gu, moe_w_down, ple_w_up, ple_w_gate, ple_norm_g, final_norm_g):
    bsz, seq, d = x.shape
    depth = w_in.shape[0]
    t = bsz * seq
    assert d == D_MODEL and seq % TM == 0 and seq % TQ == 0 and t % TMD == 0

    ropes = _rope_tables(seq)
    idx = np.arange(GLA_TG)
    tri_chunk = jnp.asarray(((idx[:, None] >= idx[None, :])
                             & (idx[:, None] // CHUNK == idx[None, :] // CHUNK)).astype(np.float32)).astype(BF16)
    vi = np.arange(GLA_V)
    ki = np.arange(GLA_QK)
    gmat = jnp.asarray((vi[:, None] // GLA_DV == vi[None, :] // GLA_DV).astype(np.float32)
                       / GLA_DV).astype(BF16)
    head_mask = jnp.asarray((vi[:, None] // GLA_DV == ki[None, :] // GLA_DK).astype(np.float32))
    ti = np.arange(TM)
    tri_strict = jnp.asarray((ti[:, None] > ti[None, :]).astype(np.float32)).astype(BF16)

    w_in_b = jnp.pad(w_in, ((0, 0), (0, 0), (0, D_IN_PAD - D_IN))).astype(BF16)
    w_out_b = w_out.astype(BF16)
    ffn_gu_b = ffn_w_gu.astype(BF16)[:, None]
    ffn_down_b = ffn_w_down.astype(BF16)[:, None]
    moe_gu_b = moe_w_gu.astype(BF16)
    moe_down_b = moe_w_down.astype(BF16)
    ple_gate_b = ple_w_gate.astype(BF16)
    ple_up_b = ple_w_up.astype(BF16)
    p_rows = p.reshape(depth, t, PLE_DIM)

    h = x.reshape(t, D_MODEL)
    for i in range(depth):
        lam_init = 0.8 - 0.6 * math.exp(-0.3 * i)
        wg2 = jnp.pad(gla_w_gate2[i], ((0, LANES - GLA_GATE_RANK), (0, 0)))
        u, dq, dk, dv, gq, gk, gv, gr, la = _inproj(
            h, norm_mix_g[i][None, :], w_in_b, i, ropes, wg2, gla_b_gate[i][None, :], seq)
        y_conv = _conv(u, conv_w[i], conv_b[i][None, :], conv_ln_g[i][None, :],
                       conv_ln_b[i][None, :], bsz, seq)
        y_diff = _attn(dq, dk, dv, diff_lambda[i], diff_subln_g[i][None, :], bsz, seq, lam_init)
        y_gla = _gla(gq, gk, gv, gr, la, tri_chunk, gmat, head_mask,
                     jnp.tile(gla_norm_g[i], GLA_HEADS)[None, :], bsz, seq)
        g_ffn = norm_ffn_g[i][None, :]
        j = i // 2
        ple_args = (p_rows, ple_norm_g[i][None, :], ple_gate_b, ple_up_b, i, final_norm_g[None, :],
                    i == depth - 1)
        if i % 2 == 0:
            h = _ffn(h, y_conv, y_diff, y_gla, w_out_b, i, g_ffn, ffn_gu_b, ffn_down_b, j)
            h = _ple(h, *ple_args)
        else:
            rw = jnp.pad(router_w[j], ((0, 0), (0, LANES - N_EXPERTS)))
            h = _moe(h, y_conv, y_diff, y_gla, w_out_b, i, g_ffn, rw, moe_gu_b, moe_down_b, j,
                     tri_strict, ple_args)
    return h.reshape(bsz, seq, D_MODEL)
```

```python
import functools
import math

import jax
import jax.numpy as jnp
import numpy as np
from jax import lax
from jax.experimental import pallas as pl
from jax.experimental.pallas import tpu as pltpu

F32 = jnp.float32
BF16 = jnp.bfloat16

D_MODEL = 1024
CHUNK = 64
CONV_CH = 256
CONV_WIDTH = 31
DIFF_HEADS = 4
DIFF_DV = 128
DIFF_DH = 64
GLA_HEADS = 4
GLA_DV = 64
GLA_DK = 32
GLA_GATE_RANK = 16
GLA_TAU = 16.0
ROPE_THETA = 500000.0
ROT_DIMS = 16
D_FF = 2816
N_EXPERTS = 8
D_FF_EXPERT = 3584
PLE_DIM = 256
EPS = 1e-6
DIFF_W = 512
GLA_QK = GLA_HEADS * GLA_DK
GLA_V = GLA_HEADS * GLA_DV
D_IN = 2832
D_IN_PAD = 2944

LANES = 128
SUBLANES = 8
VMEM_LIMIT = 56 * 1024 * 1024
LOG2E = math.log2(math.e)

TM = 512
TK = 512
TQ = 2 * TK
CONV_TS = 512
CONV_HALO = 32
CONV_SH_ROWS = CONV_TS + CONV_HALO - SUBLANES
GLA_TG = 512
FFN_FB = 1408
MOE_BLK = 512
MOE_FB = 1792
ROW_TILE = D_MODEL // LANES
TMD = 256
DMA_UNROLL = 8


def _cparams(sem):
    return pltpu.CompilerParams(dimension_semantics=sem, vmem_limit_bytes=VMEM_LIMIT)


def _rms(x, g):
    ms = jnp.mean(x * x, axis=-1, keepdims=True)
    return x * lax.rsqrt(ms + EPS) * g


def _sigmoid(x):
    return 1.0 / (1.0 + jnp.exp(-x))


def _silu(x):
    return x * _sigmoid(x)


def _split_bf16(x):
    hi = x.astype(BF16)
    return hi, (x - hi.astype(F32)).astype(BF16)


def _dot_f32(a, b):
    a_hi, a_lo = _split_bf16(a)
    b_hi, b_lo = _split_bf16(b)
    dot = functools.partial(jnp.dot, preferred_element_type=F32)
    return dot(a_hi, b_hi) + (dot(a_hi, b_lo) + dot(a_lo, b_hi))


def _rope(x, ra, rb, rc):
    outs = []
    for c in range(x.shape[1] // LANES):
        xc = x[:, c * LANES:(c + 1) * LANES]
        outs.append(xc * ra + pltpu.roll(xc, ROT_DIMS // 2, 1) * rb
                    + pltpu.roll(xc, LANES - ROT_DIMS // 2, 1) * rc)
    return jnp.concatenate(outs, axis=1)


N_INPROJ_IN = 7
N_INPROJ_OUT = 9


def _inproj_math(x, in_refs, out_refs):
    g_ref, w_ref, ra_ref, rb_ref, rc_ref, wg2_ref, bg_ref = in_refs
    u_ref, q_ref, k_ref, v_ref, gq_ref, gk_ref, gv_ref, gr_ref, la_ref = out_refs
    hn = _rms(x, g_ref[...]).astype(BF16)

    def proj(a, b):
        return jnp.dot(hn, w_ref[0, :, a:b], preferred_element_type=F32)

    ra, rb, rc = ra_ref[...], rb_ref[...], rc_ref[...]
    u_ref[...] = proj(0, 256) * _sigmoid(proj(256, 512))
    q_ref[...] = (_rope(proj(512, 1024), ra, rb, rc) * (LOG2E * DIFF_DH ** -0.5)).astype(BF16)
    k_ref[...] = _rope(proj(1024, 1536), ra, rb, rc).astype(BF16)
    v_ref[...] = proj(1536, 2048).astype(BF16)
    gq_ref[...] = proj(2048, 2176) * (GLA_DK ** -0.5)
    gk_ref[...] = proj(2176, 2304)
    gv_ref[...] = proj(2304, 2560)
    gr_ref[...] = _silu(proj(2560, 2816))
    gz = proj(2816, D_IN_PAD)
    ga = _dot_f32(gz, wg2_ref[...]) + bg_ref[...]
    la_ref[...] = (jnp.minimum(ga, 0.0) - jnp.log(1.0 + jnp.exp(-jnp.abs(ga)))) * (1.0 / GLA_TAU)


def _inproj_kernel(h_ref, *refs):
    _inproj_math(h_ref[...], refs[:N_INPROJ_IN], refs[N_INPROJ_IN:])


def _layer_spec(layer, a, b):
    return pl.BlockSpec((1, a, b), lambda i: (layer, 0, 0))


def _inproj_specs(layer, tile, t, seq):
    nseq = seq // tile
    const = lambda a, b: pl.BlockSpec((a, b), lambda i: (0, 0))
    rope_spec = pl.BlockSpec((tile, LANES), lambda i: (i % nseq, 0))
    outs = [(CONV_CH, F32), (DIFF_W, BF16), (DIFF_W, BF16), (DIFF_W, BF16),
            (GLA_QK, F32), (GLA_QK, F32), (GLA_V, F32), (GLA_V, F32), (GLA_QK, F32)]
    in_specs = [const(1, D_MODEL), _layer_spec(layer, D_MODEL, D_IN_PAD),
                rope_spec, rope_spec, rope_spec, const(LANES, LANES), const(1, LANES)]
    out_specs = [pl.BlockSpec((tile, w_), lambda i: (i, 0)) for w_, _ in outs]
    out_shape = [jax.ShapeDtypeStruct((t, w_), dt) for w_, dt in outs]
    return in_specs, out_specs, out_shape


def _inproj(h, inproj_args, seq):
    t = h.shape[0]
    in_specs, out_specs, out_shape = _inproj_specs(inproj_args[0], TM, t, seq)
    return pl.pallas_call(
        _inproj_kernel,
        grid=(t // TM,),
        in_specs=[pl.BlockSpec((TM, D_MODEL), lambda i: (i, 0))] + in_specs,
        out_specs=out_specs,
        out_shape=out_shape,
        compiler_params=_cparams(("parallel",)),
        name="inproj",
    )(h, *inproj_args[1:])


def _conv_kernel(u_ref, w_ref, b_ref, lg_ref, lb_ref, o_ref, win, shifted):
    @pl.when(pl.program_id(1) == 0)
    def _():
        win[0:CONV_HALO, :] = jnp.zeros((CONV_HALO, CONV_CH), F32)

    win[CONV_HALO:, :] = u_ref[...]
    for r in range(1, SUBLANES):
        shifted[r - 1] = win[r:r + CONV_SH_ROWS, :]
    rows = 64
    base = CONV_HALO - (CONV_WIDTH - 1)
    for r0 in range(0, CONV_TS, rows):
        acc = jnp.zeros((rows, CONV_CH), F32) + b_ref[...]
        for j in range(CONV_WIDTH):
            res = (base + j) % SUBLANES
            start = r0 + (base + j) - res
            if res == 0:
                tap = win[start:start + rows, :]
            else:
                tap = shifted[res - 1, start:start + rows, :]
            acc = acc + tap * w_ref[j:j + 1, :]
        mu = jnp.mean(acc, axis=-1, keepdims=True)
        xc = acc - mu
        var = jnp.mean(xc * xc, axis=-1, keepdims=True)
        y = xc * lax.rsqrt(var + EPS) * lg_ref[...] + lb_ref[...]
        o_ref[r0:r0 + rows, :] = _silu(y).astype(BF16)
    win[0:CONV_HALO, :] = win[CONV_TS:CONV_TS + CONV_HALO, :]


def _conv(u, w, b, lg, lb, bsz, seq):
    nt = seq // CONV_TS
    const = lambda a, c: pl.BlockSpec((a, c), lambda bi, ti: (0, 0))
    return pl.pallas_call(
        _conv_kernel,
        grid=(bsz, nt),
        in_specs=[pl.BlockSpec((CONV_TS, CONV_CH), lambda bi, ti: (bi * nt + ti, 0)),
                  const(CONV_WIDTH, CONV_CH), const(1, CONV_CH), const(1, CONV_CH), const(1, CONV_CH)],
        out_specs=pl.BlockSpec((CONV_TS, CONV_CH), lambda bi, ti: (bi * nt + ti, 0)),
        out_shape=jax.ShapeDtypeStruct(u.shape, BF16),
        scratch_shapes=[pltpu.VMEM((CONV_TS + CONV_HALO, CONV_CH), F32),
                        pltpu.VMEM((SUBLANES - 1, CONV_SH_ROWS, CONV_CH), F32)],
        compiler_params=_cparams(("arbitrary", "arbitrary")),
        name="conv",
    )(u, w, b, lg, lb)


def _attn_kernel(qt_ref, kt_ref, lam_ref, sg_ref, q_ref, k_ref, v_ref, o_ref, m_scr, acc_scr,
                 *, lam_init):
    step = pl.program_id(1)
    qi = qt_ref[step]
    kj = kt_ref[step]

    @pl.when(kj == 0)
    def _():
        m_scr[...] = jnp.full(m_scr.shape, -jnp.inf, F32)
        acc_scr[...] = jnp.zeros(acc_scr.shape, F32)

    def update(sub, masked):
        half = TK // 2
        pieces = [(0, half, half), (half, half, TK)] if masked else [(0, TK, TK)]
        for r0, nr, nk in pieces:
            rows = slice(sub * TK + r0, sub * TK + r0 + nr)
            if masked:
                rq = (lax.broadcasted_iota(jnp.int32, (nr, nk), 0) + r0) // CHUNK
                ck = lax.broadcasted_iota(jnp.int32, (nr, nk), 1) // CHUNK
                allowed = ck <= rq
            for h in range(DIFF_HEADS):
                hs = slice(h * LANES, (h + 1) * LANES)
                q = q_ref[rows, hs]
                k = k_ref[0:nk, hs]
                v = v_ref[0:nk, hs]
                vext = jnp.concatenate([v, jnp.ones_like(v)], axis=1)
                lane = lax.broadcasted_iota(jnp.int32, q.shape, 1)
                for c in range(2):
                    sel = (lane < DIFF_DH) if c == 0 else (lane >= DIFF_DH)
                    qc = jnp.where(sel, q, jnp.zeros_like(q))
                    s = lax.dot_general(qc, k, (((1,), (1,)), ((), ())), preferred_element_type=F32)
                    if masked:
                        s = jnp.where(allowed, s, -jnp.inf)
                    m_old = m_scr[2 * h + c, rows]
                    m_new = jnp.maximum(m_old, jnp.max(s, axis=-1, keepdims=True))
                    alpha = jnp.exp2(m_old - m_new)
                    p = jnp.exp2(s - jnp.tile(m_new, (1, nk // LANES)))
                    pv = jnp.dot(p.astype(BF16), vext, preferred_element_type=F32)
                    acc_scr[2 * h + c, rows] = jnp.tile(alpha, (1, 2)) * acc_scr[2 * h + c, rows] + pv
                    m_scr[2 * h + c, rows] = m_new

    @pl.when(kj < 2 * qi)
    def _():
        update(0, False)
        update(1, False)

    @pl.when(kj == 2 * qi)
    def _():
        update(0, True)
        update(1, False)

    @pl.when(kj == 2 * qi + 1)
    def _():
        update(1, True)
        lp = lam_ref[...]
        lam = (jnp.exp(jnp.sum(lp[0:1] * lp[1:2], axis=-1, keepdims=True))
               - jnp.exp(jnp.sum(lp[2:3] * lp[3:4], axis=-1, keepdims=True)) + lam_init)
        for h in range(DIFF_HEADS):
            a0 = acc_scr[2 * h]
            a1 = acc_scr[2 * h + 1]
            o = a0[:, :DIFF_DV] / a0[:, DIFF_DV:] - lam * (a1[:, :DIFF_DV] / a1[:, DIFF_DV:])
            o_ref[:, h * LANES:(h + 1) * LANES] = (_rms(o, sg_ref[...]) * (1.0 - lam_init)).astype(BF16)


def _attn(q, k, v, lam_p, sg, bsz, seq, lam_init):
    nq = seq // TQ
    nk = seq // TK
    pairs = [(a, b) for a in range(nq) for b in range(2 * a + 2)]
    qt = jnp.asarray(np.array([a for a, _ in pairs], np.int32))
    kt = jnp.asarray(np.array([b for _, b in pairs], np.int32))
    qspec = pl.BlockSpec((TQ, DIFF_W), lambda b, s, qt_, kt_: (b * nq + qt_[s], 0))
    kspec = pl.BlockSpec((TK, DIFF_W), lambda b, s, qt_, kt_: (b * nk + kt_[s], 0))
    const = lambda a, c: pl.BlockSpec((a, c), lambda b, s, qt_, kt_: (0, 0))
    grid_spec = pltpu.PrefetchScalarGridSpec(
        num_scalar_prefetch=2,
        grid=(bsz, len(pairs)),
        in_specs=[const(4, DIFF_DH), const(1, DIFF_DV), qspec, kspec, kspec],
        out_specs=qspec,
        scratch_shapes=[pltpu.VMEM((2 * DIFF_HEADS, TQ, LANES), F32),
                        pltpu.VMEM((2 * DIFF_HEADS, TQ, 2 * DIFF_DV), F32)],
    )
    return pl.pallas_call(
        functools.partial(_attn_kernel, lam_init=lam_init),
        grid_spec=grid_spec,
        out_shape=jax.ShapeDtypeStruct(q.shape, BF16),
        compiler_params=_cparams(("parallel", "arbitrary")),
        name="diffattn",
    )(qt, kt, lam_p, sg, q, k, v)


def _gla_kernel(q_ref, k_ref, v_ref, r_ref, la_ref, tri_ref, gmat_ref, mask_ref, ng_ref, o_ref,
                st, kv_scr, st_scr, o_scr):
    @pl.when(pl.program_id(1) == 0)
    def _():
        st[...] = jnp.zeros(st.shape, F32)

    la = la_ref[...]
    la_hi = la.astype(BF16)
    la_lo = (la - la_hi.astype(F32)).astype(BF16)
    cum = (jnp.dot(tri_ref[...], la_hi, preferred_element_type=F32)
           + jnp.dot(tri_ref[...], la_lo, preferred_element_type=F32))
    same_head = mask_ref[...] > 0.0
    nch = GLA_TG // CHUNK
    tots = []
    for c in range(nch):
        sl = slice(c * CHUNK, (c + 1) * CHUNK)
        cum_c = cum[sl]
        tot = cum_c[CHUNK - 1:CHUNK]
        tots.append(tot)
        kdec = (k_ref[sl, :] * jnp.exp(tot - cum_c)).astype(BF16)
        v_t = v_ref[sl, :].T.astype(BF16)
        kv_scr[c] = jnp.dot(v_t, kdec, preferred_element_type=F32)
    state = st[...]
    for c in range(nch):
        state = state * jnp.exp(tots[c]) + jnp.where(same_head, kv_scr[c], 0.0)
        st_scr[c] = state.astype(BF16)
    st[...] = state
    for c in range(nch):
        sl = slice(c * CHUNK, (c + 1) * CHUNK)
        o_scr[sl, :] = lax.dot_general(q_ref[sl, :].astype(BF16), st_scr[c], (((1,), (1,)), ((), ())),
                                       preferred_element_type=F32)
    o = o_scr[...]
    ms = jnp.dot((o * o).astype(BF16), gmat_ref[...], preferred_element_type=F32)
    o_ref[...] = (o * lax.rsqrt(ms + EPS) * ng_ref[...] * r_ref[...]).astype(BF16)


def _gla(gq, gk, gv, gr, la, tri, gmat, head_mask, ng, bsz, seq):
    nt = seq // GLA_TG
    row = lambda w_: pl.BlockSpec((GLA_TG, w_), lambda bi, ti: (bi * nt + ti, 0))
    const = lambda a, c: pl.BlockSpec((a, c), lambda bi, ti: (0, 0))
    return pl.pallas_call(
        _gla_kernel,
        grid=(bsz, nt),
        in_specs=[row(GLA_QK), row(GLA_QK), row(GLA_V), row(GLA_V), row(GLA_QK),
                  const(GLA_TG, GLA_TG), const(GLA_V, GLA_V), const(GLA_V, GLA_QK), const(1, GLA_V)],
        out_specs=row(GLA_V),
        out_shape=jax.ShapeDtypeStruct(gv.shape, BF16),
        scratch_shapes=[pltpu.VMEM((GLA_V, GLA_QK), F32),
                        pltpu.VMEM((GLA_TG // CHUNK, GLA_V, GLA_QK), F32),
                        pltpu.VMEM((GLA_TG // CHUNK, GLA_V, GLA_QK), BF16),
                        pltpu.VMEM((GLA_TG, GLA_V), F32)],
        compiler_params=_cparams(("arbitrary", "arbitrary")),
        name="gla",
    )(gq, gk, gv, gr, la, tri, gmat, head_mask, ng)


def _mix_residual(h_ref, yc_ref, yd_ref, yg_ref, wo_ref):
    acc = jnp.dot(yc_ref[...], wo_ref[0, 0:256, :], preferred_element_type=F32)
    acc = acc + jnp.dot(yd_ref[...], wo_ref[0, 256:768, :], preferred_element_type=F32)
    acc = acc + jnp.dot(yg_ref[...], wo_ref[0, 768:1024, :], preferred_element_type=F32)
    return h_ref[...] + acc


def _swiglu_step(xb, wg_ref, wu_ref, wd_ref, acc):
    x = xb[...]
    g = jnp.dot(x, wg_ref[0, 0], preferred_element_type=F32)
    u = jnp.dot(x, wu_ref[0, 0], preferred_element_type=F32)
    a = (_silu(g) * u).astype(BF16)
    acc[...] += jnp.dot(a, wd_ref[0, 0], preferred_element_type=F32)


def _ffn_kernel(h_ref, yc_ref, yd_ref, yg_ref, wo_ref, g_ref, wg_ref, wu_ref, wd_ref, o_ref,
                h1, xb, acc):
    j = pl.program_id(1)

    @pl.when(j == 0)
    def _():
        x = _mix_residual(h_ref, yc_ref, yd_ref, yg_ref, wo_ref)
        h1[...] = x
        xb[...] = _rms(x, g_ref[...]).astype(BF16)
        acc[...] = jnp.zeros(acc.shape, F32)

    _swiglu_step(xb, wg_ref, wu_ref, wd_ref, acc)

    @pl.when(j == pl.num_programs(1) - 1)
    def _():
        o_ref[...] = h1[...] + acc[...]


def _ffn(h, yc, yd, yg, wo_stack, layer, g, w_gu, w_down, ffn_layer):
    t = h.shape[0]
    nj = D_FF // FFN_FB
    row = lambda w_: pl.BlockSpec((TM, w_), lambda i, j: (i, 0))
    return pl.pallas_call(
        _ffn_kernel,
        grid=(t // TM, nj),
        in_specs=[row(D_MODEL), row(CONV_CH), row(DIFF_W), row(GLA_V),
                  pl.BlockSpec((1, D_MODEL, D_MODEL), lambda i, j: (layer, 0, 0)),
                  pl.BlockSpec((1, D_MODEL), lambda i, j: (0, 0)),
                  pl.BlockSpec((1, 1, D_MODEL, FFN_FB), lambda i, j: (ffn_layer, 0, 0, j)),
                  pl.BlockSpec((1, 1, D_MODEL, FFN_FB), lambda i, j: (ffn_layer, 0, 0, j + nj)),
                  pl.BlockSpec((1, 1, FFN_FB, D_MODEL), lambda i, j: (ffn_layer, 0, j, 0))],
        out_specs=row(D_MODEL),
        out_shape=jax.ShapeDtypeStruct((t, D_MODEL), F32),
        scratch_shapes=[pltpu.VMEM((TM, D_MODEL), F32), pltpu.VMEM((TM, D_MODEL), BF16),
                        pltpu.VMEM((TM, D_MODEL), F32)],
        compiler_params=_cparams(("parallel", "arbitrary")),
        name="ffn",
    )(h, yc, yd, yg, wo_stack, g, w_gu, w_gu, w_down)


def _tile_rows_load(ref, n):
    return jnp.concatenate([ref[pl.ds(s, n, stride=ROW_TILE), :] for s in range(ROW_TILE)], axis=1)


def _tile_rows_store(ref, x, n):
    for s in range(ROW_TILE):
        ref[pl.ds(s, n, stride=ROW_TILE), :] = x[:, s * LANES:(s + 1) * LANES]


def _experts_kernel(be_ref, nu_ref, x_ref, wg_ref, wu_ref, wd_ref, o_ref, xb, acc):
    i = pl.program_id(0)
    j = pl.program_id(1)
    used = i < nu_ref[0]

    @pl.when(used)
    def _():
        @pl.when(j == 0)
        def _():
            xb[...] = _tile_rows_load(x_ref, MOE_BLK).astype(BF16)
            acc[...] = jnp.zeros(acc.shape, F32)

        _swiglu_step(xb, wg_ref, wu_ref, wd_ref, acc)

    @pl.when(j == pl.num_programs(1) - 1)
    def _():
        _tile_rows_store(o_ref, jnp.where(used, acc[...], 0.0), MOE_BLK)


def _experts(xs, w_gu, w_down, layer, block_e, n_used):
    nb = xs.shape[0] // (MOE_BLK * ROW_TILE)
    nj = D_FF_EXPERT // MOE_FB

    def last(i, nu):
        return jnp.minimum(i, nu[0] - 1)

    def hid(i, j, nu):
        return jnp.where(i < nu[0], j, nj - 1)

    xspec = pl.BlockSpec((MOE_BLK * ROW_TILE, LANES), lambda i, j, be, nu: (last(i, nu), 0))
    wg = pl.BlockSpec((1, 1, D_MODEL, MOE_FB),
                      lambda i, j, be, nu: (layer, be[last(i, nu)], 0, hid(i, j, nu)))
    wu = pl.BlockSpec((1, 1, D_MODEL, MOE_FB),
                      lambda i, j, be, nu: (layer, be[last(i, nu)], 0, hid(i, j, nu) + nj))
    wd = pl.BlockSpec((1, 1, MOE_FB, D_MODEL),
                      lambda i, j, be, nu: (layer, be[last(i, nu)], hid(i, j, nu), 0))
    grid_spec = pltpu.PrefetchScalarGridSpec(
        num_scalar_prefetch=2,
        grid=(nb, nj),
        in_specs=[xspec, wg, wu, wd],
        out_specs=pl.BlockSpec((MOE_BLK * ROW_TILE, LANES), lambda i, j, be, nu: (i, 0)),
        scratch_shapes=[pltpu.VMEM((MOE_BLK, D_MODEL), BF16), pltpu.VMEM((MOE_BLK, D_MODEL), F32)],
    )
    return pl.pallas_call(
        _experts_kernel,
        grid_spec=grid_spec,
        out_shape=jax.ShapeDtypeStruct(xs.shape, F32),
        compiler_params=_cparams(("arbitrary", "arbitrary")),
        name="experts",
    )(block_e, n_used, xs, w_gu, w_gu, w_down)


def _router_kernel(h_ref, yc_ref, yd_ref, yg_ref, wo_ref, g_ref, rw_ref, tri_ref,
                   h1_ref, hn_ref, info_ref, cnt_ref, carry):
    @pl.when(pl.program_id(0) == 0)
    def _():
        carry[...] = jnp.zeros(carry.shape, F32)

    h1 = _mix_residual(h_ref, yc_ref, yd_ref, yg_ref, wo_ref)
    h1_ref[...] = h1
    hn = _rms(h1, g_ref[...])
    _tile_rows_store(hn_ref, hn, TM)
    logits = _dot_f32(hn, rw_ref[...])
    lane = lax.broadcasted_iota(jnp.int32, logits.shape, 1)
    lg = jnp.where(lane < N_EXPERTS, logits, -jnp.inf)
    v1 = jnp.max(lg, axis=-1, keepdims=True)
    i1 = jnp.min(jnp.where(lg == v1, lane, LANES), axis=-1, keepdims=True)
    lg2 = jnp.where(lane == i1, -jnp.inf, lg)
    v2 = jnp.max(lg2, axis=-1, keepdims=True)
    i2 = jnp.min(jnp.where(lg2 == v2, lane, LANES), axis=-1, keepdims=True)
    e2 = jnp.exp(v2 - v1)
    g1 = 1.0 / (1.0 + e2)
    g2 = e2 / (1.0 + e2)
    pick1 = lane == i1
    pick2 = lane == i2
    onehot = jnp.where(pick1 | pick2, 1.0, 0.0)
    before = jnp.dot(tri_ref[...], onehot.astype(BF16), preferred_element_type=F32) + carry[...]
    r1 = jnp.sum(jnp.where(pick1, before, 0.0), axis=-1, keepdims=True)
    r2 = jnp.sum(jnp.where(pick2, before, 0.0), axis=-1, keepdims=True)
    carry[...] += jnp.sum(onehot, axis=0, keepdims=True)
    info = jnp.where(lane == 0, i1.astype(F32), 0.0)
    info = jnp.where(lane == 1, i2.astype(F32), info)
    info = jnp.where(lane == 2, r1, info)
    info = jnp.where(lane == 3, r2, info)
    info = jnp.where(lane == 4, g1, info)
    info = jnp.where(lane == 5, g2, info)
    info_ref[...] = info
    cnt_ref[...] = jnp.broadcast_to(carry[...], cnt_ref.shape)


def _router(h, yc, yd, yg, wo_stack, layer, g, rw, tri):
    t = h.shape[0]
    row = lambda w_: pl.BlockSpec((TM, w_), lambda i: (i, 0))
    const = lambda a, c: pl.BlockSpec((a, c), lambda i: (0, 0))
    return pl.pallas_call(
        _router_kernel,
        grid=(t // TM,),
        in_specs=[row(D_MODEL), row(CONV_CH), row(DIFF_W), row(GLA_V),
                  _layer_spec(layer, D_MODEL, D_MODEL),
                  const(1, D_MODEL), const(D_MODEL, LANES), const(TM, TM)],
        out_specs=[row(D_MODEL), pl.BlockSpec((TM * ROW_TILE, LANES), lambda i: (i, 0)), row(LANES),
                   const(SUBLANES, LANES)],
        out_shape=[jax.ShapeDtypeStruct((t, D_MODEL), F32),
                   jax.ShapeDtypeStruct((t * ROW_TILE, LANES), F32),
                   jax.ShapeDtypeStruct((t, LANES), F32),
                   jax.ShapeDtypeStruct((SUBLANES, LANES), F32)],
        scratch_shapes=[pltpu.VMEM((1, LANES), F32)],
        compiler_params=_cparams(("arbitrary",)),
        name="router",
    )(h, yc, yd, yg, wo_stack, g, rw, tri)


def _row_copy(src, dst, sem):
    return pltpu.make_async_copy(src, dst, sem)


def _tile_at(ref, start):
    if not isinstance(start, int):
        start = pl.multiple_of(start, ROW_TILE)
    return ref.at[pl.ds(start, ROW_TILE), :]


def _dispatch_kernel(dest_ref, hn_ref, xs_in_ref, xs_ref, sem):
    del xs_in_ref

    def issue(t, carry):
        src = _tile_at(hn_ref, t * ROW_TILE)
        _row_copy(src, _tile_at(xs_ref, dest_ref[0, 0, 2 * t]), sem).start()
        _row_copy(src, _tile_at(xs_ref, dest_ref[0, 0, 2 * t + 1]), sem).start()
        return carry

    lax.fori_loop(0, TMD, issue, 0, unroll=DMA_UNROLL)

    def drain(t, carry):
        _row_copy(_tile_at(hn_ref, 0), _tile_at(xs_ref, 0), sem).wait()
        _row_copy(_tile_at(hn_ref, 0), _tile_at(xs_ref, 0), sem).wait()
        return carry

    lax.fori_loop(0, TMD, drain, 0, unroll=DMA_UNROLL)


def _dispatch(dest, hn, xs_zero):
    t = hn.shape[0] // ROW_TILE
    return pl.pallas_call(
        _dispatch_kernel,
        grid=(t // TMD,),
        in_specs=[pl.BlockSpec((1, 1, 2 * TMD), lambda i: (i, 0, 0), memory_space=pltpu.SMEM),
                  pl.BlockSpec((TMD * ROW_TILE, LANES), lambda i: (i, 0)),
                  pl.BlockSpec(memory_space=pl.ANY)],
        out_specs=pl.BlockSpec(memory_space=pl.ANY),
        out_shape=jax.ShapeDtypeStruct(xs_zero.shape, F32),
        scratch_shapes=[pltpu.SemaphoreType.DMA(())],
        input_output_aliases={2: 0},
        compiler_params=_cparams(("arbitrary",)),
        name="dispatch",
    )(dest, hn, xs_zero)


def _ple_math(h, p_blk, g_ref, wg_ref, wu_ref, fg_ref, final):
    hn = _rms(h, g_ref[...]).astype(BF16)
    gate = _sigmoid(jnp.dot(hn, wg_ref[0], preferred_element_type=F32))
    up = jnp.dot(p_blk.astype(BF16), wu_ref[0], preferred_element_type=F32)
    out = h + gate * up
    if final:
        out = _rms(out, fg_ref[...])
    return out


def _combine_kernel(dcur_ref, dnext_ref, h_ref, info_ref, p_ref, g_ref, wg_ref, wu_ref, fg_ref, ys_ref,
                    *rest, final, fuse_next):
    next_in, o_ref, next_out, (buf, sems) = _split_fused_refs(rest, fuse_next)
    i = pl.program_id(0)
    slot = i % 2

    def gather(dref, slot_):
        def issue(t, carry):
            for pick in range(2):
                _row_copy(_tile_at(ys_ref, dref[0, 0, 2 * t + pick]),
                          _tile_at(buf.at[slot_, pick], t * ROW_TILE), sems.at[slot_]).start()
            return carry

        lax.fori_loop(0, TMD, issue, 0, unroll=DMA_UNROLL)

    @pl.when(i == 0)
    def _():
        gather(dcur_ref, 0)

    @pl.when(i + 1 < pl.num_programs(0))
    def _():
        gather(dnext_ref, 1 - slot)

    def drain(t, carry):
        for pick in range(2):
            _row_copy(_tile_at(ys_ref, 0), _tile_at(buf.at[slot, pick], 0), sems.at[slot]).wait()
        return carry

    lax.fori_loop(0, TMD, drain, 0, unroll=DMA_UNROLL)
    g1 = info_ref[:, 4:5]
    g2 = info_ref[:, 5:6]
    y1 = _tile_rows_load(buf.at[slot, 0], TMD)
    y2 = _tile_rows_load(buf.at[slot, 1], TMD)
    h2 = h_ref[...] + (g1 * y1 + g2 * y2)
    out = _ple_math(h2, p_ref[0], g_ref, wg_ref, wu_ref, fg_ref, final)
    o_ref[...] = out
    if fuse_next:
        _inproj_math(out, next_in, next_out)


def _split_fused_refs(rest, fuse_next):
    if not fuse_next:
        return (), rest[0], (), rest[1:]
    n_out = N_INPROJ_IN + 1 + N_INPROJ_OUT
    return rest[:N_INPROJ_IN], rest[N_INPROJ_IN], rest[N_INPROJ_IN + 1:n_out], rest[n_out:]


def _combine(dest, h, info, ys, ple_args, next_args, seq):
    p_stack, g, wg_stack, wu_stack, layer, fg, final = ple_args
    t = h.shape[0]
    n = t // TMD
    row = lambda w_: pl.BlockSpec((TMD, w_), lambda i: (i, 0))
    const = lambda a, c: pl.BlockSpec((a, c), lambda i: (0, 0))
    in_specs = [pl.BlockSpec((1, 1, 2 * TMD), lambda i: (i, 0, 0), memory_space=pltpu.SMEM),
                pl.BlockSpec((1, 1, 2 * TMD), lambda i: (jnp.minimum(i + 1, n - 1), 0, 0),
                             memory_space=pltpu.SMEM),
                row(D_MODEL), row(LANES),
                pl.BlockSpec((1, TMD, PLE_DIM), lambda i: (layer, i, 0)),
                const(1, D_MODEL), _layer_spec(layer, D_MODEL, D_MODEL),
                _layer_spec(layer, PLE_DIM, D_MODEL), const(1, D_MODEL),
                pl.BlockSpec(memory_space=pl.ANY)]
    out_specs = [row(D_MODEL)]
    out_shape = [jax.ShapeDtypeStruct(h.shape, F32)]
    operands = [dest, dest, h, info, p_stack, g, wg_stack, wu_stack, fg, ys]
    if next_args is not None:
        nin, nout, nshape = _inproj_specs(next_args[0], TMD, t, seq)
        in_specs, out_specs, out_shape = in_specs + nin, out_specs + nout, out_shape + nshape
        operands += list(next_args[1:])
    res = pl.pallas_call(
        functools.partial(_combine_kernel, final=final, fuse_next=next_args is not None),
        grid=(n,),
        in_specs=in_specs,
        out_specs=out_specs,
        out_shape=out_shape,
        scratch_shapes=[pltpu.VMEM((2, 2, TMD * ROW_TILE, LANES), F32), pltpu.SemaphoreType.DMA((2,))],
        compiler_params=_cparams(("arbitrary",)),
        name="combine",
    )(*operands)
    return res[0], res[1:]


def _moe(h, yc, yd, yg, wo_stack, layer, g, rw, w_gu, w_down, moe_layer, tri_strict, ple_args,
         next_args, seq):
    t = h.shape[0]
    h1, hn, info, cnt = _router(h, yc, yd, yg, wo_stack, layer, g, rw, tri_strict)
    counts = cnt[0, :N_EXPERTS].astype(jnp.int32)
    padded = ((counts + MOE_BLK - 1) // MOE_BLK) * MOE_BLK
    pend = jnp.cumsum(padded)
    pstart = pend - padded
    e = info[:, 0:2].astype(jnp.int32)
    r = info[:, 2:4].astype(jnp.int32)
    onehot = e[:, :, None] == jnp.arange(N_EXPERTS, dtype=jnp.int32)[None, None, :]
    dest = r + jnp.sum(jnp.where(onehot, pstart[None, None, :], 0), axis=-1)
    dest = (dest * ROW_TILE).reshape(t // TMD, 1, 2 * TMD)
    m_rows = 2 * t + N_EXPERTS * MOE_BLK
    nb = m_rows // MOE_BLK
    blk_start = jnp.arange(nb, dtype=jnp.int32) * MOE_BLK
    block_e = jnp.minimum(jnp.sum(blk_start[:, None] >= pend[None, :], axis=-1), N_EXPERTS - 1)
    n_used = (pend[-1:] // MOE_BLK).astype(jnp.int32)

    xs = _dispatch(dest, hn, jnp.zeros((m_rows * ROW_TILE, LANES), F32))
    ys = _experts(xs, w_gu, w_down, moe_layer, block_e.astype(jnp.int32), n_used)
    return _combine(dest, h1, info, ys, ple_args, next_args, seq)


def _ple_kernel(h_ref, p_ref, g_ref, wg_ref, wu_ref, fg_ref, *rest, final, fuse_next):
    next_in, o_ref, next_out, _ = _split_fused_refs(rest, fuse_next)
    out = _ple_math(h_ref[...], p_ref[0], g_ref, wg_ref, wu_ref, fg_ref, final)
    o_ref[...] = out
    if fuse_next:
        _inproj_math(out, next_in, next_out)


def _ple(h, ple_args, next_args, seq):
    p_stack, g, wg_stack, wu_stack, layer, fg, final = ple_args
    t = h.shape[0]
    row = lambda w_: pl.BlockSpec((TM, w_), lambda i: (i, 0))
    const = lambda a, c: pl.BlockSpec((a, c), lambda i: (0, 0))
    in_specs = [row(D_MODEL), pl.BlockSpec((1, TM, PLE_DIM), lambda i: (layer, i, 0)),
                const(1, D_MODEL), _layer_spec(layer, D_MODEL, D_MODEL),
                _layer_spec(layer, PLE_DIM, D_MODEL), const(1, D_MODEL)]
    out_specs = [row(D_MODEL)]
    out_shape = [jax.ShapeDtypeStruct(h.shape, F32)]
    operands = [h, p_stack, g, wg_stack, wu_stack, fg]
    if next_args is not None:
        nin, nout, nshape = _inproj_specs(next_args[0], TM, t, seq)
        in_specs, out_specs, out_shape = in_specs + nin, out_specs + nout, out_shape + nshape
        operands += list(next_args[1:])
    res = pl.pallas_call(
        functools.partial(_ple_kernel, final=final, fuse_next=next_args is not None),
        grid=(t // TM,),
        in_specs=in_specs,
        out_specs=out_specs,
        out_shape=out_shape,
        compiler_params=_cparams(("parallel",)),
        name="ple",
    )(*operands)
    return res[0], res[1:]


def _rope_tables(seq):
    half = ROT_DIMS // 2
    pos = jnp.arange(seq, dtype=F32)
    inv_freq = ROPE_THETA ** (-jnp.arange(0, ROT_DIMS, 2, dtype=F32) / ROT_DIMS)
    ang = pos[:, None] * inv_freq[None, :]
    cos, sin = jnp.cos(ang), jnp.sin(ang)
    lane = np.arange(LANES) % DIFF_DH
    fidx = lane % half
    first = jnp.asarray(lane < half)[None, :]
    second = jnp.asarray((lane >= half) & (lane < ROT_DIMS))[None, :]
    ra = jnp.where(first | second, cos[:, fidx], 1.0)
    rb = jnp.where(second, sin[:, fidx], 0.0)
    rc = jnp.where(first, -sin[:, fidx], 0.0)
    return ra, rb, rc


def kernel(x, p, norm_mix_g, w_in, conv_w, conv_b, conv_ln_g, conv_ln_b, diff_lambda, diff_subln_g,
           gla_w_gate2, gla_b_gate, gla_norm_g, w_out, norm_ffn_g, ffn_w_gu, ffn_w_down, router_w,
           moe_w_gu, moe_w_down, ple_w_up, ple_w_gate, ple_norm_g, final_norm_g):
    bsz, seq, d = x.shape
    depth = w_in.shape[0]
    t = bsz * seq
    assert d == D_MODEL and seq % TM == 0 and seq % TQ == 0 and t % TMD == 0

    ropes = _rope_tables(seq)
    idx = np.arange(GLA_TG)
    tri_chunk = jnp.asarray(((idx[:, None] >= idx[None, :])
                             & (idx[:, None] // CHUNK == idx[None, :] // CHUNK)).astype(np.float32)).astype(BF16)
    vi = np.arange(GLA_V)
    ki = np.arange(GLA_QK)
    gmat = jnp.asarray((vi[:, None] // GLA_DV == vi[None, :] // GLA_DV).astype(np.float32)
                       / GLA_DV).astype(BF16)
    head_mask = jnp.asarray((vi[:, None] // GLA_DV == ki[None, :] // GLA_DK).astype(np.float32))
    ti = np.arange(TM)
    tri_strict = jnp.asarray((ti[:, None] > ti[None, :]).astype(np.float32)).astype(BF16)

    w_in_b = jnp.pad(w_in, ((0, 0), (0, 0), (0, D_IN_PAD - D_IN))).astype(BF16)
    w_out_b = w_out.astype(BF16)
    ffn_gu_b = ffn_w_gu.astype(BF16)[:, None]
    ffn_down_b = ffn_w_down.astype(BF16)[:, None]
    moe_gu_b = moe_w_gu.astype(BF16)
    moe_down_b = moe_w_down.astype(BF16)
    ple_gate_b = ple_w_gate.astype(BF16)
    ple_up_b = ple_w_up.astype(BF16)
    p_rows = p.reshape(depth, t, PLE_DIM)

    def inproj_args(i):
        wg2 = jnp.pad(gla_w_gate2[i], ((0, LANES - GLA_GATE_RANK), (0, 0)))
        return (i, norm_mix_g[i][None, :], w_in_b, *ropes, wg2, gla_b_gate[i][None, :])

    h = x.reshape(t, D_MODEL)
    mixed = _inproj(h, inproj_args(0), seq)
    for i in range(depth):
        lam_init = 0.8 - 0.6 * math.exp(-0.3 * i)
        u, dq, dk, dv, gq, gk, gv, gr, la = mixed
        next_args = inproj_args(i + 1) if i + 1 < depth else None
        y_conv = _conv(u, conv_w[i], conv_b[i][None, :], conv_ln_g[i][None, :],
                       conv_ln_b[i][None, :], bsz, seq)
        y_diff = _attn(dq, dk, dv, diff_lambda[i], diff_subln_g[i][None, :], bsz, seq, lam_init)
        y_gla = _gla(gq, gk, gv, gr, la, tri_chunk, gmat, head_mask,
                     jnp.tile(gla_norm_g[i], GLA_HEADS)[None, :], bsz, seq)
        g_ffn = norm_ffn_g[i][None, :]
        j = i // 2
        ple_args = (p_rows, ple_norm_g[i][None, :], ple_gate_b, ple_up_b, i, final_norm_g[None, :],
                    i == depth - 1)
        if i % 2 == 0:
            h = _ffn(h, y_conv, y_diff, y_gla, w_out_b, i, g_ffn, ffn_gu_b, ffn_down_b, j)
            h, mixed = _ple(h, ple_args, next_args, seq)
        else:
            rw = jnp.pad(router_w[j], ((0, 0), (0, LANES - N_EXPERTS)))
            h, mixed = _moe(h, y_conv, y_diff, y_gla, w_out_b, i, g_ffn, rw, moe_gu_b, moe_down_b, j,
                            tri_strict, ple_args, next_args, seq)
    return h.reshape(bsz, seq, D_MODEL)
```

```python
import functools
import math

import jax
import jax.numpy as jnp
import numpy as np
from jax import lax
from jax.experimental import pallas as pl
from jax.experimental.pallas import tpu as pltpu

F32 = jnp.float32
BF16 = jnp.bfloat16

D_MODEL = 1024
CHUNK = 64
CONV_CH = 256
CONV_WIDTH = 31
DIFF_HEADS = 4
DIFF_DV = 128
DIFF_DH = 64
GLA_HEADS = 4
GLA_DV = 64
GLA_DK = 32
GLA_GATE_RANK = 16
GLA_TAU = 16.0
ROPE_THETA = 500000.0
ROT_DIMS = 16
D_FF = 2816
N_EXPERTS = 8
D_FF_EXPERT = 3584
PLE_DIM = 256
EPS = 1e-6
DIFF_W = 512
GLA_QK = GLA_HEADS * GLA_DK
GLA_V = GLA_HEADS * GLA_DV
D_IN = 2832
D_IN_PAD = 2944

LANES = 128
SUBLANES = 8
VMEM_LIMIT = 56 * 1024 * 1024
LOG2E = math.log2(math.e)

TM = 512
TK = 1024
TQ = TK
Q_SUB = 512
Q_PIECE = 256
CONV_TS = 512
CONV_HALO = 32
CONV_SH_ROWS = CONV_TS + CONV_HALO - SUBLANES
GLA_TG = 512
FFN_FB = 1408
MOE_BLK = 512
MOE_FB = 1792
ROW_TILE = D_MODEL // LANES
TMD = 512
DMA_UNROLL = 8


def _cparams(sem):
    return pltpu.CompilerParams(dimension_semantics=sem, vmem_limit_bytes=VMEM_LIMIT)


def _rms(x, g):
    ms = jnp.mean(x * x, axis=-1, keepdims=True)
    return x * lax.rsqrt(ms + EPS) * g


def _sigmoid(x):
    return 1.0 / (1.0 + jnp.exp(-x))


def _silu(x):
    return x * _sigmoid(x)


def _split_bf16(x):
    hi = x.astype(BF16)
    return hi, (x - hi.astype(F32)).astype(BF16)


def _dot_f32(a, b):
    a_hi, a_lo = _split_bf16(a)
    b_hi, b_lo = _split_bf16(b)
    dot = functools.partial(jnp.dot, preferred_element_type=F32)
    return dot(a_hi, b_hi) + (dot(a_hi, b_lo) + dot(a_lo, b_hi))


def _rope(x, ra, rb, rc):
    outs = []
    for c in range(x.shape[1] // LANES):
        xc = x[:, c * LANES:(c + 1) * LANES]
        outs.append(xc * ra + pltpu.roll(xc, ROT_DIMS // 2, 1) * rb
                    + pltpu.roll(xc, LANES - ROT_DIMS // 2, 1) * rc)
    return jnp.concatenate(outs, axis=1)


N_INPROJ_IN = 7
N_INPROJ_OUT = 9


def _inproj_math(x, in_refs, out_refs):
    g_ref, w_ref, ra_ref, rb_ref, rc_ref, wg2_ref, bg_ref = in_refs
    u_ref, q_ref, k_ref, v_ref, gq_ref, gk_ref, gv_ref, gr_ref, la_ref = out_refs
    hn = _rms(x, g_ref[...]).astype(BF16)

    def proj(a, b):
        return jnp.dot(hn, w_ref[0, :, a:b], preferred_element_type=F32)

    ra, rb, rc = ra_ref[...], rb_ref[...], rc_ref[...]
    u_ref[...] = proj(0, 256) * _sigmoid(proj(256, 512))
    q_ref[...] = (_rope(proj(512, 1024), ra, rb, rc) * (LOG2E * DIFF_DH ** -0.5)).astype(BF16)
    k_ref[...] = _rope(proj(1024, 1536), ra, rb, rc).astype(BF16)
    v_ref[...] = proj(1536, 2048).astype(BF16)
    gq_ref[...] = proj(2048, 2176) * (GLA_DK ** -0.5)
    gk_ref[...] = proj(2176, 2304)
    gv_ref[...] = proj(2304, 2560)
    gr_ref[...] = _silu(proj(2560, 2816))
    gz = proj(2816, D_IN_PAD)
    ga = _dot_f32(gz, wg2_ref[...]) + bg_ref[...]
    la_ref[...] = (jnp.minimum(ga, 0.0) - jnp.log(1.0 + jnp.exp(-jnp.abs(ga)))) * (1.0 / GLA_TAU)


def _inproj_kernel(h_ref, *refs):
    _inproj_math(h_ref[...], refs[:N_INPROJ_IN], refs[N_INPROJ_IN:])


def _layer_spec(layer, a, b):
    return pl.BlockSpec((1, a, b), lambda i: (layer, 0, 0))


def _inproj_specs(layer, tile, t, seq):
    nseq = seq // tile
    const = lambda a, b: pl.BlockSpec((a, b), lambda i: (0, 0))
    rope_spec = pl.BlockSpec((tile, LANES), lambda i: (i % nseq, 0))
    outs = [(CONV_CH, F32), (DIFF_W, BF16), (DIFF_W, BF16), (DIFF_W, BF16),
            (GLA_QK, F32), (GLA_QK, F32), (GLA_V, F32), (GLA_V, F32), (GLA_QK, F32)]
    in_specs = [const(1, D_MODEL), _layer_spec(layer, D_MODEL, D_IN_PAD),
                rope_spec, rope_spec, rope_spec, const(LANES, LANES), const(1, LANES)]
    out_specs = [pl.BlockSpec((tile, w_), lambda i: (i, 0)) for w_, _ in outs]
    out_shape = [jax.ShapeDtypeStruct((t, w_), dt) for w_, dt in outs]
    return in_specs, out_specs, out_shape


def _inproj(h, inproj_args, seq):
    t = h.shape[0]
    in_specs, out_specs, out_shape = _inproj_specs(inproj_args[0], TM, t, seq)
    return pl.pallas_call(
        _inproj_kernel,
        grid=(t // TM,),
        in_specs=[pl.BlockSpec((TM, D_MODEL), lambda i: (i, 0))] + in_specs,
        out_specs=out_specs,
        out_shape=out_shape,
        compiler_params=_cparams(("parallel",)),
        name="inproj",
    )(h, *inproj_args[1:])


def _conv_kernel(u_ref, w_ref, b_ref, lg_ref, lb_ref, o_ref, win, shifted):
    @pl.when(pl.program_id(1) == 0)
    def _():
        win[0:CONV_HALO, :] = jnp.zeros((CONV_HALO, CONV_CH), F32)

    win[CONV_HALO:, :] = u_ref[...]
    for r in range(1, SUBLANES):
        shifted[r - 1] = win[r:r + CONV_SH_ROWS, :]
    rows = 64
    base = CONV_HALO - (CONV_WIDTH - 1)
    for r0 in range(0, CONV_TS, rows):
        acc = jnp.zeros((rows, CONV_CH), F32) + b_ref[...]
        for j in range(CONV_WIDTH):
            res = (base + j) % SUBLANES
            start = r0 + (base + j) - res
            if res == 0:
                tap = win[start:start + rows, :]
            else:
                tap = shifted[res - 1, start:start + rows, :]
            acc = acc + tap * w_ref[j:j + 1, :]
        mu = jnp.mean(acc, axis=-1, keepdims=True)
        xc = acc - mu
        var = jnp.mean(xc * xc, axis=-1, keepdims=True)
        y = xc * lax.rsqrt(var + EPS) * lg_ref[...] + lb_ref[...]
        o_ref[r0:r0 + rows, :] = _silu(y).astype(BF16)
    win[0:CONV_HALO, :] = win[CONV_TS:CONV_TS + CONV_HALO, :]


def _conv(u, w, b, lg, lb, bsz, seq):
    nt = seq // CONV_TS
    const = lambda a, c: pl.BlockSpec((a, c), lambda bi, ti: (0, 0))
    return pl.pallas_call(
        _conv_kernel,
        grid=(bsz, nt),
        in_specs=[pl.BlockSpec((CONV_TS, CONV_CH), lambda bi, ti: (bi * nt + ti, 0)),
                  const(CONV_WIDTH, CONV_CH), const(1, CONV_CH), const(1, CONV_CH), const(1, CONV_CH)],
        out_specs=pl.BlockSpec((CONV_TS, CONV_CH), lambda bi, ti: (bi * nt + ti, 0)),
        out_shape=jax.ShapeDtypeStruct(u.shape, BF16),
        scratch_shapes=[pltpu.VMEM((CONV_TS + CONV_HALO, CONV_CH), F32),
                        pltpu.VMEM((SUBLANES - 1, CONV_SH_ROWS, CONV_CH), F32)],
        compiler_params=_cparams(("arbitrary", "arbitrary")),
        name="conv",
    )(u, w, b, lg, lb)


def _attn_kernel(qt_ref, kt_ref, lam_ref, sg_ref, q_ref, k_ref, v_ref, o_ref, m_scr, acc_scr,
                 *, lam_init):
    step = pl.program_id(1)
    qi = qt_ref[step]
    kj = kt_ref[step]

    @pl.when(kj == 0)
    def _():
        m_scr[...] = jnp.full(m_scr.shape, -jnp.inf, F32)
        acc_scr[...] = jnp.zeros(acc_scr.shape, F32)

    def update(sub, masked):
        if masked:
            pieces = [(sub * Q_SUB + r, Q_PIECE, sub * Q_SUB + r + Q_PIECE) for r in range(0, Q_SUB, Q_PIECE)]
        else:
            pieces = [(sub * Q_SUB, Q_SUB, TK)]
        for r0, nr, nk in pieces:
            rows = slice(r0, r0 + nr)
            if masked:
                rq = (lax.broadcasted_iota(jnp.int32, (nr, nk), 0) + r0) // CHUNK
                ck = lax.broadcasted_iota(jnp.int32, (nr, nk), 1) // CHUNK
                allowed = ck <= rq
            for h in range(DIFF_HEADS):
                hs = slice(h * LANES, (h + 1) * LANES)
                q = q_ref[rows, hs]
                k = k_ref[0:nk, hs]
                v = v_ref[0:nk, hs]
                vext = jnp.concatenate([v, jnp.ones_like(v)], axis=1)
                lane = lax.broadcasted_iota(jnp.int32, q.shape, 1)
                for c in range(2):
                    sel = (lane < DIFF_DH) if c == 0 else (lane >= DIFF_DH)
                    qc = jnp.where(sel, q, jnp.zeros_like(q))
                    s = lax.dot_general(qc, k, (((1,), (1,)), ((), ())), preferred_element_type=F32)
                    if masked:
                        s = jnp.where(allowed, s, -jnp.inf)
                    m_old = m_scr[2 * h + c, rows]
                    m_new = jnp.maximum(m_old, jnp.max(s, axis=-1, keepdims=True))
                    alpha = jnp.exp2(m_old - m_new)
                    p = jnp.exp2(s - jnp.tile(m_new, (1, nk // LANES)))
                    pv = jnp.dot(p.astype(BF16), vext, preferred_element_type=F32)
                    acc_scr[2 * h + c, rows] = jnp.tile(alpha, (1, 2)) * acc_scr[2 * h + c, rows] + pv
                    m_scr[2 * h + c, rows] = m_new

    @pl.when(kj < qi)
    def _():
        for sub in range(TQ // Q_SUB):
            update(sub, False)

    @pl.when(kj == qi)
    def _():
        for sub in range(TQ // Q_SUB):
            update(sub, True)
        lp = lam_ref[...]
        lam = (jnp.exp(jnp.sum(lp[0:1] * lp[1:2], axis=-1, keepdims=True))
               - jnp.exp(jnp.sum(lp[2:3] * lp[3:4], axis=-1, keepdims=True)) + lam_init)
        for h in range(DIFF_HEADS):
            a0 = acc_scr[2 * h]
            a1 = acc_scr[2 * h + 1]
            o = a0[:, :DIFF_DV] / a0[:, DIFF_DV:] - lam * (a1[:, :DIFF_DV] / a1[:, DIFF_DV:])
            o_ref[:, h * LANES:(h + 1) * LANES] = (_rms(o, sg_ref[...]) * (1.0 - lam_init)).astype(BF16)


def _attn(q, k, v, lam_p, sg, bsz, seq, lam_init):
    nq = seq // TQ
    nk = seq // TK
    pairs = [(a, b) for a in range(nq) for b in range(a + 1)]
    qt = jnp.asarray(np.array([a for a, _ in pairs], np.int32))
    kt = jnp.asarray(np.array([b for _, b in pairs], np.int32))
    qspec = pl.BlockSpec((TQ, DIFF_W), lambda b, s, qt_, kt_: (b * nq + qt_[s], 0))
    kspec = pl.BlockSpec((TK, DIFF_W), lambda b, s, qt_, kt_: (b * nk + kt_[s], 0))
    const = lambda a, c: pl.BlockSpec((a, c), lambda b, s, qt_, kt_: (0, 0))
    grid_spec = pltpu.PrefetchScalarGridSpec(
        num_scalar_prefetch=2,
        grid=(bsz, len(pairs)),
        in_specs=[const(4, DIFF_DH), const(1, DIFF_DV), qspec, kspec, kspec],
        out_specs=qspec,
        scratch_shapes=[pltpu.VMEM((2 * DIFF_HEADS, TQ, LANES), F32),
                        pltpu.VMEM((2 * DIFF_HEADS, TQ, 2 * DIFF_DV), F32)],
    )
    return pl.pallas_call(
        functools.partial(_attn_kernel, lam_init=lam_init),
        grid_spec=grid_spec,
        out_shape=jax.ShapeDtypeStruct(q.shape, BF16),
        compiler_params=_cparams(("parallel", "arbitrary")),
        name="diffattn",
    )(qt, kt, lam_p, sg, q, k, v)


def _gla_kernel(q_ref, k_ref, v_ref, r_ref, la_ref, tri_ref, gmat_ref, mask_ref, ng_ref, o_ref,
                st, kv_scr, st_scr, o_scr):
    @pl.when(pl.program_id(1) == 0)
    def _():
        st[...] = jnp.zeros(st.shape, F32)

    la = la_ref[...]
    la_hi = la.astype(BF16)
    la_lo = (la - la_hi.astype(F32)).astype(BF16)
    cum = (jnp.dot(tri_ref[...], la_hi, preferred_element_type=F32)
           + jnp.dot(tri_ref[...], la_lo, preferred_element_type=F32))
    same_head = mask_ref[...] > 0.0
    nch = GLA_TG // CHUNK
    tots = []
    for c in range(nch):
        sl = slice(c * CHUNK, (c + 1) * CHUNK)
        cum_c = cum[sl]
        tot = cum_c[CHUNK - 1:CHUNK]
        tots.append(tot)
        kdec = (k_ref[sl, :] * jnp.exp(tot - cum_c)).astype(BF16)
        v_t = v_ref[sl, :].T.astype(BF16)
        kv_scr[c] = jnp.dot(v_t, kdec, preferred_element_type=F32)
    state = st[...]
    for c in range(nch):
        state = state * jnp.exp(tots[c]) + jnp.where(same_head, kv_scr[c], 0.0)
        st_scr[c] = state.astype(BF16)
    st[...] = state
    for c in range(nch):
        sl = slice(c * CHUNK, (c + 1) * CHUNK)
        o_scr[sl, :] = lax.dot_general(q_ref[sl, :].astype(BF16), st_scr[c], (((1,), (1,)), ((), ())),
                                       preferred_element_type=F32)
    o = o_scr[...]
    ms = jnp.dot((o * o).astype(BF16), gmat_ref[...], preferred_element_type=F32)
    o_ref[...] = (o * lax.rsqrt(ms + EPS) * ng_ref[...] * r_ref[...]).astype(BF16)


def _gla(gq, gk, gv, gr, la, tri, gmat, head_mask, ng, bsz, seq):
    nt = seq // GLA_TG
    row = lambda w_: pl.BlockSpec((GLA_TG, w_), lambda bi, ti: (bi * nt + ti, 0))
    const = lambda a, c: pl.BlockSpec((a, c), lambda bi, ti: (0, 0))
    return pl.pallas_call(
        _gla_kernel,
        grid=(bsz, nt),
        in_specs=[row(GLA_QK), row(GLA_QK), row(GLA_V), row(GLA_V), row(GLA_QK),
                  const(GLA_TG, GLA_TG), const(GLA_V, GLA_V), const(GLA_V, GLA_QK), const(1, GLA_V)],
        out_specs=row(GLA_V),
        out_shape=jax.ShapeDtypeStruct(gv.shape, BF16),
        scratch_shapes=[pltpu.VMEM((GLA_V, GLA_QK), F32),
                        pltpu.VMEM((GLA_TG // CHUNK, GLA_V, GLA_QK), F32),
                        pltpu.VMEM((GLA_TG // CHUNK, GLA_V, GLA_QK), BF16),
                        pltpu.VMEM((GLA_TG, GLA_V), F32)],
        compiler_params=_cparams(("arbitrary", "arbitrary")),
        name="gla",
    )(gq, gk, gv, gr, la, tri, gmat, head_mask, ng)


def _mix_residual(h_ref, yc_ref, yd_ref, yg_ref, wo_ref):
    acc = jnp.dot(yc_ref[...], wo_ref[0, 0:256, :], preferred_element_type=F32)
    acc = acc + jnp.dot(yd_ref[...], wo_ref[0, 256:768, :], preferred_element_type=F32)
    acc = acc + jnp.dot(yg_ref[...], wo_ref[0, 768:1024, :], preferred_element_type=F32)
    return h_ref[...] + acc


def _swiglu_step(xb, wg_ref, wu_ref, wd_ref, acc):
    x = xb[...]
    g = jnp.dot(x, wg_ref[0, 0], preferred_element_type=F32)
    u = jnp.dot(x, wu_ref[0, 0], preferred_element_type=F32)
    a = (_silu(g) * u).astype(BF16)
    acc[...] += jnp.dot(a, wd_ref[0, 0], preferred_element_type=F32)


def _ffn_kernel(h_ref, yc_ref, yd_ref, yg_ref, wo_ref, g_ref, wg_ref, wu_ref, wd_ref, o_ref,
                h1, xb, acc):
    j = pl.program_id(1)

    @pl.when(j == 0)
    def _():
        x = _mix_residual(h_ref, yc_ref, yd_ref, yg_ref, wo_ref)
        h1[...] = x
        xb[...] = _rms(x, g_ref[...]).astype(BF16)
        acc[...] = jnp.zeros(acc.shape, F32)

    _swiglu_step(xb, wg_ref, wu_ref, wd_ref, acc)

    @pl.when(j == pl.num_programs(1) - 1)
    def _():
        o_ref[...] = h1[...] + acc[...]


def _ffn(h, yc, yd, yg, wo_stack, layer, g, w_gu, w_down, ffn_layer):
    t = h.shape[0]
    nj = D_FF // FFN_FB
    row = lambda w_: pl.BlockSpec((TM, w_), lambda i, j: (i, 0))
    return pl.pallas_call(
        _ffn_kernel,
        grid=(t // TM, nj),
        in_specs=[row(D_MODEL), row(CONV_CH), row(DIFF_W), row(GLA_V),
                  pl.BlockSpec((1, D_MODEL, D_MODEL), lambda i, j: (layer, 0, 0)),
                  pl.BlockSpec((1, D_MODEL), lambda i, j: (0, 0)),
                  pl.BlockSpec((1, 1, D_MODEL, FFN_FB), lambda i, j: (ffn_layer, 0, 0, j)),
                  pl.BlockSpec((1, 1, D_MODEL, FFN_FB), lambda i, j: (ffn_layer, 0, 0, j + nj)),
                  pl.BlockSpec((1, 1, FFN_FB, D_MODEL), lambda i, j: (ffn_layer, 0, j, 0))],
        out_specs=row(D_MODEL),
        out_shape=jax.ShapeDtypeStruct((t, D_MODEL), F32),
        scratch_shapes=[pltpu.VMEM((TM, D_MODEL), F32), pltpu.VMEM((TM, D_MODEL), BF16),
                        pltpu.VMEM((TM, D_MODEL), F32)],
        compiler_params=_cparams(("parallel", "arbitrary")),
        name="ffn",
    )(h, yc, yd, yg, wo_stack, g, w_gu, w_gu, w_down)


def _tile_rows_load(ref, n):
    return jnp.concatenate([ref[pl.ds(s, n, stride=ROW_TILE), :] for s in range(ROW_TILE)], axis=1)


def _tile_rows_store(ref, x, n):
    for s in range(ROW_TILE):
        ref[pl.ds(s, n, stride=ROW_TILE), :] = x[:, s * LANES:(s + 1) * LANES]


def _experts_kernel(be_ref, nu_ref, x_ref, wg_ref, wu_ref, wd_ref, o_ref, xb, acc):
    i = pl.program_id(0)
    j = pl.program_id(1)
    used = i < nu_ref[0]

    @pl.when(used)
    def _():
        @pl.when(j == 0)
        def _():
            xb[...] = _tile_rows_load(x_ref, MOE_BLK).astype(BF16)
            acc[...] = jnp.zeros(acc.shape, F32)

        _swiglu_step(xb, wg_ref, wu_ref, wd_ref, acc)

    @pl.when(j == pl.num_programs(1) - 1)
    def _():
        _tile_rows_store(o_ref, jnp.where(used, acc[...], 0.0), MOE_BLK)


def _experts(xs, w_gu, w_down, layer, block_e, n_used):
    nb = xs.shape[0] // (MOE_BLK * ROW_TILE)
    nj = D_FF_EXPERT // MOE_FB

    def last(i, nu):
        return jnp.minimum(i, nu[0] - 1)

    def hid(i, j, nu):
        return jnp.where(i < nu[0], j, nj - 1)

    xspec = pl.BlockSpec((MOE_BLK * ROW_TILE, LANES), lambda i, j, be, nu: (last(i, nu), 0))
    wg = pl.BlockSpec((1, 1, D_MODEL, MOE_FB),
                      lambda i, j, be, nu: (layer, be[last(i, nu)], 0, hid(i, j, nu)))
    wu = pl.BlockSpec((1, 1, D_MODEL, MOE_FB),
                      lambda i, j, be, nu: (layer, be[last(i, nu)], 0, hid(i, j, nu) + nj))
    wd = pl.BlockSpec((1, 1, MOE_FB, D_MODEL),
                      lambda i, j, be, nu: (layer, be[last(i, nu)], hid(i, j, nu), 0))
    grid_spec = pltpu.PrefetchScalarGridSpec(
        num_scalar_prefetch=2,
        grid=(nb, nj),
        in_specs=[xspec, wg, wu, wd],
        out_specs=pl.BlockSpec((MOE_BLK * ROW_TILE, LANES), lambda i, j, be, nu: (i, 0)),
        scratch_shapes=[pltpu.VMEM((MOE_BLK, D_MODEL), BF16), pltpu.VMEM((MOE_BLK, D_MODEL), F32)],
    )
    return pl.pallas_call(
        _experts_kernel,
        grid_spec=grid_spec,
        out_shape=jax.ShapeDtypeStruct(xs.shape, F32),
        compiler_params=_cparams(("arbitrary", "arbitrary")),
        name="experts",
    )(block_e, n_used, xs, w_gu, w_gu, w_down)


def _router_kernel(h_ref, yc_ref, yd_ref, yg_ref, wo_ref, g_ref, rw_ref, tri_ref,
                   h1_ref, hn_ref, info_ref, cnt_ref, carry):
    @pl.when(pl.program_id(0) == 0)
    def _():
        carry[...] = jnp.zeros(carry.shape, F32)

    h1 = _mix_residual(h_ref, yc_ref, yd_ref, yg_ref, wo_ref)
    h1_ref[...] = h1
    hn = _rms(h1, g_ref[...])
    _tile_rows_store(hn_ref, hn, TM)
    logits = _dot_f32(hn, rw_ref[...])
    lane = lax.broadcasted_iota(jnp.int32, logits.shape, 1)
    lg = jnp.where(lane < N_EXPERTS, logits, -jnp.inf)
    v1 = jnp.max(lg, axis=-1, keepdims=True)
    i1 = jnp.min(jnp.where(lg == v1, lane, LANES), axis=-1, keepdims=True)
    lg2 = jnp.where(lane == i1, -jnp.inf, lg)
    v2 = jnp.max(lg2, axis=-1, keepdims=True)
    i2 = jnp.min(jnp.where(lg2 == v2, lane, LANES), axis=-1, keepdims=True)
    e2 = jnp.exp(v2 - v1)
    g1 = 1.0 / (1.0 + e2)
    g2 = e2 / (1.0 + e2)
    pick1 = lane == i1
    pick2 = lane == i2
    onehot = jnp.where(pick1 | pick2, 1.0, 0.0)
    before = jnp.dot(tri_ref[...], onehot.astype(BF16), preferred_element_type=F32) + carry[...]
    r1 = jnp.sum(jnp.where(pick1, before, 0.0), axis=-1, keepdims=True)
    r2 = jnp.sum(jnp.where(pick2, before, 0.0), axis=-1, keepdims=True)
    carry[...] += jnp.sum(onehot, axis=0, keepdims=True)
    info = jnp.where(lane == 0, i1.astype(F32), 0.0)
    info = jnp.where(lane == 1, i2.astype(F32), info)
    info = jnp.where(lane == 2, r1, info)
    info = jnp.where(lane == 3, r2, info)
    info = jnp.where(lane == 4, g1, info)
    info = jnp.where(lane == 5, g2, info)
    info_ref[...] = info
    cnt_ref[...] = jnp.broadcast_to(carry[...], cnt_ref.shape)


def _router(h, yc, yd, yg, wo_stack, layer, g, rw, tri):
    t = h.shape[0]
    row = lambda w_: pl.BlockSpec((TM, w_), lambda i: (i, 0))
    const = lambda a, c: pl.BlockSpec((a, c), lambda i: (0, 0))
    return pl.pallas_call(
        _router_kernel,
        grid=(t // TM,),
        in_specs=[row(D_MODEL), row(CONV_CH), row(DIFF_W), row(GLA_V),
                  _layer_spec(layer, D_MODEL, D_MODEL),
                  const(1, D_MODEL), const(D_MODEL, LANES), const(TM, TM)],
        out_specs=[row(D_MODEL), pl.BlockSpec((TM * ROW_TILE, LANES), lambda i: (i, 0)), row(LANES),
                   const(SUBLANES, LANES)],
        out_shape=[jax.ShapeDtypeStruct((t, D_MODEL), F32),
                   jax.ShapeDtypeStruct((t * ROW_TILE, LANES), F32),
                   jax.ShapeDtypeStruct((t, LANES), F32),
                   jax.ShapeDtypeStruct((SUBLANES, LANES), F32)],
        scratch_shapes=[pltpu.VMEM((1, LANES), F32)],
        compiler_params=_cparams(("arbitrary",)),
        name="router",
    )(h, yc, yd, yg, wo_stack, g, rw, tri)


def _row_copy(src, dst, sem):
    return pltpu.make_async_copy(src, dst, sem)


def _tile_at(ref, start):
    if not isinstance(start, int):
        start = pl.multiple_of(start, ROW_TILE)
    return ref.at[pl.ds(start, ROW_TILE), :]


def _dispatch_kernel(dest_ref, hn_ref, xs_in_ref, xs_ref, sem):
    del xs_in_ref

    def issue(t, carry):
        src = _tile_at(hn_ref, t * ROW_TILE)
        _row_copy(src, _tile_at(xs_ref, dest_ref[0, 0, 2 * t]), sem).start()
        _row_copy(src, _tile_at(xs_ref, dest_ref[0, 0, 2 * t + 1]), sem).start()
        return carry

    lax.fori_loop(0, TMD, issue, 0, unroll=DMA_UNROLL)

    def drain(t, carry):
        _row_copy(_tile_at(hn_ref, 0), _tile_at(xs_ref, 0), sem).wait()
        _row_copy(_tile_at(hn_ref, 0), _tile_at(xs_ref, 0), sem).wait()
        return carry

    lax.fori_loop(0, TMD, drain, 0, unroll=DMA_UNROLL)


def _dispatch(dest, hn, xs_zero):
    t = hn.shape[0] // ROW_TILE
    return pl.pallas_call(
        _dispatch_kernel,
        grid=(t // TMD,),
        in_specs=[pl.BlockSpec((1, 1, 2 * TMD), lambda i: (i, 0, 0), memory_space=pltpu.SMEM),
                  pl.BlockSpec((TMD * ROW_TILE, LANES), lambda i: (i, 0)),
                  pl.BlockSpec(memory_space=pl.ANY)],
        out_specs=pl.BlockSpec(memory_space=pl.ANY),
        out_shape=jax.ShapeDtypeStruct(xs_zero.shape, F32),
        scratch_shapes=[pltpu.SemaphoreType.DMA(())],
        input_output_aliases={2: 0},
        compiler_params=_cparams(("arbitrary",)),
        name="dispatch",
    )(dest, hn, xs_zero)


def _ple_math(h, p_blk, g_ref, wg_ref, wu_ref, fg_ref, final):
    hn = _rms(h, g_ref[...]).astype(BF16)
    gate = _sigmoid(jnp.dot(hn, wg_ref[0], preferred_element_type=F32))
    up = jnp.dot(p_blk.astype(BF16), wu_ref[0], preferred_element_type=F32)
    out = h + gate * up
    if final:
        out = _rms(out, fg_ref[...])
    return out


def _combine_kernel(dcur_ref, dnext_ref, h_ref, info_ref, p_ref, g_ref, wg_ref, wu_ref, fg_ref, ys_ref,
                    *rest, final, fuse_next):
    next_in, o_ref, next_out, (buf, sems) = _split_fused_refs(rest, fuse_next)
    i = pl.program_id(0)
    slot = i % 2

    def gather(dref, slot_):
        def issue(t, carry):
            for pick in range(2):
                _row_copy(_tile_at(ys_ref, dref[0, 0, 2 * t + pick]),
                          _tile_at(buf.at[slot_, pick], t * ROW_TILE), sems.at[slot_]).start()
            return carry

        lax.fori_loop(0, TMD, issue, 0, unroll=DMA_UNROLL)

    @pl.when(i == 0)
    def _():
        gather(dcur_ref, 0)

    @pl.when(i + 1 < pl.num_programs(0))
    def _():
        gather(dnext_ref, 1 - slot)

    def drain(t, carry):
        for pick in range(2):
            _row_copy(_tile_at(ys_ref, 0), _tile_at(buf.at[slot, pick], 0), sems.at[slot]).wait()
        return carry

    lax.fori_loop(0, TMD, drain, 0, unroll=DMA_UNROLL)
    g1 = info_ref[:, 4:5]
    g2 = info_ref[:, 5:6]
    y1 = _tile_rows_load(buf.at[slot, 0], TMD)
    y2 = _tile_rows_load(buf.at[slot, 1], TMD)
    h2 = h_ref[...] + (g1 * y1 + g2 * y2)
    out = _ple_math(h2, p_ref[0], g_ref, wg_ref, wu_ref, fg_ref, final)
    o_ref[...] = out
    if fuse_next:
        _inproj_math(out, next_in, next_out)


def _split_fused_refs(rest, fuse_next):
    if not fuse_next:
        return (), rest[0], (), rest[1:]
    n_out = N_INPROJ_IN + 1 + N_INPROJ_OUT
    return rest[:N_INPROJ_IN], rest[N_INPROJ_IN], rest[N_INPROJ_IN + 1:n_out], rest[n_out:]


def _combine(dest, h, info, ys, ple_args, next_args, seq):
    p_stack, g, wg_stack, wu_stack, layer, fg, final = ple_args
    t = h.shape[0]
    n = t // TMD
    row = lambda w_: pl.BlockSpec((TMD, w_), lambda i: (i, 0))
    const = lambda a, c: pl.BlockSpec((a, c), lambda i: (0, 0))
    in_specs = [pl.BlockSpec((1, 1, 2 * TMD), lambda i: (i, 0, 0), memory_space=pltpu.SMEM),
                pl.BlockSpec((1, 1, 2 * TMD), lambda i: (jnp.minimum(i + 1, n - 1), 0, 0),
                             memory_space=pltpu.SMEM),
                row(D_MODEL), row(LANES),
                pl.BlockSpec((1, TMD, PLE_DIM), lambda i: (layer, i, 0)),
                const(1, D_MODEL), _layer_spec(layer, D_MODEL, D_MODEL),
                _layer_spec(layer, PLE_DIM, D_MODEL), const(1, D_MODEL),
                pl.BlockSpec(memory_space=pl.ANY)]
    out_specs = [row(D_MODEL)]
    out_shape = [jax.ShapeDtypeStruct(h.shape, F32)]
    operands = [dest, dest, h, info, p_stack, g, wg_stack, wu_stack, fg, ys]
    if next_args is not None:
        nin, nout, nshape = _inproj_specs(next_args[0], TMD, t, seq)
        in_specs, out_specs, out_shape = in_specs + nin, out_specs + nout, out_shape + nshape
        operands += list(next_args[1:])
    res = pl.pallas_call(
        functools.partial(_combine_kernel, final=final, fuse_next=next_args is not None),
        grid=(n,),
        in_specs=in_specs,
        out_specs=out_specs,
        out_shape=out_shape,
        scratch_shapes=[pltpu.VMEM((2, 2, TMD * ROW_TILE, LANES), F32), pltpu.SemaphoreType.DMA((2,))],
        compiler_params=_cparams(("arbitrary",)),
        name="combine",
    )(*operands)
    return res[0], res[1:]


def _moe(h, yc, yd, yg, wo_stack, layer, g, rw, w_gu, w_down, moe_layer, tri_strict, ple_args,
         next_args, seq):
    t = h.shape[0]
    h1, hn, info, cnt = _router(h, yc, yd, yg, wo_stack, layer, g, rw, tri_strict)
    counts = cnt[0, :N_EXPERTS].astype(jnp.int32)
    padded = ((counts + MOE_BLK - 1) // MOE_BLK) * MOE_BLK
    pend = jnp.cumsum(padded)
    pstart = pend - padded
    e = info[:, 0:2].astype(jnp.int32)
    r = info[:, 2:4].astype(jnp.int32)
    onehot = e[:, :, None] == jnp.arange(N_EXPERTS, dtype=jnp.int32)[None, None, :]
    dest = r + jnp.sum(jnp.where(onehot, pstart[None, None, :], 0), axis=-1)
    dest = (dest * ROW_TILE).reshape(t // TMD, 1, 2 * TMD)
    m_rows = 2 * t + N_EXPERTS * MOE_BLK
    nb = m_rows // MOE_BLK
    blk_start = jnp.arange(nb, dtype=jnp.int32) * MOE_BLK
    block_e = jnp.minimum(jnp.sum(blk_start[:, None] >= pend[None, :], axis=-1), N_EXPERTS - 1)
    n_used = (pend[-1:] // MOE_BLK).astype(jnp.int32)

    xs = _dispatch(dest, hn, jnp.zeros((m_rows * ROW_TILE, LANES), F32))
    ys = _experts(xs, w_gu, w_down, moe_layer, block_e.astype(jnp.int32), n_used)
    return _combine(dest, h1, info, ys, ple_args, next_args, seq)


def _ple_kernel(h_ref, p_ref, g_ref, wg_ref, wu_ref, fg_ref, *rest, final, fuse_next):
    next_in, o_ref, next_out, _ = _split_fused_refs(rest, fuse_next)
    out = _ple_math(h_ref[...], p_ref[0], g_ref, wg_ref, wu_ref, fg_ref, final)
    o_ref[...] = out
    if fuse_next:
        _inproj_math(out, next_in, next_out)


def _ple(h, ple_args, next_args, seq):
    p_stack, g, wg_stack, wu_stack, layer, fg, final = ple_args
    t = h.shape[0]
    row = lambda w_: pl.BlockSpec((TM, w_), lambda i: (i, 0))
    const = lambda a, c: pl.BlockSpec((a, c), lambda i: (0, 0))
    in_specs = [row(D_MODEL), pl.BlockSpec((1, TM, PLE_DIM), lambda i: (layer, i, 0)),
                const(1, D_MODEL), _layer_spec(layer, D_MODEL, D_MODEL),
                _layer_spec(layer, PLE_DIM, D_MODEL), const(1, D_MODEL)]
    out_specs = [row(D_MODEL)]
    out_shape = [jax.ShapeDtypeStruct(h.shape, F32)]
    operands = [h, p_stack, g, wg_stack, wu_stack, fg]
    if next_args is not None:
        nin, nout, nshape = _inproj_specs(next_args[0], TM, t, seq)
        in_specs, out_specs, out_shape = in_specs + nin, out_specs + nout, out_shape + nshape
        operands += list(next_args[1:])
    res = pl.pallas_call(
        functools.partial(_ple_kernel, final=final, fuse_next=next_args is not None),
        grid=(t // TM,),
        in_specs=in_specs,
        out_specs=out_specs,
        out_shape=out_shape,
        compiler_params=_cparams(("parallel",)),
        name="ple",
    )(*operands)
    return res[0], res[1:]


def _rope_tables(seq):
    half = ROT_DIMS // 2
    pos = jnp.arange(seq, dtype=F32)
    inv_freq = ROPE_THETA ** (-jnp.arange(0, ROT_DIMS, 2, dtype=F32) / ROT_DIMS)
    ang = pos[:, None] * inv_freq[None, :]
    cos, sin = jnp.cos(ang), jnp.sin(ang)
    lane = np.arange(LANES) % DIFF_DH
    fidx = lane % half
    first = jnp.asarray(lane < half)[None, :]
    second = jnp.asarray((lane >= half) & (lane < ROT_DIMS))[None, :]
    ra = jnp.where(first | second, cos[:, fidx], 1.0)
    rb = jnp.where(second, sin[:, fidx], 0.0)
    rc = jnp.where(first, -sin[:, fidx], 0.0)
    return ra, rb, rc


def kernel(x, p, norm_mix_g, w_in, conv_w, conv_b, conv_ln_g, conv_ln_b, diff_lambda, diff_subln_g,
           gla_w_gate2, gla_b_gate, gla_norm_g, w_out, norm_ffn_g, ffn_w_gu, ffn_w_down, router_w,
           moe_w_gu, moe_w_down, ple_w_up, ple_w_gate, ple_norm_g, final_norm_g):
    bsz, seq, d = x.shape
    depth = w_in.shape[0]
    t = bsz * seq
    assert d == D_MODEL and seq % TM == 0 and seq % TQ == 0 and t % TMD == 0

    ropes = _rope_tables(seq)
    idx = np.arange(GLA_TG)
    tri_chunk = jnp.asarray(((idx[:, None] >= idx[None, :])
                             & (idx[:, None] // CHUNK == idx[None, :] // CHUNK)).astype(np.float32)).astype(BF16)
    vi = np.arange(GLA_V)
    ki = np.arange(GLA_QK)
    gmat = jnp.asarray((vi[:, None] // GLA_DV == vi[None, :] // GLA_DV).astype(np.float32)
                       / GLA_DV).astype(BF16)
    head_mask = jnp.asarray((vi[:, None] // GLA_DV == ki[None, :] // GLA_DK).astype(np.float32))
    ti = np.arange(TM)
    tri_strict = jnp.asarray((ti[:, None] > ti[None, :]).astype(np.float32)).astype(BF16)

    w_in_b = jnp.pad(w_in, ((0, 0), (0, 0), (0, D_IN_PAD - D_IN))).astype(BF16)
    w_out_b = w_out.astype(BF16)
    ffn_gu_b = ffn_w_gu.astype(BF16)[:, None]
    ffn_down_b = ffn_w_down.astype(BF16)[:, None]
    moe_gu_b = moe_w_gu.astype(BF16)
    moe_down_b = moe_w_down.astype(BF16)
    ple_gate_b = ple_w_gate.astype(BF16)
    ple_up_b = ple_w_up.astype(BF16)
    p_rows = p.reshape(depth, t, PLE_DIM)

    def inproj_args(i):
        wg2 = jnp.pad(gla_w_gate2[i], ((0, LANES - GLA_GATE_RANK), (0, 0)))
        return (i, norm_mix_g[i][None, :], w_in_b, *ropes, wg2, gla_b_gate[i][None, :])

    h = x.reshape(t, D_MODEL)
    mixed = _inproj(h, inproj_args(0), seq)
    for i in range(depth):
        lam_init = 0.8 - 0.6 * math.exp(-0.3 * i)
        u, dq, dk, dv, gq, gk, gv, gr, la = mixed
        next_args = inproj_args(i + 1) if i + 1 < depth else None
        y_conv = _conv(u, conv_w[i], conv_b[i][None, :], conv_ln_g[i][None, :],
                       conv_ln_b[i][None, :], bsz, seq)
        y_diff = _attn(dq, dk, dv, diff_lambda[i], diff_subln_g[i][None, :], bsz, seq, lam_init)
        y_gla = _gla(gq, gk, gv, gr, la, tri_chunk, gmat, head_mask,
                     jnp.tile(gla_norm_g[i], GLA_HEADS)[None, :], bsz, seq)
        g_ffn = norm_ffn_g[i][None, :]
        j = i // 2
        ple_args = (p_rows, ple_norm_g[i][None, :], ple_gate_b, ple_up_b, i, final_norm_g[None, :],
                    i == depth - 1)
        if i % 2 == 0:
            h = _ffn(h, y_conv, y_diff, y_gla, w_out_b, i, g_ffn, ffn_gu_b, ffn_down_b, j)
            h, mixed = _ple(h, ple_args, next_args, seq)
        else:
            rw = jnp.pad(router_w[j], ((0, 0), (0, LANES - N_EXPERTS)))
            h, mixed = _moe(h, y_conv, y_diff, y_gla, w_out_b, i, g_ffn, rw, moe_gu_b, moe_down_b, j,
                            tri_strict, ple_args, next_args, seq)
    return h.reshape(bsz, seq, D_MODEL)
```

```python
import functools
import math

import jax
import jax.numpy as jnp
import numpy as np
from jax import lax
from jax.experimental import pallas as pl
from jax.experimental.pallas import tpu as pltpu

F32 = jnp.float32
BF16 = jnp.bfloat16

D_MODEL = 1024
CHUNK = 64
CONV_CH = 256
CONV_WIDTH = 31
DIFF_HEADS = 4
DIFF_DV = 128
DIFF_DH = 64
GLA_HEADS = 4
GLA_DV = 64
GLA_DK = 32
GLA_GATE_RANK = 16
GLA_TAU = 16.0
ROPE_THETA = 500000.0
ROT_DIMS = 16
D_FF = 2816
N_EXPERTS = 8
D_FF_EXPERT = 3584
PLE_DIM = 256
EPS = 1e-6
DIFF_W = 512
GLA_QK = GLA_HEADS * GLA_DK
GLA_V = GLA_HEADS * GLA_DV
D_IN = 2832
D_IN_PAD = 2944

LANES = 128
SUBLANES = 8
VMEM_LIMIT = 56 * 1024 * 1024
LOG2E = math.log2(math.e)

TM = 512
TK = 1024
TQ = TK
Q_SUB = 512
Q_PIECE = 256
CONV_TS = 512
CONV_HALO = 32
CONV_SH_ROWS = CONV_TS + CONV_HALO - SUBLANES
GLA_TG = 1024
GLA_TRI = 512
FFN_FB = 1408
MOE_BLK = 512
MOE_FB = 1792
ROW_TILE = D_MODEL // LANES
ROUTER_TM = 1024
TMD = 512
TMC = 256
DMA_UNROLL = 8


def _cparams(sem):
    return pltpu.CompilerParams(dimension_semantics=sem, vmem_limit_bytes=VMEM_LIMIT)


def _rms(x, g):
    ms = jnp.mean(x * x, axis=-1, keepdims=True)
    return x * lax.rsqrt(ms + EPS) * g


def _sigmoid(x):
    return 1.0 / (1.0 + jnp.exp(-x))


def _silu(x):
    return x * _sigmoid(x)


def _split_bf16(x):
    hi = x.astype(BF16)
    return hi, (x - hi.astype(F32)).astype(BF16)


def _dot_f32(a, b):
    a_hi, a_lo = _split_bf16(a)
    b_hi, b_lo = _split_bf16(b)
    dot = functools.partial(jnp.dot, preferred_element_type=F32)
    return dot(a_hi, b_hi) + (dot(a_hi, b_lo) + dot(a_lo, b_hi))


def _rope(x, ra, rb, rc):
    outs = []
    for c in range(x.shape[1] // LANES):
        xc = x[:, c * LANES:(c + 1) * LANES]
        outs.append(xc * ra + pltpu.roll(xc, ROT_DIMS // 2, 1) * rb
                    + pltpu.roll(xc, LANES - ROT_DIMS // 2, 1) * rc)
    return jnp.concatenate(outs, axis=1)


N_INPROJ_IN = 7
N_INPROJ_OUT = 9


def _inproj_math(x, in_refs, out_refs):
    g_ref, w_ref, ra_ref, rb_ref, rc_ref, wg2_ref, bg_ref = in_refs
    u_ref, q_ref, k_ref, v_ref, gq_ref, gk_ref, gv_ref, gr_ref, la_ref = out_refs
    hn = _rms(x, g_ref[...]).astype(BF16)

    def proj(a, b):
        return jnp.dot(hn, w_ref[0, :, a:b], preferred_element_type=F32)

    ra, rb, rc = ra_ref[...], rb_ref[...], rc_ref[...]
    u_ref[...] = proj(0, 256) * _sigmoid(proj(256, 512))
    q_ref[...] = (_rope(proj(512, 1024), ra, rb, rc) * (LOG2E * DIFF_DH ** -0.5)).astype(BF16)
    k_ref[...] = _rope(proj(1024, 1536), ra, rb, rc).astype(BF16)
    v_ref[...] = proj(1536, 2048).astype(BF16)
    gq_ref[...] = proj(2048, 2176) * (GLA_DK ** -0.5)
    gk_ref[...] = proj(2176, 2304)
    gv_ref[...] = proj(2304, 2560)
    gr_ref[...] = _silu(proj(2560, 2816))
    gz = proj(2816, D_IN_PAD)
    ga = _dot_f32(gz, wg2_ref[...]) + bg_ref[...]
    la_ref[...] = (jnp.minimum(ga, 0.0) - jnp.log(1.0 + jnp.exp(-jnp.abs(ga)))) * (1.0 / GLA_TAU)


def _inproj_kernel(h_ref, *refs):
    _inproj_math(h_ref[...], refs[:N_INPROJ_IN], refs[N_INPROJ_IN:])


def _layer_spec(layer, a, b):
    return pl.BlockSpec((1, a, b), lambda i: (layer, 0, 0))


def _inproj_specs(layer, tile, t, seq):
    nseq = seq // tile
    const = lambda a, b: pl.BlockSpec((a, b), lambda i: (0, 0))
    rope_spec = pl.BlockSpec((tile, LANES), lambda i: (i % nseq, 0))
    outs = [(CONV_CH, F32), (DIFF_W, BF16), (DIFF_W, BF16), (DIFF_W, BF16),
            (GLA_QK, F32), (GLA_QK, F32), (GLA_V, F32), (GLA_V, F32), (GLA_QK, F32)]
    in_specs = [const(1, D_MODEL), _layer_spec(layer, D_MODEL, D_IN_PAD),
                rope_spec, rope_spec, rope_spec, const(LANES, LANES), const(1, LANES)]
    out_specs = [pl.BlockSpec((tile, w_), lambda i: (i, 0)) for w_, _ in outs]
    out_shape = [jax.ShapeDtypeStruct((t, w_), dt) for w_, dt in outs]
    return in_specs, out_specs, out_shape


def _inproj(h, inproj_args, seq):
    t = h.shape[0]
    in_specs, out_specs, out_shape = _inproj_specs(inproj_args[0], TM, t, seq)
    return pl.pallas_call(
        _inproj_kernel,
        grid=(t // TM,),
        in_specs=[pl.BlockSpec((TM, D_MODEL), lambda i: (i, 0))] + in_specs,
        out_specs=out_specs,
        out_shape=out_shape,
        compiler_params=_cparams(("parallel",)),
        name="inproj",
    )(h, *inproj_args[1:])


def _conv_kernel(u_ref, w_ref, b_ref, lg_ref, lb_ref, o_ref, win, shifted):
    @pl.when(pl.program_id(1) == 0)
    def _():
        win[0:CONV_HALO, :] = jnp.zeros((CONV_HALO, CONV_CH), F32)

    win[CONV_HALO:, :] = u_ref[...]
    for r in range(1, SUBLANES):
        shifted[r - 1] = win[r:r + CONV_SH_ROWS, :]
    rows = 64
    base = CONV_HALO - (CONV_WIDTH - 1)
    for r0 in range(0, CONV_TS, rows):
        acc = jnp.zeros((rows, CONV_CH), F32) + b_ref[...]
        for j in range(CONV_WIDTH):
            res = (base + j) % SUBLANES
            start = r0 + (base + j) - res
            if res == 0:
                tap = win[start:start + rows, :]
            else:
                tap = shifted[res - 1, start:start + rows, :]
            acc = acc + tap * w_ref[j:j + 1, :]
        mu = jnp.mean(acc, axis=-1, keepdims=True)
        xc = acc - mu
        var = jnp.mean(xc * xc, axis=-1, keepdims=True)
        y = xc * lax.rsqrt(var + EPS) * lg_ref[...] + lb_ref[...]
        o_ref[r0:r0 + rows, :] = _silu(y).astype(BF16)
    win[0:CONV_HALO, :] = win[CONV_TS:CONV_TS + CONV_HALO, :]


def _conv(u, w, b, lg, lb, bsz, seq):
    nt = seq // CONV_TS
    const = lambda a, c: pl.BlockSpec((a, c), lambda bi, ti: (0, 0))
    return pl.pallas_call(
        _conv_kernel,
        grid=(bsz, nt),
        in_specs=[pl.BlockSpec((CONV_TS, CONV_CH), lambda bi, ti: (bi * nt + ti, 0)),
                  const(CONV_WIDTH, CONV_CH), const(1, CONV_CH), const(1, CONV_CH), const(1, CONV_CH)],
        out_specs=pl.BlockSpec((CONV_TS, CONV_CH), lambda bi, ti: (bi * nt + ti, 0)),
        out_shape=jax.ShapeDtypeStruct(u.shape, BF16),
        scratch_shapes=[pltpu.VMEM((CONV_TS + CONV_HALO, CONV_CH), F32),
                        pltpu.VMEM((SUBLANES - 1, CONV_SH_ROWS, CONV_CH), F32)],
        compiler_params=_cparams(("arbitrary", "arbitrary")),
        name="conv",
    )(u, w, b, lg, lb)


def _attn_kernel(qt_ref, kt_ref, lam_ref, sg_ref, q_ref, k_ref, v_ref, o_ref, m_scr, acc_scr,
                 *, lam_init):
    step = pl.program_id(1)
    qi = qt_ref[step]
    kj = kt_ref[step]

    @pl.when(kj == 0)
    def _():
        m_scr[...] = jnp.full(m_scr.shape, -jnp.inf, F32)
        acc_scr[...] = jnp.zeros(acc_scr.shape, F32)

    def update(sub, masked):
        if masked:
            pieces = [(sub * Q_SUB + r, Q_PIECE, sub * Q_SUB + r + Q_PIECE) for r in range(0, Q_SUB, Q_PIECE)]
        else:
            pieces = [(sub * Q_SUB, Q_SUB, TK)]
        for r0, nr, nk in pieces:
            rows = slice(r0, r0 + nr)
            if masked:
                rq = (lax.broadcasted_iota(jnp.int32, (nr, nk), 0) + r0) // CHUNK
                ck = lax.broadcasted_iota(jnp.int32, (nr, nk), 1) // CHUNK
                allowed = ck <= rq
            for h in range(DIFF_HEADS):
                hs = slice(h * LANES, (h + 1) * LANES)
                q = q_ref[rows, hs]
                k = k_ref[0:nk, hs]
                v = v_ref[0:nk, hs]
                vext = jnp.concatenate([v, jnp.ones_like(v)], axis=1)
                lane = lax.broadcasted_iota(jnp.int32, q.shape, 1)
                for c in range(2):
                    sel = (lane < DIFF_DH) if c == 0 else (lane >= DIFF_DH)
                    qc = jnp.where(sel, q, jnp.zeros_like(q))
                    s = lax.dot_general(qc, k, (((1,), (1,)), ((), ())), preferred_element_type=F32)
                    if masked:
                        s = jnp.where(allowed, s, -jnp.inf)
                    m_old = m_scr[2 * h + c, rows]
                    m_new = jnp.maximum(m_old, jnp.max(s, axis=-1, keepdims=True))
                    alpha = jnp.exp2(m_old - m_new)
                    p = jnp.exp2(s - jnp.tile(m_new, (1, nk // LANES)))
                    pv = jnp.dot(p.astype(BF16), vext, preferred_element_type=F32)
                    acc_scr[2 * h + c, rows] = jnp.tile(alpha, (1, 2)) * acc_scr[2 * h + c, rows] + pv
                    m_scr[2 * h + c, rows] = m_new

    @pl.when(kj < qi)
    def _():
        for sub in range(TQ // Q_SUB):
            update(sub, False)

    @pl.when(kj == qi)
    def _():
        for sub in range(TQ // Q_SUB):
            update(sub, True)
        lp = lam_ref[...]
        lam = (jnp.exp(jnp.sum(lp[0:1] * lp[1:2], axis=-1, keepdims=True))
               - jnp.exp(jnp.sum(lp[2:3] * lp[3:4], axis=-1, keepdims=True)) + lam_init)
        for h in range(DIFF_HEADS):
            a0 = acc_scr[2 * h]
            a1 = acc_scr[2 * h + 1]
            o = a0[:, :DIFF_DV] / a0[:, DIFF_DV:] - lam * (a1[:, :DIFF_DV] / a1[:, DIFF_DV:])
            o_ref[:, h * LANES:(h + 1) * LANES] = (_rms(o, sg_ref[...]) * (1.0 - lam_init)).astype(BF16)


def _attn(q, k, v, lam_p, sg, bsz, seq, lam_init):
    nq = seq // TQ
    nk = seq // TK
    pairs = [(a, b) for a in range(nq) for b in range(a + 1)]
    qt = jnp.asarray(np.array([a for a, _ in pairs], np.int32))
    kt = jnp.asarray(np.array([b for _, b in pairs], np.int32))
    qspec = pl.BlockSpec((TQ, DIFF_W), lambda b, s, qt_, kt_: (b * nq + qt_[s], 0))
    kspec = pl.BlockSpec((TK, DIFF_W), lambda b, s, qt_, kt_: (b * nk + kt_[s], 0))
    const = lambda a, c: pl.BlockSpec((a, c), lambda b, s, qt_, kt_: (0, 0))
    grid_spec = pltpu.PrefetchScalarGridSpec(
        num_scalar_prefetch=2,
        grid=(bsz, len(pairs)),
        in_specs=[const(4, DIFF_DH), const(1, DIFF_DV), qspec, kspec, kspec],
        out_specs=qspec,
        scratch_shapes=[pltpu.VMEM((2 * DIFF_HEADS, TQ, LANES), F32),
                        pltpu.VMEM((2 * DIFF_HEADS, TQ, 2 * DIFF_DV), F32)],
    )
    return pl.pallas_call(
        functools.partial(_attn_kernel, lam_init=lam_init),
        grid_spec=grid_spec,
        out_shape=jax.ShapeDtypeStruct(q.shape, BF16),
        compiler_params=_cparams(("parallel", "arbitrary")),
        name="diffattn",
    )(qt, kt, lam_p, sg, q, k, v)


def _gla_kernel(q_ref, k_ref, v_ref, r_ref, la_ref, tri_ref, gmat_ref, mask_ref, ng_ref, o_ref,
                st, kv_scr, st_scr, o_scr):
    @pl.when(pl.program_id(1) == 0)
    def _():
        st[...] = jnp.zeros(st.shape, F32)

    cums = []
    for r0 in range(0, GLA_TG, GLA_TRI):
        la_hi, la_lo = _split_bf16(la_ref[r0:r0 + GLA_TRI, :])
        cums.append(jnp.dot(tri_ref[...], la_hi, preferred_element_type=F32)
                    + jnp.dot(tri_ref[...], la_lo, preferred_element_type=F32))
    cum = jnp.concatenate(cums, axis=0)
    same_head = mask_ref[...] > 0.0
    nch = GLA_TG // CHUNK
    tots = []
    for c in range(nch):
        sl = slice(c * CHUNK, (c + 1) * CHUNK)
        cum_c = cum[sl]
        tot = cum_c[CHUNK - 1:CHUNK]
        tots.append(tot)
        kdec = (k_ref[sl, :] * jnp.exp(tot - cum_c)).astype(BF16)
        v_t = v_ref[sl, :].T.astype(BF16)
        kv_scr[c] = jnp.dot(v_t, kdec, preferred_element_type=F32)
    state = st[...]
    for c in range(nch):
        state = state * jnp.exp(tots[c]) + jnp.where(same_head, kv_scr[c], 0.0)
        st_scr[c] = state.astype(BF16)
    st[...] = state
    for c in range(nch):
        sl = slice(c * CHUNK, (c + 1) * CHUNK)
        o_scr[sl, :] = lax.dot_general(q_ref[sl, :].astype(BF16), st_scr[c], (((1,), (1,)), ((), ())),
                                       preferred_element_type=F32)
    o = o_scr[...]
    ms = jnp.dot((o * o).astype(BF16), gmat_ref[...], preferred_element_type=F32)
    o_ref[...] = (o * lax.rsqrt(ms + EPS) * ng_ref[...] * r_ref[...]).astype(BF16)


def _gla(gq, gk, gv, gr, la, tri, gmat, head_mask, ng, bsz, seq):
    nt = seq // GLA_TG
    row = lambda w_: pl.BlockSpec((GLA_TG, w_), lambda bi, ti: (bi * nt + ti, 0))
    const = lambda a, c: pl.BlockSpec((a, c), lambda bi, ti: (0, 0))
    return pl.pallas_call(
        _gla_kernel,
        grid=(bsz, nt),
        in_specs=[row(GLA_QK), row(GLA_QK), row(GLA_V), row(GLA_V), row(GLA_QK),
                  const(GLA_TRI, GLA_TRI), const(GLA_V, GLA_V), const(GLA_V, GLA_QK), const(1, GLA_V)],
        out_specs=row(GLA_V),
        out_shape=jax.ShapeDtypeStruct(gv.shape, BF16),
        scratch_shapes=[pltpu.VMEM((GLA_V, GLA_QK), F32),
                        pltpu.VMEM((GLA_TG // CHUNK, GLA_V, GLA_QK), F32),
                        pltpu.VMEM((GLA_TG // CHUNK, GLA_V, GLA_QK), BF16),
                        pltpu.VMEM((GLA_TG, GLA_V), F32)],
        compiler_params=_cparams(("arbitrary", "arbitrary")),
        name="gla",
    )(gq, gk, gv, gr, la, tri, gmat, head_mask, ng)


def _mix_residual(h_ref, yc_ref, yd_ref, yg_ref, wo_ref):
    acc = jnp.dot(yc_ref[...], wo_ref[0, 0:256, :], preferred_element_type=F32)
    acc = acc + jnp.dot(yd_ref[...], wo_ref[0, 256:768, :], preferred_element_type=F32)
    acc = acc + jnp.dot(yg_ref[...], wo_ref[0, 768:1024, :], preferred_element_type=F32)
    return h_ref[...] + acc


def _swiglu_step(xb, wg_ref, wu_ref, wd_ref, acc):
    x = xb[...]
    g = jnp.dot(x, wg_ref[0, 0], preferred_element_type=F32)
    u = jnp.dot(x, wu_ref[0, 0], preferred_element_type=F32)
    a = (_silu(g) * u).astype(BF16)
    acc[...] += jnp.dot(a, wd_ref[0, 0], preferred_element_type=F32)


def _ffn_kernel(h_ref, yc_ref, yd_ref, yg_ref, wo_ref, g_ref, wg_ref, wu_ref, wd_ref, o_ref,
                h1, xb, acc):
    j = pl.program_id(1)

    @pl.when(j == 0)
    def _():
        x = _mix_residual(h_ref, yc_ref, yd_ref, yg_ref, wo_ref)
        h1[...] = x
        xb[...] = _rms(x, g_ref[...]).astype(BF16)
        acc[...] = jnp.zeros(acc.shape, F32)

    _swiglu_step(xb, wg_ref, wu_ref, wd_ref, acc)

    @pl.when(j == pl.num_programs(1) - 1)
    def _():
        o_ref[...] = h1[...] + acc[...]


def _ffn(h, yc, yd, yg, wo_stack, layer, g, w_gu, w_down, ffn_layer):
    t = h.shape[0]
    nj = D_FF // FFN_FB
    row = lambda w_: pl.BlockSpec((TM, w_), lambda i, j: (i, 0))
    return pl.pallas_call(
        _ffn_kernel,
        grid=(t // TM, nj),
        in_specs=[row(D_MODEL), row(CONV_CH), row(DIFF_W), row(GLA_V),
                  pl.BlockSpec((1, D_MODEL, D_MODEL), lambda i, j: (layer, 0, 0)),
                  pl.BlockSpec((1, D_MODEL), lambda i, j: (0, 0)),
                  pl.BlockSpec((1, 1, D_MODEL, FFN_FB), lambda i, j: (ffn_layer, 0, 0, j)),
                  pl.BlockSpec((1, 1, D_MODEL, FFN_FB), lambda i, j: (ffn_layer, 0, 0, j + nj)),
                  pl.BlockSpec((1, 1, FFN_FB, D_MODEL), lambda i, j: (ffn_layer, 0, j, 0))],
        out_specs=row(D_MODEL),
        out_shape=jax.ShapeDtypeStruct((t, D_MODEL), F32),
        scratch_shapes=[pltpu.VMEM((TM, D_MODEL), F32), pltpu.VMEM((TM, D_MODEL), BF16),
                        pltpu.VMEM((TM, D_MODEL), F32)],
        compiler_params=_cparams(("parallel", "arbitrary")),
        name="ffn",
    )(h, yc, yd, yg, wo_stack, g, w_gu, w_gu, w_down)


def _tile_rows_load(ref, n):
    return jnp.concatenate([ref[pl.ds(s, n, stride=ROW_TILE), :] for s in range(ROW_TILE)], axis=1)


def _tile_rows_store(ref, x, n):
    for s in range(ROW_TILE):
        ref[pl.ds(s, n, stride=ROW_TILE), :] = x[:, s * LANES:(s + 1) * LANES]


def _experts_kernel(be_ref, nu_ref, x_ref, wg_ref, wu_ref, wd_ref, o_ref, xb, acc):
    i = pl.program_id(0)
    j = pl.program_id(1)
    used = i < nu_ref[0]

    @pl.when(used)
    def _():
        @pl.when(j == 0)
        def _():
            xb[...] = _tile_rows_load(x_ref, MOE_BLK).astype(BF16)
            acc[...] = jnp.zeros(acc.shape, F32)

        _swiglu_step(xb, wg_ref, wu_ref, wd_ref, acc)

    @pl.when(j == pl.num_programs(1) - 1)
    def _():
        _tile_rows_store(o_ref, jnp.where(used, acc[...], 0.0), MOE_BLK)


def _experts(xs, w_gu, w_down, layer, block_e, n_used):
    nb = xs.shape[0] // (MOE_BLK * ROW_TILE)
    nj = D_FF_EXPERT // MOE_FB

    def last(i, nu):
        return jnp.minimum(i, nu[0] - 1)

    def hid(i, j, nu):
        return jnp.where(i < nu[0], j, nj - 1)

    xspec = pl.BlockSpec((MOE_BLK * ROW_TILE, LANES), lambda i, j, be, nu: (last(i, nu), 0))
    wg = pl.BlockSpec((1, 1, D_MODEL, MOE_FB),
                      lambda i, j, be, nu: (layer, be[last(i, nu)], 0, hid(i, j, nu)))
    wu = pl.BlockSpec((1, 1, D_MODEL, MOE_FB),
                      lambda i, j, be, nu: (layer, be[last(i, nu)], 0, hid(i, j, nu) + nj))
    wd = pl.BlockSpec((1, 1, MOE_FB, D_MODEL),
                      lambda i, j, be, nu: (layer, be[last(i, nu)], hid(i, j, nu), 0))
    grid_spec = pltpu.PrefetchScalarGridSpec(
        num_scalar_prefetch=2,
        grid=(nb, nj),
        in_specs=[xspec, wg, wu, wd],
        out_specs=pl.BlockSpec((MOE_BLK * ROW_TILE, LANES), lambda i, j, be, nu: (i, 0)),
        scratch_shapes=[pltpu.VMEM((MOE_BLK, D_MODEL), BF16), pltpu.VMEM((MOE_BLK, D_MODEL), F32)],
    )
    return pl.pallas_call(
        _experts_kernel,
        grid_spec=grid_spec,
        out_shape=jax.ShapeDtypeStruct(xs.shape, F32),
        compiler_params=_cparams(("arbitrary", "arbitrary")),
        name="experts",
    )(block_e, n_used, xs, w_gu, w_gu, w_down)


def _router_kernel(h_ref, yc_ref, yd_ref, yg_ref, wo_ref, g_ref, rw_ref, tri_ref,
                   h1_ref, hn_ref, info_ref, cnt_ref, carry):
    @pl.when(pl.program_id(0) == 0)
    def _():
        carry[...] = jnp.zeros(carry.shape, F32)

    h1 = _mix_residual(h_ref, yc_ref, yd_ref, yg_ref, wo_ref)
    h1_ref[...] = h1
    hn = _rms(h1, g_ref[...])
    _tile_rows_store(hn_ref, hn, ROUTER_TM)
    logits = _dot_f32(hn, rw_ref[...])
    lane = lax.broadcasted_iota(jnp.int32, logits.shape, 1)
    lg = jnp.where(lane < N_EXPERTS, logits, -jnp.inf)
    v1 = jnp.max(lg, axis=-1, keepdims=True)
    i1 = jnp.min(jnp.where(lg == v1, lane, LANES), axis=-1, keepdims=True)
    lg2 = jnp.where(lane == i1, -jnp.inf, lg)
    v2 = jnp.max(lg2, axis=-1, keepdims=True)
    i2 = jnp.min(jnp.where(lg2 == v2, lane, LANES), axis=-1, keepdims=True)
    e2 = jnp.exp(v2 - v1)
    g1 = 1.0 / (1.0 + e2)
    g2 = e2 / (1.0 + e2)
    pick1 = lane == i1
    pick2 = lane == i2
    onehot = jnp.where(pick1 | pick2, 1.0, 0.0)
    before = jnp.dot(tri_ref[...], onehot.astype(BF16), preferred_element_type=F32) + carry[...]
    r1 = jnp.sum(jnp.where(pick1, before, 0.0), axis=-1, keepdims=True)
    r2 = jnp.sum(jnp.where(pick2, before, 0.0), axis=-1, keepdims=True)
    carry[...] += jnp.sum(onehot, axis=0, keepdims=True)
    info = jnp.where(lane == 0, i1.astype(F32), 0.0)
    info = jnp.where(lane == 1, i2.astype(F32), info)
    info = jnp.where(lane == 2, r1, info)
    info = jnp.where(lane == 3, r2, info)
    info = jnp.where(lane == 4, g1, info)
    info = jnp.where(lane == 5, g2, info)
    info_ref[...] = info
    cnt_ref[...] = jnp.broadcast_to(carry[...], cnt_ref.shape)


def _router(h, yc, yd, yg, wo_stack, layer, g, rw, tri):
    t = h.shape[0]
    tm = ROUTER_TM
    row = lambda w_: pl.BlockSpec((tm, w_), lambda i: (i, 0))
    const = lambda a, c: pl.BlockSpec((a, c), lambda i: (0, 0))
    return pl.pallas_call(
        _router_kernel,
        grid=(t // tm,),
        in_specs=[row(D_MODEL), row(CONV_CH), row(DIFF_W), row(GLA_V),
                  _layer_spec(layer, D_MODEL, D_MODEL),
                  const(1, D_MODEL), const(D_MODEL, LANES), const(tm, tm)],
        out_specs=[row(D_MODEL), pl.BlockSpec((tm * ROW_TILE, LANES), lambda i: (i, 0)), row(LANES),
                   const(SUBLANES, LANES)],
        out_shape=[jax.ShapeDtypeStruct((t, D_MODEL), F32),
                   jax.ShapeDtypeStruct((t * ROW_TILE, LANES), F32),
                   jax.ShapeDtypeStruct((t, LANES), F32),
                   jax.ShapeDtypeStruct((SUBLANES, LANES), F32)],
        scratch_shapes=[pltpu.VMEM((1, LANES), F32)],
        compiler_params=_cparams(("arbitrary",)),
        name="router",
    )(h, yc, yd, yg, wo_stack, g, rw, tri)


def _row_copy(src, dst, sem):
    return pltpu.make_async_copy(src, dst, sem)


def _tile_at(ref, start):
    if not isinstance(start, int):
        start = pl.multiple_of(start, ROW_TILE)
    return ref.at[pl.ds(start, ROW_TILE), :]


def _dispatch_kernel(dest_ref, hn_ref, xs_in_ref, xs_ref, sem):
    del xs_in_ref

    def issue(t, carry):
        src = _tile_at(hn_ref, t * ROW_TILE)
        _row_copy(src, _tile_at(xs_ref, dest_ref[0, 0, 2 * t]), sem).start()
        _row_copy(src, _tile_at(xs_ref, dest_ref[0, 0, 2 * t + 1]), sem).start()
        return carry

    lax.fori_loop(0, TMD, issue, 0, unroll=DMA_UNROLL)

    def drain(t, carry):
        _row_copy(_tile_at(hn_ref, 0), _tile_at(xs_ref, 0), sem).wait()
        _row_copy(_tile_at(hn_ref, 0), _tile_at(xs_ref, 0), sem).wait()
        return carry

    lax.fori_loop(0, TMD, drain, 0, unroll=DMA_UNROLL)


def _dispatch(dest, hn, xs_zero):
    t = hn.shape[0] // ROW_TILE
    return pl.pallas_call(
        _dispatch_kernel,
        grid=(t // TMD,),
        in_specs=[pl.BlockSpec((1, 1, 2 * TMD), lambda i: (i, 0, 0), memory_space=pltpu.SMEM),
                  pl.BlockSpec((TMD * ROW_TILE, LANES), lambda i: (i, 0)),
                  pl.BlockSpec(memory_space=pl.ANY)],
        out_specs=pl.BlockSpec(memory_space=pl.ANY),
        out_shape=jax.ShapeDtypeStruct(xs_zero.shape, F32),
        scratch_shapes=[pltpu.SemaphoreType.DMA(())],
        input_output_aliases={2: 0},
        compiler_params=_cparams(("arbitrary",)),
        name="dispatch",
    )(dest, hn, xs_zero)


def _ple_math(h, p_blk, g_ref, wg_ref, wu_ref, fg_ref, final):
    hn = _rms(h, g_ref[...]).astype(BF16)
    gate = _sigmoid(jnp.dot(hn, wg_ref[0], preferred_element_type=F32))
    up = jnp.dot(p_blk.astype(BF16), wu_ref[0], preferred_element_type=F32)
    out = h + gate * up
    if final:
        out = _rms(out, fg_ref[...])
    return out


def _combine_kernel(dcur_ref, dnext_ref, h_ref, info_ref, p_ref, g_ref, wg_ref, wu_ref, fg_ref, ys_ref,
                    *rest, final, fuse_next):
    next_in, o_ref, next_out, (buf, sems) = _split_fused_refs(rest, fuse_next)
    i = pl.program_id(0)
    slot = i % 2

    def gather(dref, slot_):
        def issue(t, carry):
            for pick in range(2):
                _row_copy(_tile_at(ys_ref, dref[0, 0, 2 * t + pick]),
                          _tile_at(buf.at[slot_, pick], t * ROW_TILE), sems.at[slot_]).start()
            return carry

        lax.fori_loop(0, TMC, issue, 0, unroll=DMA_UNROLL)

    @pl.when(i == 0)
    def _():
        gather(dcur_ref, 0)

    @pl.when(i + 1 < pl.num_programs(0))
    def _():
        gather(dnext_ref, 1 - slot)

    def drain(t, carry):
        for pick in range(2):
            _row_copy(_tile_at(ys_ref, 0), _tile_at(buf.at[slot, pick], 0), sems.at[slot]).wait()
        return carry

    lax.fori_loop(0, TMC, drain, 0, unroll=DMA_UNROLL)
    g1 = info_ref[:, 4:5]
    g2 = info_ref[:, 5:6]
    y1 = _tile_rows_load(buf.at[slot, 0], TMC)
    y2 = _tile_rows_load(buf.at[slot, 1], TMC)
    h2 = h_ref[...] + (g1 * y1 + g2 * y2)
    out = _ple_math(h2, p_ref[0], g_ref, wg_ref, wu_ref, fg_ref, final)
    o_ref[...] = out
    if fuse_next:
        _inproj_math(out, next_in, next_out)


def _split_fused_refs(rest, fuse_next):
    if not fuse_next:
        return (), rest[0], (), rest[1:]
    n_out = N_INPROJ_IN + 1 + N_INPROJ_OUT
    return rest[:N_INPROJ_IN], rest[N_INPROJ_IN], rest[N_INPROJ_IN + 1:n_out], rest[n_out:]


def _combine(dest, h, info, ys, ple_args, next_args, seq):
    p_stack, g, wg_stack, wu_stack, layer, fg, final = ple_args
    t = h.shape[0]
    n = t // TMC
    dest = dest.reshape(n, 1, 2 * TMC)
    row = lambda w_: pl.BlockSpec((TMC, w_), lambda i: (i, 0))
    const = lambda a, c: pl.BlockSpec((a, c), lambda i: (0, 0))
    in_specs = [pl.BlockSpec((1, 1, 2 * TMC), lambda i: (i, 0, 0), memory_space=pltpu.SMEM),
                pl.BlockSpec((1, 1, 2 * TMC), lambda i: (jnp.minimum(i + 1, n - 1), 0, 0),
                             memory_space=pltpu.SMEM),
                row(D_MODEL), row(LANES),
                pl.BlockSpec((1, TMC, PLE_DIM), lambda i: (layer, i, 0)),
                const(1, D_MODEL), _layer_spec(layer, D_MODEL, D_MODEL),
                _layer_spec(layer, PLE_DIM, D_MODEL), const(1, D_MODEL),
                pl.BlockSpec(memory_space=pl.ANY)]
    out_specs = [row(D_MODEL)]
    out_shape = [jax.ShapeDtypeStruct(h.shape, F32)]
    operands = [dest, dest, h, info, p_stack, g, wg_stack, wu_stack, fg, ys]
    if next_args is not None:
        nin, nout, nshape = _inproj_specs(next_args[0], TMC, t, seq)
        in_specs, out_specs, out_shape = in_specs + nin, out_specs + nout, out_shape + nshape
        operands += list(next_args[1:])
    res = pl.pallas_call(
        functools.partial(_combine_kernel, final=final, fuse_next=next_args is not None),
        grid=(n,),
        in_specs=in_specs,
        out_specs=out_specs,
        out_shape=out_shape,
        scratch_shapes=[pltpu.VMEM((2, 2, TMC * ROW_TILE, LANES), F32), pltpu.SemaphoreType.DMA((2,))],
        compiler_params=_cparams(("arbitrary",)),
        name="combine",
    )(*operands)
    return res[0], res[1:]


def _moe(h, yc, yd, yg, wo_stack, layer, g, rw, w_gu, w_down, moe_layer, tri_strict, ple_args,
         next_args, seq):
    t = h.shape[0]
    h1, hn, info, cnt = _router(h, yc, yd, yg, wo_stack, layer, g, rw, tri_strict)
    counts = cnt[0, :N_EXPERTS].astype(jnp.int32)
    padded = ((counts + MOE_BLK - 1) // MOE_BLK) * MOE_BLK
    pend = jnp.cumsum(padded)
    pstart = pend - padded
    e = info[:, 0:2].astype(jnp.int32)
    r = info[:, 2:4].astype(jnp.int32)
    onehot = e[:, :, None] == jnp.arange(N_EXPERTS, dtype=jnp.int32)[None, None, :]
    dest = r + jnp.sum(jnp.where(onehot, pstart[None, None, :], 0), axis=-1)
    dest = (dest * ROW_TILE).reshape(t // TMD, 1, 2 * TMD)
    m_rows = 2 * t + N_EXPERTS * MOE_BLK
    nb = m_rows // MOE_BLK
    blk_start = jnp.arange(nb, dtype=jnp.int32) * MOE_BLK
    block_e = jnp.minimum(jnp.sum(blk_start[:, None] >= pend[None, :], axis=-1), N_EXPERTS - 1)
    n_used = (pend[-1:] // MOE_BLK).astype(jnp.int32)

    xs = _dispatch(dest, hn, jnp.zeros((m_rows * ROW_TILE, LANES), F32))
    ys = _experts(xs, w_gu, w_down, moe_layer, block_e.astype(jnp.int32), n_used)
    return _combine(dest, h1, info, ys, ple_args, next_args, seq)


def _ple_kernel(h_ref, p_ref, g_ref, wg_ref, wu_ref, fg_ref, *rest, final, fuse_next):
    next_in, o_ref, next_out, _ = _split_fused_refs(rest, fuse_next)
    out = _ple_math(h_ref[...], p_ref[0], g_ref, wg_ref, wu_ref, fg_ref, final)
    o_ref[...] = out
    if fuse_next:
        _inproj_math(out, next_in, next_out)


def _ple(h, ple_args, next_args, seq):
    p_stack, g, wg_stack, wu_stack, layer, fg, final = ple_args
    t = h.shape[0]
    row = lambda w_: pl.BlockSpec((TM, w_), lambda i: (i, 0))
    const = lambda a, c: pl.BlockSpec((a, c), lambda i: (0, 0))
    in_specs = [row(D_MODEL), pl.BlockSpec((1, TM, PLE_DIM), lambda i: (layer, i, 0)),
                const(1, D_MODEL), _layer_spec(layer, D_MODEL, D_MODEL),
                _layer_spec(layer, PLE_DIM, D_MODEL), const(1, D_MODEL)]
    out_specs = [row(D_MODEL)]
    out_shape = [jax.ShapeDtypeStruct(h.shape, F32)]
    operands = [h, p_stack, g, wg_stack, wu_stack, fg]
    if next_args is not None:
        nin, nout, nshape = _inproj_specs(next_args[0], TM, t, seq)
        in_specs, out_specs, out_shape = in_specs + nin, out_specs + nout, out_shape + nshape
        operands += list(next_args[1:])
    res = pl.pallas_call(
        functools.partial(_ple_kernel, final=final, fuse_next=next_args is not None),
        grid=(t // TM,),
        in_specs=in_specs,
        out_specs=out_specs,
        out_shape=out_shape,
        compiler_params=_cparams(("parallel",)),
        name="ple",
    )(*operands)
    return res[0], res[1:]


def _rope_tables(seq):
    half = ROT_DIMS // 2
    pos = jnp.arange(seq, dtype=F32)
    inv_freq = ROPE_THETA ** (-jnp.arange(0, ROT_DIMS, 2, dtype=F32) / ROT_DIMS)
    ang = pos[:, None] * inv_freq[None, :]
    cos, sin = jnp.cos(ang), jnp.sin(ang)
    lane = np.arange(LANES) % DIFF_DH
    fidx = lane % half
    first = jnp.asarray(lane < half)[None, :]
    second = jnp.asarray((lane >= half) & (lane < ROT_DIMS))[None, :]
    ra = jnp.where(first | second, cos[:, fidx], 1.0)
    rb = jnp.where(second, sin[:, fidx], 0.0)
    rc = jnp.where(first, -sin[:, fidx], 0.0)
    return ra, rb, rc


def kernel(x, p, norm_mix_g, w_in, conv_w, conv_b, conv_ln_g, conv_ln_b, diff_lambda, diff_subln_g,
           gla_w_gate2, gla_b_gate, gla_norm_g, w_out, norm_ffn_g, ffn_w_gu, ffn_w_down, router_w,
           moe_w_gu, moe_w_down, ple_w_up, ple_w_gate, ple_norm_g, final_norm_g):
    bsz, seq, d = x.shape
    depth = w_in.shape[0]
    t = bsz * seq
    assert d == D_MODEL and seq % TM == 0 and seq % TQ == 0 and t % TMD == 0

    ropes = _rope_tables(seq)
    idx = np.arange(GLA_TRI)
    tri_chunk = jnp.asarray(((idx[:, None] >= idx[None, :])
                             & (idx[:, None] // CHUNK == idx[None, :] // CHUNK)).astype(np.float32)).astype(BF16)
    vi = np.arange(GLA_V)
    ki = np.arange(GLA_QK)
    gmat = jnp.asarray((vi[:, None] // GLA_DV == vi[None, :] // GLA_DV).astype(np.float32)
                       / GLA_DV).astype(BF16)
    head_mask = jnp.asarray((vi[:, None] // GLA_DV == ki[None, :] // GLA_DK).astype(np.float32))
    ti = np.arange(ROUTER_TM)
    tri_strict = jnp.asarray((ti[:, None] > ti[None, :]).astype(np.float32)).astype(BF16)

    w_in_b = jnp.pad(w_in, ((0, 0), (0, 0), (0, D_IN_PAD - D_IN))).astype(BF16)
    w_out_b = w_out.astype(BF16)
    ffn_gu_b = ffn_w_gu.astype(BF16)[:, None]
    ffn_down_b = ffn_w_down.astype(BF16)[:, None]
    moe_gu_b = moe_w_gu.astype(BF16)
    moe_down_b = moe_w_down.astype(BF16)
    ple_gate_b = ple_w_gate.astype(BF16)
    ple_up_b = ple_w_up.astype(BF16)
    p_rows = p.reshape(depth, t, PLE_DIM)

    def inproj_args(i):
        wg2 = jnp.pad(gla_w_gate2[i], ((0, LANES - GLA_GATE_RANK), (0, 0)))
        return (i, norm_mix_g[i][None, :], w_in_b, *ropes, wg2, gla_b_gate[i][None, :])

    h = x.reshape(t, D_MODEL)
    mixed = _inproj(h, inproj_args(0), seq)
    for i in range(depth):
        lam_init = 0.8 - 0.6 * math.exp(-0.3 * i)
        u, dq, dk, dv, gq, gk, gv, gr, la = mixed
        next_args = inproj_args(i + 1) if i + 1 < depth else None
        y_conv = _conv(u, conv_w[i], conv_b[i][None, :], conv_ln_g[i][None, :],
                       conv_ln_b[i][None, :], bsz, seq)
        y_diff = _attn(dq, dk, dv, diff_lambda[i], diff_subln_g[i][None, :], bsz, seq, lam_init)
        y_gla = _gla(gq, gk, gv, gr, la, tri_chunk, gmat, head_mask,
                     jnp.tile(gla_norm_g[i], GLA_HEADS)[None, :], bsz, seq)
        g_ffn = norm_ffn_g[i][None, :]
        j = i // 2
        ple_args = (p_rows, ple_norm_g[i][None, :], ple_gate_b, ple_up_b, i, final_norm_g[None, :],
                    i == depth - 1)
        if i % 2 == 0:
            h = _ffn(h, y_conv, y_diff, y_gla, w_out_b, i, g_ffn, ffn_gu_b, ffn_down_b, j)
            h, mixed = _ple(h, ple_args, next_args, seq)
        else:
            rw = jnp.pad(router_w[j], ((0, 0), (0, LANES - N_EXPERTS)))
            h, mixed = _moe(h, y_conv, y_diff, y_gla, w_out_b, i, g_ffn, rw, moe_gu_b, moe_down_b, j,
                            tri_strict, ple_args, next_args, seq)
    return h.reshape(bsz, seq, D_MODEL)
```

```python
import functools
import math

import jax
import jax.numpy as jnp
import numpy as np
from jax import lax
from jax.experimental import pallas as pl
from jax.experimental.pallas import tpu as pltpu

F32 = jnp.float32
BF16 = jnp.bfloat16

D_MODEL = 1024
CHUNK = 64
CONV_CH = 256
CONV_WIDTH = 31
DIFF_HEADS = 4
DIFF_DV = 128
DIFF_DH = 64
GLA_HEADS = 4
GLA_DV = 64
GLA_DK = 32
GLA_GATE_RANK = 16
GLA_TAU = 16.0
ROPE_THETA = 500000.0
ROT_DIMS = 16
D_FF = 2816
N_EXPERTS = 8
D_FF_EXPERT = 3584
PLE_DIM = 256
EPS = 1e-6
DIFF_W = 512
GLA_QK = GLA_HEADS * GLA_DK
GLA_V = GLA_HEADS * GLA_DV
D_IN = 2832
D_IN_PAD = 2944

LANES = 128
SUBLANES = 8
VMEM_LIMIT = 56 * 1024 * 1024
LOG2E = math.log2(math.e)

TM = 512
TK = 1024
TQ = TK
Q_SUB = 512
Q_PIECE = 256
CONV_TS = 512
CONV_HALO = 32
CONV_SH_ROWS = CONV_TS + CONV_HALO - SUBLANES
GLA_TG = 1024
GLA_TRI = 512
FFN_FB = 1408
MOE_BLK = 512
MOE_FB = 1792
ROW_TILE = D_MODEL // LANES
ROUTER_TM = 1024
TMD = 512
TMC = 256
DMA_UNROLL = 8


def _cparams(sem):
    return pltpu.CompilerParams(dimension_semantics=sem, vmem_limit_bytes=VMEM_LIMIT)


def _rms(x, g):
    ms = jnp.mean(x * x, axis=-1, keepdims=True)
    return x * lax.rsqrt(ms + EPS) * g


def _sigmoid(x):
    return 1.0 / (1.0 + jnp.exp(-x))


def _silu(x):
    return x * _sigmoid(x)


def _split_bf16(x):
    hi = x.astype(BF16)
    return hi, (x - hi.astype(F32)).astype(BF16)


def _dot_f32(a, b):
    a_hi, a_lo = _split_bf16(a)
    b_hi, b_lo = _split_bf16(b)
    dot = functools.partial(jnp.dot, preferred_element_type=F32)
    return dot(a_hi, b_hi) + (dot(a_hi, b_lo) + dot(a_lo, b_hi))


def _rope(x, ra, rb, rc):
    outs = []
    for c in range(x.shape[1] // LANES):
        xc = x[:, c * LANES:(c + 1) * LANES]
        outs.append(xc * ra + pltpu.roll(xc, ROT_DIMS // 2, 1) * rb
                    + pltpu.roll(xc, LANES - ROT_DIMS // 2, 1) * rc)
    return jnp.concatenate(outs, axis=1)


N_INPROJ_IN = 7
N_INPROJ_OUT = 9


def _inproj_math(x, in_refs, out_refs):
    g_ref, w_ref, ra_ref, rb_ref, rc_ref, wg2_ref, bg_ref = in_refs
    u_ref, q_ref, k_ref, v_ref, gq_ref, gk_ref, gv_ref, gr_ref, la_ref = out_refs
    hn = _rms(x, g_ref[...]).astype(BF16)

    def proj(a, b):
        return jnp.dot(hn, w_ref[0, :, a:b], preferred_element_type=F32)

    ra, rb, rc = ra_ref[...], rb_ref[...], rc_ref[...]
    u_ref[...] = proj(0, 256) * _sigmoid(proj(256, 512))
    q_ref[...] = (_rope(proj(512, 1024), ra, rb, rc) * (LOG2E * DIFF_DH ** -0.5)).astype(BF16)
    k_ref[...] = _rope(proj(1024, 1536), ra, rb, rc).astype(BF16)
    v_ref[...] = proj(1536, 2048).astype(BF16)
    gq_ref[...] = proj(2048, 2176) * (GLA_DK ** -0.5)
    gk_ref[...] = proj(2176, 2304)
    gv_ref[...] = proj(2304, 2560)
    gr_ref[...] = _silu(proj(2560, 2816))
    gz = proj(2816, D_IN_PAD)
    ga = _dot_f32(gz, wg2_ref[...]) + bg_ref[...]
    la_ref[...] = (jnp.minimum(ga, 0.0) - jnp.log(1.0 + jnp.exp(-jnp.abs(ga)))) * (1.0 / GLA_TAU)


def _inproj_kernel(h_ref, *refs):
    _inproj_math(h_ref[...], refs[:N_INPROJ_IN], refs[N_INPROJ_IN:])


def _layer_spec(layer, a, b):
    return pl.BlockSpec((1, a, b), lambda i: (layer, 0, 0))


def _inproj_specs(layer, tile, t, seq):
    nseq = seq // tile
    const = lambda a, b: pl.BlockSpec((a, b), lambda i: (0, 0))
    rope_spec = pl.BlockSpec((tile, LANES), lambda i: (i % nseq, 0))
    outs = [(CONV_CH, F32), (DIFF_W, BF16), (DIFF_W, BF16), (DIFF_W, BF16),
            (GLA_QK, F32), (GLA_QK, F32), (GLA_V, F32), (GLA_V, F32), (GLA_QK, F32)]
    in_specs = [const(1, D_MODEL), _layer_spec(layer, D_MODEL, D_IN_PAD),
                rope_spec, rope_spec, rope_spec, const(LANES, LANES), const(1, LANES)]
    out_specs = [pl.BlockSpec((tile, w_), lambda i: (i, 0)) for w_, _ in outs]
    out_shape = [jax.ShapeDtypeStruct((t, w_), dt) for w_, dt in outs]
    return in_specs, out_specs, out_shape


def _inproj(h, inproj_args, seq):
    t = h.shape[0]
    in_specs, out_specs, out_shape = _inproj_specs(inproj_args[0], TM, t, seq)
    return pl.pallas_call(
        _inproj_kernel,
        grid=(t // TM,),
        in_specs=[pl.BlockSpec((TM, D_MODEL), lambda i: (i, 0))] + in_specs,
        out_specs=out_specs,
        out_shape=out_shape,
        compiler_params=_cparams(("parallel",)),
        name="inproj",
    )(h, *inproj_args[1:])


def _conv_kernel(u_ref, w_ref, b_ref, lg_ref, lb_ref, o_ref, win, shifted):
    @pl.when(pl.program_id(1) == 0)
    def _():
        win[0:CONV_HALO, :] = jnp.zeros((CONV_HALO, CONV_CH), F32)

    win[CONV_HALO:, :] = u_ref[...]
    for r in range(1, SUBLANES):
        shifted[r - 1] = win[r:r + CONV_SH_ROWS, :]
    rows = 64
    base = CONV_HALO - (CONV_WIDTH - 1)
    for r0 in range(0, CONV_TS, rows):
        acc = jnp.zeros((rows, CONV_CH), F32) + b_ref[...]
        for j in range(CONV_WIDTH):
            res = (base + j) % SUBLANES
            start = r0 + (base + j) - res
            if res == 0:
                tap = win[start:start + rows, :]
            else:
                tap = shifted[res - 1, start:start + rows, :]
            acc = acc + tap * w_ref[j:j + 1, :]
        mu = jnp.mean(acc, axis=-1, keepdims=True)
        xc = acc - mu
        var = jnp.mean(xc * xc, axis=-1, keepdims=True)
        y = xc * lax.rsqrt(var + EPS) * lg_ref[...] + lb_ref[...]
        o_ref[r0:r0 + rows, :] = _silu(y).astype(BF16)
    win[0:CONV_HALO, :] = win[CONV_TS:CONV_TS + CONV_HALO, :]


def _conv(u, w, b, lg, lb, bsz, seq):
    nt = seq // CONV_TS
    const = lambda a, c: pl.BlockSpec((a, c), lambda bi, ti: (0, 0))
    return pl.pallas_call(
        _conv_kernel,
        grid=(bsz, nt),
        in_specs=[pl.BlockSpec((CONV_TS, CONV_CH), lambda bi, ti: (bi * nt + ti, 0)),
                  const(CONV_WIDTH, CONV_CH), const(1, CONV_CH), const(1, CONV_CH), const(1, CONV_CH)],
        out_specs=pl.BlockSpec((CONV_TS, CONV_CH), lambda bi, ti: (bi * nt + ti, 0)),
        out_shape=jax.ShapeDtypeStruct(u.shape, BF16),
        scratch_shapes=[pltpu.VMEM((CONV_TS + CONV_HALO, CONV_CH), F32),
                        pltpu.VMEM((SUBLANES - 1, CONV_SH_ROWS, CONV_CH), F32)],
        compiler_params=_cparams(("arbitrary", "arbitrary")),
        name="conv",
    )(u, w, b, lg, lb)


def _attn_kernel(qt_ref, kt_ref, lam_ref, sg_ref, q_ref, k_ref, v_ref, o_ref, m_scr, acc_scr,
                 *, lam_init):
    step = pl.program_id(1)
    qi = qt_ref[step]
    kj = kt_ref[step]

    @pl.when(kj == 0)
    def _():
        m_scr[...] = jnp.full(m_scr.shape, -jnp.inf, F32)
        acc_scr[...] = jnp.zeros(acc_scr.shape, F32)

    def update(sub, masked):
        if masked:
            pieces = [(sub * Q_SUB + r, Q_PIECE, sub * Q_SUB + r + Q_PIECE) for r in range(0, Q_SUB, Q_PIECE)]
        else:
            pieces = [(sub * Q_SUB, Q_SUB, TK)]
        for r0, nr, nk in pieces:
            rows = slice(r0, r0 + nr)
            if masked:
                rq = (lax.broadcasted_iota(jnp.int32, (nr, nk), 0) + r0) // CHUNK
                ck = lax.broadcasted_iota(jnp.int32, (nr, nk), 1) // CHUNK
                allowed = ck <= rq
            for h in range(DIFF_HEADS):
                hs = slice(h * LANES, (h + 1) * LANES)
                q = q_ref[rows, hs]
                k = k_ref[0:nk, hs]
                v = v_ref[0:nk, hs]
                vext = jnp.concatenate([v, jnp.ones_like(v)], axis=1)
                lane = lax.broadcasted_iota(jnp.int32, q.shape, 1)
                for c in range(2):
                    sel = (lane < DIFF_DH) if c == 0 else (lane >= DIFF_DH)
                    qc = jnp.where(sel, q, jnp.zeros_like(q))
                    s = lax.dot_general(qc, k, (((1,), (1,)), ((), ())), preferred_element_type=F32)
                    if masked:
                        s = jnp.where(allowed, s, -jnp.inf)
                    m_old = m_scr[2 * h + c, rows]
                    m_new = jnp.maximum(m_old, jnp.max(s, axis=-1, keepdims=True))
                    alpha = jnp.exp2(m_old - m_new)
                    p = jnp.exp2(s - jnp.tile(m_new, (1, nk // LANES)))
                    pv = jnp.dot(p.astype(BF16), vext, preferred_element_type=F32)
                    acc_scr[2 * h + c, rows] = jnp.tile(alpha, (1, 2)) * acc_scr[2 * h + c, rows] + pv
                    m_scr[2 * h + c, rows] = m_new

    @pl.when(kj < qi)
    def _():
        for sub in range(TQ // Q_SUB):
            update(sub, False)

    @pl.when(kj == qi)
    def _():
        for sub in range(TQ // Q_SUB):
            update(sub, True)
        lp = lam_ref[...]
        lam = (jnp.exp(jnp.sum(lp[0:1] * lp[1:2], axis=-1, keepdims=True))
               - jnp.exp(jnp.sum(lp[2:3] * lp[3:4], axis=-1, keepdims=True)) + lam_init)
        for h in range(DIFF_HEADS):
            a0 = acc_scr[2 * h]
            a1 = acc_scr[2 * h + 1]
            o = a0[:, :DIFF_DV] / a0[:, DIFF_DV:] - lam * (a1[:, :DIFF_DV] / a1[:, DIFF_DV:])
            o_ref[:, h * LANES:(h + 1) * LANES] = (_rms(o, sg_ref[...]) * (1.0 - lam_init)).astype(BF16)


def _attn(q, k, v, lam_p, sg, bsz, seq, lam_init):
    nq = seq // TQ
    nk = seq // TK
    pairs = [(a, b) for a in range(nq) for b in range(a + 1)]
    qt = jnp.asarray(np.array([a for a, _ in pairs], np.int32))
    kt = jnp.asarray(np.array([b for _, b in pairs], np.int32))
    qspec = pl.BlockSpec((TQ, DIFF_W), lambda b, s, qt_, kt_: (b * nq + qt_[s], 0))
    kspec = pl.BlockSpec((TK, DIFF_W), lambda b, s, qt_, kt_: (b * nk + kt_[s], 0))
    const = lambda a, c: pl.BlockSpec((a, c), lambda b, s, qt_, kt_: (0, 0))
    grid_spec = pltpu.PrefetchScalarGridSpec(
        num_scalar_prefetch=2,
        grid=(bsz, len(pairs)),
        in_specs=[const(4, DIFF_DH), const(1, DIFF_DV), qspec, kspec, kspec],
        out_specs=qspec,
        scratch_shapes=[pltpu.VMEM((2 * DIFF_HEADS, TQ, LANES), F32),
                        pltpu.VMEM((2 * DIFF_HEADS, TQ, 2 * DIFF_DV), F32)],
    )
    return pl.pallas_call(
        functools.partial(_attn_kernel, lam_init=lam_init),
        grid_spec=grid_spec,
        out_shape=jax.ShapeDtypeStruct(q.shape, BF16),
        compiler_params=_cparams(("parallel", "arbitrary")),
        name="diffattn",
    )(qt, kt, lam_p, sg, q, k, v)


def _gla_kernel(q_ref, k_ref, v_ref, r_ref, la_ref, tri_ref, gmat_ref, mask_ref, ng_ref, o_ref,
                st, kv_scr, st_scr, o_scr):
    @pl.when(pl.program_id(1) == 0)
    def _():
        st[...] = jnp.zeros(st.shape, F32)

    cums = []
    for r0 in range(0, GLA_TG, GLA_TRI):
        la_hi, la_lo = _split_bf16(la_ref[r0:r0 + GLA_TRI, :])
        cums.append(jnp.dot(tri_ref[...], la_hi, preferred_element_type=F32)
                    + jnp.dot(tri_ref[...], la_lo, preferred_element_type=F32))
    cum = jnp.concatenate(cums, axis=0)
    same_head = mask_ref[...] > 0.0
    nch = GLA_TG // CHUNK
    tots = []
    for c in range(nch):
        sl = slice(c * CHUNK, (c + 1) * CHUNK)
        cum_c = cum[sl]
        tot = cum_c[CHUNK - 1:CHUNK]
        tots.append(tot)
        kdec = (k_ref[sl, :] * jnp.exp(tot - cum_c)).astype(BF16)
        v_t = v_ref[sl, :].T.astype(BF16)
        kv_scr[c] = jnp.dot(v_t, kdec, preferred_element_type=F32)
    state = st[...]
    for c in range(nch):
        state = state * jnp.exp(tots[c]) + jnp.where(same_head, kv_scr[c], 0.0)
        st_scr[c] = state.astype(BF16)
    st[...] = state
    for c in range(nch):
        sl = slice(c * CHUNK, (c + 1) * CHUNK)
        o_scr[sl, :] = lax.dot_general(q_ref[sl, :].astype(BF16), st_scr[c], (((1,), (1,)), ((), ())),
                                       preferred_element_type=F32)
    o = o_scr[...]
    ms = jnp.dot((o * o).astype(BF16), gmat_ref[...], preferred_element_type=F32)
    o_ref[...] = (o * lax.rsqrt(ms + EPS) * ng_ref[...] * r_ref[...]).astype(BF16)


def _gla(gq, gk, gv, gr, la, tri, gmat, head_mask, ng, bsz, seq):
    nt = seq // GLA_TG
    row = lambda w_: pl.BlockSpec((GLA_TG, w_), lambda bi, ti: (bi * nt + ti, 0))
    const = lambda a, c: pl.BlockSpec((a, c), lambda bi, ti: (0, 0))
    return pl.pallas_call(
        _gla_kernel,
        grid=(bsz, nt),
        in_specs=[row(GLA_QK), row(GLA_QK), row(GLA_V), row(GLA_V), row(GLA_QK),
                  const(GLA_TRI, GLA_TRI), const(GLA_V, GLA_V), const(GLA_V, GLA_QK), const(1, GLA_V)],
        out_specs=row(GLA_V),
        out_shape=jax.ShapeDtypeStruct(gv.shape, BF16),
        scratch_shapes=[pltpu.VMEM((GLA_V, GLA_QK), F32),
                        pltpu.VMEM((GLA_TG // CHUNK, GLA_V, GLA_QK), F32),
                        pltpu.VMEM((GLA_TG // CHUNK, GLA_V, GLA_QK), BF16),
                        pltpu.VMEM((GLA_TG, GLA_V), F32)],
        compiler_params=_cparams(("arbitrary", "arbitrary")),
        name="gla",
    )(gq, gk, gv, gr, la, tri, gmat, head_mask, ng)


def _mix_residual(h_ref, yc_ref, yd_ref, yg_ref, wo_ref):
    acc = jnp.dot(yc_ref[...], wo_ref[0, 0:256, :], preferred_element_type=F32)
    acc = acc + jnp.dot(yd_ref[...], wo_ref[0, 256:768, :], preferred_element_type=F32)
    acc = acc + jnp.dot(yg_ref[...], wo_ref[0, 768:1024, :], preferred_element_type=F32)
    return h_ref[...] + acc


def _swiglu_step(xb, wg_ref, wu_ref, wd_ref, acc):
    x = xb[...]
    g = jnp.dot(x, wg_ref[0, 0], preferred_element_type=F32)
    u = jnp.dot(x, wu_ref[0, 0], preferred_element_type=F32)
    a = (_silu(g) * u).astype(BF16)
    acc[...] += jnp.dot(a, wd_ref[0, 0], preferred_element_type=F32)


def _ffn_kernel(h_ref, yc_ref, yd_ref, yg_ref, wo_ref, g_ref, wg_ref, wu_ref, wd_ref, o_ref,
                h1, xb, acc):
    j = pl.program_id(1)

    @pl.when(j == 0)
    def _():
        x = _mix_residual(h_ref, yc_ref, yd_ref, yg_ref, wo_ref)
        h1[...] = x
        xb[...] = _rms(x, g_ref[...]).astype(BF16)
        acc[...] = jnp.zeros(acc.shape, F32)

    _swiglu_step(xb, wg_ref, wu_ref, wd_ref, acc)

    @pl.when(j == pl.num_programs(1) - 1)
    def _():
        o_ref[...] = h1[...] + acc[...]


def _ffn(h, yc, yd, yg, wo_stack, layer, g, w_gu, w_down, ffn_layer):
    t = h.shape[0]
    nj = D_FF // FFN_FB
    row = lambda w_: pl.BlockSpec((TM, w_), lambda i, j: (i, 0))
    return pl.pallas_call(
        _ffn_kernel,
        grid=(t // TM, nj),
        in_specs=[row(D_MODEL), row(CONV_CH), row(DIFF_W), row(GLA_V),
                  pl.BlockSpec((1, D_MODEL, D_MODEL), lambda i, j: (layer, 0, 0)),
                  pl.BlockSpec((1, D_MODEL), lambda i, j: (0, 0)),
                  pl.BlockSpec((1, 1, D_MODEL, FFN_FB), lambda i, j: (ffn_layer, 0, 0, j)),
                  pl.BlockSpec((1, 1, D_MODEL, FFN_FB), lambda i, j: (ffn_layer, 0, 0, j + nj)),
                  pl.BlockSpec((1, 1, FFN_FB, D_MODEL), lambda i, j: (ffn_layer, 0, j, 0))],
        out_specs=row(D_MODEL),
        out_shape=jax.ShapeDtypeStruct((t, D_MODEL), F32),
        scratch_shapes=[pltpu.VMEM((TM, D_MODEL), F32), pltpu.VMEM((TM, D_MODEL), BF16),
                        pltpu.VMEM((TM, D_MODEL), F32)],
        compiler_params=_cparams(("parallel", "arbitrary")),
        name="ffn",
    )(h, yc, yd, yg, wo_stack, g, w_gu, w_gu, w_down)


def _tile_rows_load(ref, n):
    return jnp.concatenate([ref[pl.ds(s, n, stride=ROW_TILE), :] for s in range(ROW_TILE)], axis=1)


def _tile_rows_store(ref, x, n):
    for s in range(ROW_TILE):
        ref[pl.ds(s, n, stride=ROW_TILE), :] = x[:, s * LANES:(s + 1) * LANES]


def _experts_kernel(be_ref, nu_ref, x_ref, wg_ref, wu_ref, wd_ref, o_ref, xb, acc):
    i = pl.program_id(0)
    j = pl.program_id(1)
    used = i < nu_ref[0]

    @pl.when(used)
    def _():
        @pl.when(j == 0)
        def _():
            xb[...] = _tile_rows_load(x_ref, MOE_BLK).astype(BF16)
            acc[...] = jnp.zeros(acc.shape, F32)

        _swiglu_step(xb, wg_ref, wu_ref, wd_ref, acc)

    @pl.when(j == pl.num_programs(1) - 1)
    def _():
        _tile_rows_store(o_ref, jnp.where(used, acc[...], 0.0), MOE_BLK)


def _experts(xs, w_gu, w_down, layer, block_e, n_used):
    nb = xs.shape[0] // (MOE_BLK * ROW_TILE)
    nj = D_FF_EXPERT // MOE_FB

    def last(i, nu):
        return jnp.minimum(i, nu[0] - 1)

    def hid(i, j, nu):
        return jnp.where(i < nu[0], j, nj - 1)

    xspec = pl.BlockSpec((MOE_BLK * ROW_TILE, LANES), lambda i, j, be, nu: (last(i, nu), 0))
    wg = pl.BlockSpec((1, 1, D_MODEL, MOE_FB),
                      lambda i, j, be, nu: (layer, be[last(i, nu)], 0, hid(i, j, nu)))
    wu = pl.BlockSpec((1, 1, D_MODEL, MOE_FB),
                      lambda i, j, be, nu: (layer, be[last(i, nu)], 0, hid(i, j, nu) + nj))
    wd = pl.BlockSpec((1, 1, MOE_FB, D_MODEL),
                      lambda i, j, be, nu: (layer, be[last(i, nu)], hid(i, j, nu), 0))
    grid_spec = pltpu.PrefetchScalarGridSpec(
        num_scalar_prefetch=2,
        grid=(nb, nj),
        in_specs=[xspec, wg, wu, wd],
        out_specs=pl.BlockSpec((MOE_BLK * ROW_TILE, LANES), lambda i, j, be, nu: (i, 0)),
        scratch_shapes=[pltpu.VMEM((MOE_BLK, D_MODEL), BF16), pltpu.VMEM((MOE_BLK, D_MODEL), F32)],
    )
    return pl.pallas_call(
        _experts_kernel,
        grid_spec=grid_spec,
        out_shape=jax.ShapeDtypeStruct(xs.shape, F32),
        compiler_params=_cparams(("arbitrary", "arbitrary")),
        name="experts",
    )(block_e, n_used, xs, w_gu, w_gu, w_down)


def _router_kernel(h_ref, yc_ref, yd_ref, yg_ref, wo_ref, g_ref, rw_ref, tri_ref,
                   h1_ref, hn_ref, info_ref, cnt_ref, carry):
    @pl.when(pl.program_id(0) == 0)
    def _():
        carry[...] = jnp.zeros(carry.shape, F32)

    h1 = _mix_residual(h_ref, yc_ref, yd_ref, yg_ref, wo_ref)
    h1_ref[...] = h1
    hn = _rms(h1, g_ref[...])
    _tile_rows_store(hn_ref, hn, ROUTER_TM)
    logits = _dot_f32(hn, rw_ref[...])
    lane = lax.broadcasted_iota(jnp.int32, logits.shape, 1)
    lg = jnp.where(lane < N_EXPERTS, logits, -jnp.inf)
    v1 = jnp.max(lg, axis=-1, keepdims=True)
    i1 = jnp.min(jnp.where(lg == v1, lane, LANES), axis=-1, keepdims=True)
    lg2 = jnp.where(lane == i1, -jnp.inf, lg)
    v2 = jnp.max(lg2, axis=-1, keepdims=True)
    i2 = jnp.min(jnp.where(lg2 == v2, lane, LANES), axis=-1, keepdims=True)
    e2 = jnp.exp(v2 - v1)
    g1 = 1.0 / (1.0 + e2)
    g2 = e2 / (1.0 + e2)
    pick1 = lane == i1
    pick2 = lane == i2
    onehot = jnp.where(pick1 | pick2, 1.0, 0.0)
    before = jnp.dot(tri_ref[...], onehot.astype(BF16), preferred_element_type=F32) + carry[...]
    r1 = jnp.sum(jnp.where(pick1, before, 0.0), axis=-1, keepdims=True)
    r2 = jnp.sum(jnp.where(pick2, before, 0.0), axis=-1, keepdims=True)
    carry[...] += jnp.sum(onehot, axis=0, keepdims=True)
    info = jnp.where(lane == 0, i1.astype(F32), 0.0)
    info = jnp.where(lane == 1, i2.astype(F32), info)
    info = jnp.where(lane == 2, r1, info)
    info = jnp.where(lane == 3, r2, info)
    info = jnp.where(lane == 4, g1, info)
    info = jnp.where(lane == 5, g2, info)
    info_ref[...] = info
    cnt_ref[...] = jnp.broadcast_to(carry[...], cnt_ref.shape)


def _router(h, yc, yd, yg, wo_stack, layer, g, rw, tri):
    t = h.shape[0]
    tm = ROUTER_TM
    row = lambda w_: pl.BlockSpec((tm, w_), lambda i: (i, 0))
    const = lambda a, c: pl.BlockSpec((a, c), lambda i: (0, 0))
    return pl.pallas_call(
        _router_kernel,
        grid=(t // tm,),
        in_specs=[row(D_MODEL), row(CONV_CH), row(DIFF_W), row(GLA_V),
                  _layer_spec(layer, D_MODEL, D_MODEL),
                  const(1, D_MODEL), const(D_MODEL, LANES), const(tm, tm)],
        out_specs=[row(D_MODEL), pl.BlockSpec((tm * ROW_TILE, LANES), lambda i: (i, 0)), row(LANES),
                   const(SUBLANES, LANES)],
        out_shape=[jax.ShapeDtypeStruct((t, D_MODEL), F32),
                   jax.ShapeDtypeStruct((t * ROW_TILE, LANES), F32),
                   jax.ShapeDtypeStruct((t, LANES), F32),
                   jax.ShapeDtypeStruct((SUBLANES, LANES), F32)],
        scratch_shapes=[pltpu.VMEM((1, LANES), F32)],
        compiler_params=_cparams(("arbitrary",)),
        name="router",
    )(h, yc, yd, yg, wo_stack, g, rw, tri)


def _row_copy(src, dst, sem):
    return pltpu.make_async_copy(src, dst, sem)


def _tile_at(ref, start):
    if not isinstance(start, int):
        start = pl.multiple_of(start, ROW_TILE)
    return ref.at[pl.ds(start, ROW_TILE), :]


def _dispatch_kernel(dest_ref, hn_ref, xs_in_ref, xs_ref, sem):
    del xs_in_ref

    def issue(t, carry):
        src = _tile_at(hn_ref, t * ROW_TILE)
        _row_copy(src, _tile_at(xs_ref, dest_ref[0, 0, 2 * t]), sem).start(priority=0)
        _row_copy(src, _tile_at(xs_ref, dest_ref[0, 0, 2 * t + 1]), sem).start(priority=1)
        return carry

    lax.fori_loop(0, TMD, issue, 0, unroll=DMA_UNROLL)

    def drain(t, carry):
        _row_copy(_tile_at(hn_ref, 0), _tile_at(xs_ref, 0), sem).wait()
        _row_copy(_tile_at(hn_ref, 0), _tile_at(xs_ref, 0), sem).wait()
        return carry

    lax.fori_loop(0, TMD, drain, 0, unroll=DMA_UNROLL)


def _dispatch(dest, hn, xs_zero):
    t = hn.shape[0] // ROW_TILE
    return pl.pallas_call(
        _dispatch_kernel,
        grid=(t // TMD,),
        in_specs=[pl.BlockSpec((1, 1, 2 * TMD), lambda i: (i, 0, 0), memory_space=pltpu.SMEM),
                  pl.BlockSpec((TMD * ROW_TILE, LANES), lambda i: (i, 0)),
                  pl.BlockSpec(memory_space=pl.ANY)],
        out_specs=pl.BlockSpec(memory_space=pl.ANY),
        out_shape=jax.ShapeDtypeStruct(xs_zero.shape, F32),
        scratch_shapes=[pltpu.SemaphoreType.DMA(())],
        input_output_aliases={2: 0},
        compiler_params=_cparams(("arbitrary",)),
        name="dispatch",
    )(dest, hn, xs_zero)


def _ple_math(h, p_blk, g_ref, wg_ref, wu_ref, fg_ref, final):
    hn = _rms(h, g_ref[...]).astype(BF16)
    gate = _sigmoid(jnp.dot(hn, wg_ref[0], preferred_element_type=F32))
    up = jnp.dot(p_blk.astype(BF16), wu_ref[0], preferred_element_type=F32)
    out = h + gate * up
    if final:
        out = _rms(out, fg_ref[...])
    return out


def _combine_kernel(dcur_ref, dnext_ref, h_ref, info_ref, p_ref, g_ref, wg_ref, wu_ref, fg_ref, ys_ref,
                    *rest, final, fuse_next):
    next_in, o_ref, next_out, (buf, sems) = _split_fused_refs(rest, fuse_next)
    i = pl.program_id(0)
    slot = i % 2

    def gather(dref, slot_):
        def issue(t, carry):
            for pick in range(2):
                _row_copy(_tile_at(ys_ref, dref[0, 0, 2 * t + pick]),
                          _tile_at(buf.at[slot_, pick], t * ROW_TILE), sems.at[slot_]).start(priority=pick)
            return carry

        lax.fori_loop(0, TMC, issue, 0, unroll=DMA_UNROLL)

    @pl.when(i == 0)
    def _():
        gather(dcur_ref, 0)

    @pl.when(i + 1 < pl.num_programs(0))
    def _():
        gather(dnext_ref, 1 - slot)

    def drain(t, carry):
        for pick in range(2):
            _row_copy(_tile_at(ys_ref, 0), _tile_at(buf.at[slot, pick], 0), sems.at[slot]).wait()
        return carry

    lax.fori_loop(0, TMC, drain, 0, unroll=DMA_UNROLL)
    g1 = info_ref[:, 4:5]
    g2 = info_ref[:, 5:6]
    y1 = _tile_rows_load(buf.at[slot, 0], TMC)
    y2 = _tile_rows_load(buf.at[slot, 1], TMC)
    h2 = h_ref[...] + (g1 * y1 + g2 * y2)
    out = _ple_math(h2, p_ref[0], g_ref, wg_ref, wu_ref, fg_ref, final)
    o_ref[...] = out
    if fuse_next:
        _inproj_math(out, next_in, next_out)


def _split_fused_refs(rest, fuse_next):
    if not fuse_next:
        return (), rest[0], (), rest[1:]
    n_out = N_INPROJ_IN + 1 + N_INPROJ_OUT
    return rest[:N_INPROJ_IN], rest[N_INPROJ_IN], rest[N_INPROJ_IN + 1:n_out], rest[n_out:]


def _combine(dest, h, info, ys, ple_args, next_args, seq):
    p_stack, g, wg_stack, wu_stack, layer, fg, final = ple_args
    t = h.shape[0]
    n = t // TMC
    dest = dest.reshape(n, 1, 2 * TMC)
    row = lambda w_: pl.BlockSpec((TMC, w_), lambda i: (i, 0))
    const = lambda a, c: pl.BlockSpec((a, c), lambda i: (0, 0))
    in_specs = [pl.BlockSpec((1, 1, 2 * TMC), lambda i: (i, 0, 0), memory_space=pltpu.SMEM),
                pl.BlockSpec((1, 1, 2 * TMC), lambda i: (jnp.minimum(i + 1, n - 1), 0, 0),
                             memory_space=pltpu.SMEM),
                row(D_MODEL), row(LANES),
                pl.BlockSpec((1, TMC, PLE_DIM), lambda i: (layer, i, 0)),
                const(1, D_MODEL), _layer_spec(layer, D_MODEL, D_MODEL),
                _layer_spec(layer, PLE_DIM, D_MODEL), const(1, D_MODEL),
                pl.BlockSpec(memory_space=pl.ANY)]
    out_specs = [row(D_MODEL)]
    out_shape = [jax.ShapeDtypeStruct(h.shape, F32)]
    operands = [dest, dest, h, info, p_stack, g, wg_stack, wu_stack, fg, ys]
    if next_args is not None:
        nin, nout, nshape = _inproj_specs(next_args[0], TMC, t, seq)
        in_specs, out_specs, out_shape = in_specs + nin, out_specs + nout, out_shape + nshape
        operands += list(next_args[1:])
    res = pl.pallas_call(
        functools.partial(_combine_kernel, final=final, fuse_next=next_args is not None),
        grid=(n,),
        in_specs=in_specs,
        out_specs=out_specs,
        out_shape=out_shape,
        scratch_shapes=[pltpu.VMEM((2, 2, TMC * ROW_TILE, LANES), F32), pltpu.SemaphoreType.DMA((2,))],
        compiler_params=_cparams(("arbitrary",)),
        name="combine",
    )(*operands)
    return res[0], res[1:]


def _moe(h, yc, yd, yg, wo_stack, layer, g, rw, w_gu, w_down, moe_layer, tri_strict, ple_args,
         next_args, seq):
    t = h.shape[0]
    h1, hn, info, cnt = _router(h, yc, yd, yg, wo_stack, layer, g, rw, tri_strict)
    counts = cnt[0, :N_EXPERTS].astype(jnp.int32)
    padded = ((counts + MOE_BLK - 1) // MOE_BLK) * MOE_BLK
    pend = jnp.cumsum(padded)
    pstart = pend - padded
    e = info[:, 0:2].astype(jnp.int32)
    r = info[:, 2:4].astype(jnp.int32)
    onehot = e[:, :, None] == jnp.arange(N_EXPERTS, dtype=jnp.int32)[None, None, :]
    dest = r + jnp.sum(jnp.where(onehot, pstart[None, None, :], 0), axis=-1)
    dest = (dest * ROW_TILE).reshape(t // TMD, 1, 2 * TMD)
    m_rows = 2 * t + N_EXPERTS * MOE_BLK
    nb = m_rows // MOE_BLK
    blk_start = jnp.arange(nb, dtype=jnp.int32) * MOE_BLK
    block_e = jnp.minimum(jnp.sum(blk_start[:, None] >= pend[None, :], axis=-1), N_EXPERTS - 1)
    n_used = (pend[-1:] // MOE_BLK).astype(jnp.int32)

    xs = _dispatch(dest, hn, jnp.zeros((m_rows * ROW_TILE, LANES), F32))
    ys = _experts(xs, w_gu, w_down, moe_layer, block_e.astype(jnp.int32), n_used)
    return _combine(dest, h1, info, ys, ple_args, next_args, seq)


def _ple_kernel(h_ref, p_ref, g_ref, wg_ref, wu_ref, fg_ref, *rest, final, fuse_next):
    next_in, o_ref, next_out, _ = _split_fused_refs(rest, fuse_next)
    out = _ple_math(h_ref[...], p_ref[0], g_ref, wg_ref, wu_ref, fg_ref, final)
    o_ref[...] = out
    if fuse_next:
        _inproj_math(out, next_in, next_out)


def _ple(h, ple_args, next_args, seq):
    p_stack, g, wg_stack, wu_stack, layer, fg, final = ple_args
    t = h.shape[0]
    row = lambda w_: pl.BlockSpec((TM, w_), lambda i: (i, 0))
    const = lambda a, c: pl.BlockSpec((a, c), lambda i: (0, 0))
    in_specs = [row(D_MODEL), pl.BlockSpec((1, TM, PLE_DIM), lambda i: (layer, i, 0)),
                const(1, D_MODEL), _layer_spec(layer, D_MODEL, D_MODEL),
                _layer_spec(layer, PLE_DIM, D_MODEL), const(1, D_MODEL)]
    out_specs = [row(D_MODEL)]
    out_shape = [jax.ShapeDtypeStruct(h.shape, F32)]
    operands = [h, p_stack, g, wg_stack, wu_stack, fg]
    if next_args is not None:
        nin, nout, nshape = _inproj_specs(next_args[0], TM, t, seq)
        in_specs, out_specs, out_shape = in_specs + nin, out_specs + nout, out_shape + nshape
        operands += list(next_args[1:])
    res = pl.pallas_call(
        functools.partial(_ple_kernel, final=final, fuse_next=next_args is not None),
        grid=(t // TM,),
        in_specs=in_specs,
        out_specs=out_specs,
        out_shape=out_shape,
        compiler_params=_cparams(("parallel",)),
        name="ple",
    )(*operands)
    return res[0], res[1:]


def _rope_tables(seq):
    half = ROT_DIMS // 2
    pos = jnp.arange(seq, dtype=F32)
    inv_freq = ROPE_THETA ** (-jnp.arange(0, ROT_DIMS, 2, dtype=F32) / ROT_DIMS)
    ang = pos[:, None] * inv_freq[None, :]
    cos, sin = jnp.cos(ang), jnp.sin(ang)
    lane = np.arange(LANES) % DIFF_DH
    fidx = lane % half
    first = jnp.asarray(lane < half)[None, :]
    second = jnp.asarray((lane >= half) & (lane < ROT_DIMS))[None, :]
    ra = jnp.where(first | second, cos[:, fidx], 1.0)
    rb = jnp.where(second, sin[:, fidx], 0.0)
    rc = jnp.where(first, -sin[:, fidx], 0.0)
    return ra, rb, rc


def kernel(x, p, norm_mix_g, w_in, conv_w, conv_b, conv_ln_g, conv_ln_b, diff_lambda, diff_subln_g,
           gla_w_gate2, gla_b_gate, gla_norm_g, w_out, norm_ffn_g, ffn_w_gu, ffn_w_down, router_w,
           moe_w_gu, moe_w_down, ple_w_up, ple_w_gate, ple_norm_g, final_norm_g):
    bsz, seq, d = x.shape
    depth = w_in.shape[0]
    t = bsz * seq
    assert d == D_MODEL and seq % TM == 0 and seq % TQ == 0 and t % TMD == 0

    ropes = _rope_tables(seq)
    idx = np.arange(GLA_TRI)
    tri_chunk = jnp.asarray(((idx[:, None] >= idx[None, :])
                             & (idx[:, None] // CHUNK == idx[None, :] // CHUNK)).astype(np.float32)).astype(BF16)
    vi = np.arange(GLA_V)
    ki = np.arange(GLA_QK)
    gmat = jnp.asarray((vi[:, None] // GLA_DV == vi[None, :] // GLA_DV).astype(np.float32)
                       / GLA_DV).astype(BF16)
    head_mask = jnp.asarray((vi[:, None] // GLA_DV == ki[None, :] // GLA_DK).astype(np.float32))
    ti = np.arange(ROUTER_TM)
    tri_strict = jnp.asarray((ti[:, None] > ti[None, :]).astype(np.float32)).astype(BF16)

    w_in_b = jnp.pad(w_in, ((0, 0), (0, 0), (0, D_IN_PAD - D_IN))).astype(BF16)
    w_out_b = w_out.astype(BF16)
    ffn_gu_b = ffn_w_gu.astype(BF16)[:, None]
    ffn_down_b = ffn_w_down.astype(BF16)[:, None]
    moe_gu_b = moe_w_gu.astype(BF16)
    moe_down_b = moe_w_down.astype(BF16)
    ple_gate_b = ple_w_gate.astype(BF16)
    ple_up_b = ple_w_up.astype(BF16)
    p_rows = p.reshape(depth, t, PLE_DIM)

    def inproj_args(i):
        wg2 = jnp.pad(gla_w_gate2[i], ((0, LANES - GLA_GATE_RANK), (0, 0)))
        return (i, norm_mix_g[i][None, :], w_in_b, *ropes, wg2, gla_b_gate[i][None, :])

    h = x.reshape(t, D_MODEL)
    mixed = _inproj(h, inproj_args(0), seq)
    for i in range(depth):
        lam_init = 0.8 - 0.6 * math.exp(-0.3 * i)
        u, dq, dk, dv, gq, gk, gv, gr, la = mixed
        next_args = inproj_args(i + 1) if i + 1 < depth else None
        y_conv = _conv(u, conv_w[i], conv_b[i][None, :], conv_ln_g[i][None, :],
                       conv_ln_b[i][None, :], bsz, seq)
        y_diff = _attn(dq, dk, dv, diff_lambda[i], diff_subln_g[i][None, :], bsz, seq, lam_init)
        y_gla = _gla(gq, gk, gv, gr, la, tri_chunk, gmat, head_mask,
                     jnp.tile(gla_norm_g[i], GLA_HEADS)[None, :], bsz, seq)
        g_ffn = norm_ffn_g[i][None, :]
        j = i // 2
        ple_args = (p_rows, ple_norm_g[i][None, :], ple_gate_b, ple_up_b, i, final_norm_g[None, :],
                    i == depth - 1)
        if i % 2 == 0:
            h = _ffn(h, y_conv, y_diff, y_gla, w_out_b, i, g_ffn, ffn_gu_b, ffn_down_b, j)
            h, mixed = _ple(h, ple_args, next_args, seq)
        else:
            rw = jnp.pad(router_w[j], ((0, 0), (0, LANES - N_EXPERTS)))
            h, mixed = _moe(h, y_conv, y_diff, y_gla, w_out_b, i, g_ffn, rw, moe_gu_b, moe_down_b, j,
                            tri_strict, ple_args, next_args, seq)
    return h.reshape(bsz, seq, D_MODEL)
```

```python
import functools
import math

import jax
import jax.numpy as jnp
import numpy as np
from jax import lax
from jax.experimental import pallas as pl
from jax.experimental.pallas import tpu as pltpu

F32 = jnp.float32
BF16 = jnp.bfloat16

D_MODEL = 1024
CHUNK = 64
CONV_CH = 256
CONV_WIDTH = 31
DIFF_HEADS = 4
DIFF_DV = 128
DIFF_DH = 64
GLA_HEADS = 4
GLA_DV = 64
GLA_DK = 32
GLA_GATE_RANK = 16
GLA_TAU = 16.0
ROPE_THETA = 500000.0
ROT_DIMS = 16
D_FF = 2816
N_EXPERTS = 8
D_FF_EXPERT = 3584
PLE_DIM = 256
EPS = 1e-6
DIFF_W = 512
GLA_QK = GLA_HEADS * GLA_DK
GLA_V = GLA_HEADS * GLA_DV
D_IN = 2832
D_IN_PAD = 2944

LANES = 128
SUBLANES = 8
VMEM_LIMIT = 56 * 1024 * 1024
LOG2E = math.log2(math.e)

TM = 512
TK = 1024
TQ = TK
Q_SUB = 512
Q_PIECE = 256
CONV_TS = 512
CONV_HALO = 32
CONV_SH_ROWS = CONV_TS + CONV_HALO - SUBLANES
GLA_TG = 1024
GLA_TRI = 512
FFN_FB = 1408
MOE_BLK = 512
MOE_FB = 1792
ROW_TILE = D_MODEL // LANES
ROUTER_TM = 1024
ROUTER_SUB = 512
TMD = 512
TMC = 256
DMA_UNROLL = 8


def _cparams(sem):
    return pltpu.CompilerParams(dimension_semantics=sem, vmem_limit_bytes=VMEM_LIMIT)


def _rms(x, g):
    ms = jnp.mean(x * x, axis=-1, keepdims=True)
    return x * lax.rsqrt(ms + EPS) * g


def _sigmoid(x):
    return 1.0 / (1.0 + jnp.exp(-x))


def _silu(x):
    return x * _sigmoid(x)


def _split_bf16(x):
    hi = x.astype(BF16)
    return hi, (x - hi.astype(F32)).astype(BF16)


def _dot_f32(a, b):
    a_hi, a_lo = _split_bf16(a)
    b_hi, b_lo = _split_bf16(b)
    dot = functools.partial(jnp.dot, preferred_element_type=F32)
    return dot(a_hi, b_hi) + (dot(a_hi, b_lo) + dot(a_lo, b_hi))


def _rope(x, ra, rb, rc):
    outs = []
    for c in range(x.shape[1] // LANES):
        xc = x[:, c * LANES:(c + 1) * LANES]
        outs.append(xc * ra + pltpu.roll(xc, ROT_DIMS // 2, 1) * rb
                    + pltpu.roll(xc, LANES - ROT_DIMS // 2, 1) * rc)
    return jnp.concatenate(outs, axis=1)


N_INPROJ_IN = 7
N_INPROJ_OUT = 9


def _inproj_math(x, in_refs, out_refs):
    g_ref, w_ref, ra_ref, rb_ref, rc_ref, wg2_ref, bg_ref = in_refs
    u_ref, q_ref, k_ref, v_ref, gq_ref, gk_ref, gv_ref, gr_ref, la_ref = out_refs
    hn = _rms(x, g_ref[...]).astype(BF16)

    def proj(a, b):
        return jnp.dot(hn, w_ref[0, :, a:b], preferred_element_type=F32)

    ra, rb, rc = ra_ref[...], rb_ref[...], rc_ref[...]
    u_ref[...] = proj(0, 256) * _sigmoid(proj(256, 512))
    q_ref[...] = (_rope(proj(512, 1024), ra, rb, rc) * (LOG2E * DIFF_DH ** -0.5)).astype(BF16)
    k_ref[...] = _rope(proj(1024, 1536), ra, rb, rc).astype(BF16)
    v_ref[...] = proj(1536, 2048).astype(BF16)
    gq_ref[...] = proj(2048, 2176) * (GLA_DK ** -0.5)
    gk_ref[...] = proj(2176, 2304)
    gv_ref[...] = proj(2304, 2560)
    gr_ref[...] = _silu(proj(2560, 2816))
    gz = proj(2816, D_IN_PAD)
    ga = _dot_f32(gz, wg2_ref[...]) + bg_ref[...]
    la_ref[...] = (jnp.minimum(ga, 0.0) - jnp.log(1.0 + jnp.exp(-jnp.abs(ga)))) * (1.0 / GLA_TAU)


def _inproj_kernel(h_ref, *refs):
    _inproj_math(h_ref[...], refs[:N_INPROJ_IN], refs[N_INPROJ_IN:])


def _layer_spec(layer, a, b):
    return pl.BlockSpec((1, a, b), lambda i: (layer, 0, 0))


def _inproj_specs(layer, tile, t, seq):
    nseq = seq // tile
    const = lambda a, b: pl.BlockSpec((a, b), lambda i: (0, 0))
    rope_spec = pl.BlockSpec((tile, LANES), lambda i: (i % nseq, 0))
    outs = [(CONV_CH, F32), (DIFF_W, BF16), (DIFF_W, BF16), (DIFF_W, BF16),
            (GLA_QK, F32), (GLA_QK, F32), (GLA_V, F32), (GLA_V, F32), (GLA_QK, F32)]
    in_specs = [const(1, D_MODEL), _layer_spec(layer, D_MODEL, D_IN_PAD),
                rope_spec, rope_spec, rope_spec, const(LANES, LANES), const(1, LANES)]
    out_specs = [pl.BlockSpec((tile, w_), lambda i: (i, 0)) for w_, _ in outs]
    out_shape = [jax.ShapeDtypeStruct((t, w_), dt) for w_, dt in outs]
    return in_specs, out_specs, out_shape


def _inproj(h, inproj_args, seq):
    t = h.shape[0]
    in_specs, out_specs, out_shape = _inproj_specs(inproj_args[0], TM, t, seq)
    return pl.pallas_call(
        _inproj_kernel,
        grid=(t // TM,),
        in_specs=[pl.BlockSpec((TM, D_MODEL), lambda i: (i, 0))] + in_specs,
        out_specs=out_specs,
        out_shape=out_shape,
        compiler_params=_cparams(("parallel",)),
        name="inproj",
    )(h, *inproj_args[1:])


def _conv_kernel(u_ref, w_ref, b_ref, lg_ref, lb_ref, o_ref, win, shifted):
    @pl.when(pl.program_id(1) == 0)
    def _():
        win[0:CONV_HALO, :] = jnp.zeros((CONV_HALO, CONV_CH), F32)

    win[CONV_HALO:, :] = u_ref[...]
    for r in range(1, SUBLANES):
        shifted[r - 1] = win[r:r + CONV_SH_ROWS, :]
    rows = 64
    base = CONV_HALO - (CONV_WIDTH - 1)
    for r0 in range(0, CONV_TS, rows):
        acc = jnp.zeros((rows, CONV_CH), F32) + b_ref[...]
        for j in range(CONV_WIDTH):
            res = (base + j) % SUBLANES
            start = r0 + (base + j) - res
            if res == 0:
                tap = win[start:start + rows, :]
            else:
                tap = shifted[res - 1, start:start + rows, :]
            acc = acc + tap * w_ref[j:j + 1, :]
        mu = jnp.mean(acc, axis=-1, keepdims=True)
        xc = acc - mu
        var = jnp.mean(xc * xc, axis=-1, keepdims=True)
        y = xc * lax.rsqrt(var + EPS) * lg_ref[...] + lb_ref[...]
        o_ref[r0:r0 + rows, :] = _silu(y).astype(BF16)
    win[0:CONV_HALO, :] = win[CONV_TS:CONV_TS + CONV_HALO, :]


def _conv(u, w, b, lg, lb, bsz, seq):
    nt = seq // CONV_TS
    const = lambda a, c: pl.BlockSpec((a, c), lambda bi, ti: (0, 0))
    return pl.pallas_call(
        _conv_kernel,
        grid=(bsz, nt),
        in_specs=[pl.BlockSpec((CONV_TS, CONV_CH), lambda bi, ti: (bi * nt + ti, 0)),
                  const(CONV_WIDTH, CONV_CH), const(1, CONV_CH), const(1, CONV_CH), const(1, CONV_CH)],
        out_specs=pl.BlockSpec((CONV_TS, CONV_CH), lambda bi, ti: (bi * nt + ti, 0)),
        out_shape=jax.ShapeDtypeStruct(u.shape, BF16),
        scratch_shapes=[pltpu.VMEM((CONV_TS + CONV_HALO, CONV_CH), F32),
                        pltpu.VMEM((SUBLANES - 1, CONV_SH_ROWS, CONV_CH), F32)],
        compiler_params=_cparams(("arbitrary", "arbitrary")),
        name="conv",
    )(u, w, b, lg, lb)


def _attn_kernel(qt_ref, kt_ref, lam_ref, sg_ref, q_ref, k_ref, v_ref, o_ref, m_scr, acc_scr,
                 *, lam_init):
    step = pl.program_id(1)
    qi = qt_ref[step]
    kj = kt_ref[step]

    @pl.when(kj == 0)
    def _():
        m_scr[...] = jnp.full(m_scr.shape, -jnp.inf, F32)
        acc_scr[...] = jnp.zeros(acc_scr.shape, F32)

    def update(sub, masked):
        if masked:
            pieces = [(sub * Q_SUB + r, Q_PIECE, sub * Q_SUB + r + Q_PIECE) for r in range(0, Q_SUB, Q_PIECE)]
        else:
            pieces = [(sub * Q_SUB, Q_SUB, TK)]
        for r0, nr, nk in pieces:
            rows = slice(r0, r0 + nr)
            if masked:
                rq = (lax.broadcasted_iota(jnp.int32, (nr, nk), 0) + r0) // CHUNK
                ck = lax.broadcasted_iota(jnp.int32, (nr, nk), 1) // CHUNK
                allowed = ck <= rq
            for h in range(DIFF_HEADS):
                hs = slice(h * LANES, (h + 1) * LANES)
                q = q_ref[rows, hs]
                k = k_ref[0:nk, hs]
                v = v_ref[0:nk, hs]
                vext = jnp.concatenate([v, jnp.ones_like(v)], axis=1)
                lane = lax.broadcasted_iota(jnp.int32, q.shape, 1)
                for c in range(2):
                    sel = (lane < DIFF_DH) if c == 0 else (lane >= DIFF_DH)
                    qc = jnp.where(sel, q, jnp.zeros_like(q))
                    s = lax.dot_general(qc, k, (((1,), (1,)), ((), ())), preferred_element_type=F32)
                    if masked:
                        s = jnp.where(allowed, s, -jnp.inf)
                    m_old = m_scr[2 * h + c, rows]
                    m_new = jnp.maximum(m_old, jnp.max(s, axis=-1, keepdims=True))
                    alpha = jnp.exp2(m_old - m_new)
                    p = jnp.exp2(s - jnp.tile(m_new, (1, nk // LANES)))
                    pv = jnp.dot(p.astype(BF16), vext, preferred_element_type=F32)
                    acc_scr[2 * h + c, rows] = jnp.tile(alpha, (1, 2)) * acc_scr[2 * h + c, rows] + pv
                    m_scr[2 * h + c, rows] = m_new

    @pl.when(kj < qi)
    def _():
        for sub in range(TQ // Q_SUB):
            update(sub, False)

    @pl.when(kj == qi)
    def _():
        for sub in range(TQ // Q_SUB):
            update(sub, True)
        lp = lam_ref[...]
        lam = (jnp.exp(jnp.sum(lp[0:1] * lp[1:2], axis=-1, keepdims=True))
               - jnp.exp(jnp.sum(lp[2:3] * lp[3:4], axis=-1, keepdims=True)) + lam_init)
        for h in range(DIFF_HEADS):
            a0 = acc_scr[2 * h]
            a1 = acc_scr[2 * h + 1]
            o = a0[:, :DIFF_DV] / a0[:, DIFF_DV:] - lam * (a1[:, :DIFF_DV] / a1[:, DIFF_DV:])
            o_ref[:, h * LANES:(h + 1) * LANES] = (_rms(o, sg_ref[...]) * (1.0 - lam_init)).astype(BF16)


def _attn(q, k, v, lam_p, sg, bsz, seq, lam_init):
    nq = seq // TQ
    nk = seq // TK
    pairs = [(a, b) for a in range(nq) for b in range(a + 1)]
    qt = jnp.asarray(np.array([a for a, _ in pairs], np.int32))
    kt = jnp.asarray(np.array([b for _, b in pairs], np.int32))
    qspec = pl.BlockSpec((TQ, DIFF_W), lambda b, s, qt_, kt_: (b * nq + qt_[s], 0))
    kspec = pl.BlockSpec((TK, DIFF_W), lambda b, s, qt_, kt_: (b * nk + kt_[s], 0))
    const = lambda a, c: pl.BlockSpec((a, c), lambda b, s, qt_, kt_: (0, 0))
    grid_spec = pltpu.PrefetchScalarGridSpec(
        num_scalar_prefetch=2,
        grid=(bsz, len(pairs)),
        in_specs=[const(4, DIFF_DH), const(1, DIFF_DV), qspec, kspec, kspec],
        out_specs=qspec,
        scratch_shapes=[pltpu.VMEM((2 * DIFF_HEADS, TQ, LANES), F32),
                        pltpu.VMEM((2 * DIFF_HEADS, TQ, 2 * DIFF_DV), F32)],
    )
    return pl.pallas_call(
        functools.partial(_attn_kernel, lam_init=lam_init),
        grid_spec=grid_spec,
        out_shape=jax.ShapeDtypeStruct(q.shape, BF16),
        compiler_params=_cparams(("parallel", "arbitrary")),
        name="diffattn",
    )(qt, kt, lam_p, sg, q, k, v)


def _gla_kernel(q_ref, k_ref, v_ref, r_ref, la_ref, tri_ref, gmat_ref, mask_ref, ng_ref, o_ref,
                st, kv_scr, st_scr, o_scr):
    @pl.when(pl.program_id(1) == 0)
    def _():
        st[...] = jnp.zeros(st.shape, F32)

    cums = []
    for r0 in range(0, GLA_TG, GLA_TRI):
        la_hi, la_lo = _split_bf16(la_ref[r0:r0 + GLA_TRI, :])
        cums.append(jnp.dot(tri_ref[...], la_hi, preferred_element_type=F32)
                    + jnp.dot(tri_ref[...], la_lo, preferred_element_type=F32))
    cum = jnp.concatenate(cums, axis=0)
    same_head = mask_ref[...] > 0.0
    nch = GLA_TG // CHUNK
    tots = []
    for c in range(nch):
        sl = slice(c * CHUNK, (c + 1) * CHUNK)
        cum_c = cum[sl]
        tot = cum_c[CHUNK - 1:CHUNK]
        tots.append(tot)
        kdec = (k_ref[sl, :] * jnp.exp(tot - cum_c)).astype(BF16)
        v_t = v_ref[sl, :].T.astype(BF16)
        kv_scr[c] = jnp.dot(v_t, kdec, preferred_element_type=F32)
    state = st[...]
    for c in range(nch):
        state = state * jnp.exp(tots[c]) + jnp.where(same_head, kv_scr[c], 0.0)
        st_scr[c] = state.astype(BF16)
    st[...] = state
    for c in range(nch):
        sl = slice(c * CHUNK, (c + 1) * CHUNK)
        o_scr[sl, :] = lax.dot_general(q_ref[sl, :].astype(BF16), st_scr[c], (((1,), (1,)), ((), ())),
                                       preferred_element_type=F32)
    o = o_scr[...]
    ms = jnp.dot((o * o).astype(BF16), gmat_ref[...], preferred_element_type=F32)
    o_ref[...] = (o * lax.rsqrt(ms + EPS) * ng_ref[...] * r_ref[...]).astype(BF16)


def _gla(gq, gk, gv, gr, la, tri, gmat, head_mask, ng, bsz, seq):
    nt = seq // GLA_TG
    row = lambda w_: pl.BlockSpec((GLA_TG, w_), lambda bi, ti: (bi * nt + ti, 0))
    const = lambda a, c: pl.BlockSpec((a, c), lambda bi, ti: (0, 0))
    return pl.pallas_call(
        _gla_kernel,
        grid=(bsz, nt),
        in_specs=[row(GLA_QK), row(GLA_QK), row(GLA_V), row(GLA_V), row(GLA_QK),
                  const(GLA_TRI, GLA_TRI), const(GLA_V, GLA_V), const(GLA_V, GLA_QK), const(1, GLA_V)],
        out_specs=row(GLA_V),
        out_shape=jax.ShapeDtypeStruct(gv.shape, BF16),
        scratch_shapes=[pltpu.VMEM((GLA_V, GLA_QK), F32),
                        pltpu.VMEM((GLA_TG // CHUNK, GLA_V, GLA_QK), F32),
                        pltpu.VMEM((GLA_TG // CHUNK, GLA_V, GLA_QK), BF16),
                        pltpu.VMEM((GLA_TG, GLA_V), F32)],
        compiler_params=_cparams(("arbitrary", "arbitrary")),
        name="gla",
    )(gq, gk, gv, gr, la, tri, gmat, head_mask, ng)


def _mix_residual(h_ref, yc_ref, yd_ref, yg_ref, wo_ref, rows=slice(None)):
    acc = jnp.dot(yc_ref[rows, :], wo_ref[0, 0:256, :], preferred_element_type=F32)
    acc = acc + jnp.dot(yd_ref[rows, :], wo_ref[0, 256:768, :], preferred_element_type=F32)
    acc = acc + jnp.dot(yg_ref[rows, :], wo_ref[0, 768:1024, :], preferred_element_type=F32)
    return h_ref[rows, :] + acc


def _swiglu_step(xb, wg_ref, wu_ref, wd_ref, acc):
    x = xb[...]
    g = jnp.dot(x, wg_ref[0, 0], preferred_element_type=F32)
    u = jnp.dot(x, wu_ref[0, 0], preferred_element_type=F32)
    a = (_silu(g) * u).astype(BF16)
    acc[...] += jnp.dot(a, wd_ref[0, 0], preferred_element_type=F32)


def _ffn_kernel(h_ref, yc_ref, yd_ref, yg_ref, wo_ref, g_ref, wg_ref, wu_ref, wd_ref, o_ref,
                h1, xb, acc):
    j = pl.program_id(1)

    @pl.when(j == 0)
    def _():
        x = _mix_residual(h_ref, yc_ref, yd_ref, yg_ref, wo_ref)
        h1[...] = x
        xb[...] = _rms(x, g_ref[...]).astype(BF16)
        acc[...] = jnp.zeros(acc.shape, F32)

    _swiglu_step(xb, wg_ref, wu_ref, wd_ref, acc)

    @pl.when(j == pl.num_programs(1) - 1)
    def _():
        o_ref[...] = h1[...] + acc[...]


def _ffn(h, yc, yd, yg, wo_stack, layer, g, w_gu, w_down, ffn_layer):
    t = h.shape[0]
    nj = D_FF // FFN_FB
    row = lambda w_: pl.BlockSpec((TM, w_), lambda i, j: (i, 0))
    return pl.pallas_call(
        _ffn_kernel,
        grid=(t // TM, nj),
        in_specs=[row(D_MODEL), row(CONV_CH), row(DIFF_W), row(GLA_V),
                  pl.BlockSpec((1, D_MODEL, D_MODEL), lambda i, j: (layer, 0, 0)),
                  pl.BlockSpec((1, D_MODEL), lambda i, j: (0, 0)),
                  pl.BlockSpec((1, 1, D_MODEL, FFN_FB), lambda i, j: (ffn_layer, 0, 0, j)),
                  pl.BlockSpec((1, 1, D_MODEL, FFN_FB), lambda i, j: (ffn_layer, 0, 0, j + nj)),
                  pl.BlockSpec((1, 1, FFN_FB, D_MODEL), lambda i, j: (ffn_layer, 0, j, 0))],
        out_specs=row(D_MODEL),
        out_shape=jax.ShapeDtypeStruct((t, D_MODEL), F32),
        scratch_shapes=[pltpu.VMEM((TM, D_MODEL), F32), pltpu.VMEM((TM, D_MODEL), BF16),
                        pltpu.VMEM((TM, D_MODEL), F32)],
        compiler_params=_cparams(("parallel", "arbitrary")),
        name="ffn",
    )(h, yc, yd, yg, wo_stack, g, w_gu, w_gu, w_down)


def _tile_rows_load(ref, n):
    return jnp.concatenate([ref[pl.ds(s, n, stride=ROW_TILE), :] for s in range(ROW_TILE)], axis=1)


def _tile_rows_store(ref, x, n, row0=0):
    for s in range(ROW_TILE):
        ref[pl.ds(row0 * ROW_TILE + s, n, stride=ROW_TILE), :] = x[:, s * LANES:(s + 1) * LANES]


def _experts_kernel(be_ref, nu_ref, x_ref, wg_ref, wu_ref, wd_ref, o_ref, xb, acc):
    i = pl.program_id(0)
    j = pl.program_id(1)
    used = i < nu_ref[0]

    @pl.when(used)
    def _():
        @pl.when(j == 0)
        def _():
            xb[...] = _tile_rows_load(x_ref, MOE_BLK).astype(BF16)
            acc[...] = jnp.zeros(acc.shape, F32)

        _swiglu_step(xb, wg_ref, wu_ref, wd_ref, acc)

    @pl.when(j == pl.num_programs(1) - 1)
    def _():
        _tile_rows_store(o_ref, jnp.where(used, acc[...], 0.0), MOE_BLK)


def _experts(xs, w_gu, w_down, layer, block_e, n_used):
    nb = xs.shape[0] // (MOE_BLK * ROW_TILE)
    nj = D_FF_EXPERT // MOE_FB

    def last(i, nu):
        return jnp.minimum(i, nu[0] - 1)

    def hid(i, j, nu):
        return jnp.where(i < nu[0], j, nj - 1)

    xspec = pl.BlockSpec((MOE_BLK * ROW_TILE, LANES), lambda i, j, be, nu: (last(i, nu), 0))
    wg = pl.BlockSpec((1, 1, D_MODEL, MOE_FB),
                      lambda i, j, be, nu: (layer, be[last(i, nu)], 0, hid(i, j, nu)))
    wu = pl.BlockSpec((1, 1, D_MODEL, MOE_FB),
                      lambda i, j, be, nu: (layer, be[last(i, nu)], 0, hid(i, j, nu) + nj))
    wd = pl.BlockSpec((1, 1, MOE_FB, D_MODEL),
                      lambda i, j, be, nu: (layer, be[last(i, nu)], hid(i, j, nu), 0))
    grid_spec = pltpu.PrefetchScalarGridSpec(
        num_scalar_prefetch=2,
        grid=(nb, nj),
        in_specs=[xspec, wg, wu, wd],
        out_specs=pl.BlockSpec((MOE_BLK * ROW_TILE, LANES), lambda i, j, be, nu: (i, 0)),
        scratch_shapes=[pltpu.VMEM((MOE_BLK, D_MODEL), BF16), pltpu.VMEM((MOE_BLK, D_MODEL), F32)],
    )
    return pl.pallas_call(
        _experts_kernel,
        grid_spec=grid_spec,
        out_shape=jax.ShapeDtypeStruct(xs.shape, F32),
        compiler_params=_cparams(("arbitrary", "arbitrary")),
        name="experts",
    )(block_e, n_used, xs, w_gu, w_gu, w_down)


def _router_kernel(h_ref, yc_ref, yd_ref, yg_ref, wo_ref, g_ref, rw_ref, tri_ref,
                   h1_ref, hn_ref, info_ref, cnt_ref, carry):
    @pl.when(pl.program_id(0) == 0)
    def _():
        carry[...] = jnp.zeros(carry.shape, F32)

    subs = [slice(r0, r0 + ROUTER_SUB) for r0 in range(0, ROUTER_TM, ROUTER_SUB)]
    each = lambda f, *cols: [f(*a) for a in zip(*cols)]
    rmax = lambda x: jnp.max(x, axis=-1, keepdims=True)
    rmin = lambda x: jnp.min(x, axis=-1, keepdims=True)
    rsum = lambda x: jnp.sum(x, axis=-1, keepdims=True)
    lane = lax.broadcasted_iota(jnp.int32, (ROUTER_SUB, LANES), 1)

    h1 = each(lambda rows: _mix_residual(h_ref, yc_ref, yd_ref, yg_ref, wo_ref, rows), subs)
    for rows, x in zip(subs, h1):
        h1_ref[rows, :] = x
    hn = each(lambda x: _rms(x, g_ref[...]), h1)
    for rows, x in zip(subs, hn):
        _tile_rows_store(hn_ref, x, ROUTER_SUB, rows.start)
    logits = each(lambda x: _dot_f32(x, rw_ref[...]), hn)
    lg = each(lambda x: jnp.where(lane < N_EXPERTS, x, -jnp.inf), logits)
    v1 = each(rmax, lg)
    i1 = each(lambda x, v: rmin(jnp.where(x == v, lane, LANES)), lg, v1)
    lg2 = each(lambda x, i: jnp.where(lane == i, -jnp.inf, x), lg, i1)
    v2 = each(rmax, lg2)
    i2 = each(lambda x, v: rmin(jnp.where(x == v, lane, LANES)), lg2, v2)
    e2 = each(lambda a, b: jnp.exp(b - a), v1, v2)
    g1 = each(lambda e: 1.0 / (1.0 + e), e2)
    g2 = each(lambda e: e / (1.0 + e), e2)
    pick1 = each(lambda i: lane == i, i1)
    pick2 = each(lambda i: lane == i, i2)
    onehot = each(lambda a, b: jnp.where(a | b, 1.0, 0.0), pick1, pick2)
    local = each(lambda o: jnp.dot(tri_ref[...], o.astype(BF16), preferred_element_type=F32), onehot)
    count = carry[...]
    before = []
    for loc, o in zip(local, onehot):
        before.append(loc + count)
        count = count + jnp.sum(o, axis=0, keepdims=True)
    r1 = each(lambda p_, b: rsum(jnp.where(p_, b, 0.0)), pick1, before)
    r2 = each(lambda p_, b: rsum(jnp.where(p_, b, 0.0)), pick2, before)
    for k, rows in enumerate(subs):
        info = jnp.where(lane == 0, i1[k].astype(F32), 0.0)
        info = jnp.where(lane == 1, i2[k].astype(F32), info)
        info = jnp.where(lane == 2, r1[k], info)
        info = jnp.where(lane == 3, r2[k], info)
        info = jnp.where(lane == 4, g1[k], info)
        info = jnp.where(lane == 5, g2[k], info)
        info_ref[rows, :] = info
    carry[...] = count
    cnt_ref[...] = jnp.broadcast_to(count, cnt_ref.shape)


def _router(h, yc, yd, yg, wo_stack, layer, g, rw, tri):
    t = h.shape[0]
    tm = ROUTER_TM
    row = lambda w_: pl.BlockSpec((tm, w_), lambda i: (i, 0))
    const = lambda a, c: pl.BlockSpec((a, c), lambda i: (0, 0))
    return pl.pallas_call(
        _router_kernel,
        grid=(t // tm,),
        in_specs=[row(D_MODEL), row(CONV_CH), row(DIFF_W), row(GLA_V),
                  _layer_spec(layer, D_MODEL, D_MODEL),
                  const(1, D_MODEL), const(D_MODEL, LANES), const(ROUTER_SUB, ROUTER_SUB)],
        out_specs=[row(D_MODEL), pl.BlockSpec((tm * ROW_TILE, LANES), lambda i: (i, 0)), row(LANES),
                   const(SUBLANES, LANES)],
        out_shape=[jax.ShapeDtypeStruct((t, D_MODEL), F32),
                   jax.ShapeDtypeStruct((t * ROW_TILE, LANES), F32),
                   jax.ShapeDtypeStruct((t, LANES), F32),
                   jax.ShapeDtypeStruct((SUBLANES, LANES), F32)],
        scratch_shapes=[pltpu.VMEM((1, LANES), F32)],
        compiler_params=_cparams(("arbitrary",)),
        name="router",
    )(h, yc, yd, yg, wo_stack, g, rw, tri)


def _row_copy(src, dst, sem):
    return pltpu.make_async_copy(src, dst, sem)


def _tile_at(ref, start):
    if not isinstance(start, int):
        start = pl.multiple_of(start, ROW_TILE)
    return ref.at[pl.ds(start, ROW_TILE), :]


def _dispatch_kernel(dest_ref, hn_ref, xs_in_ref, xs_ref, sem):
    del xs_in_ref

    def issue(t, carry):
        src = _tile_at(hn_ref, t * ROW_TILE)
        _row_copy(src, _tile_at(xs_ref, dest_ref[0, 0, 2 * t]), sem).start(priority=0)
        _row_copy(src, _tile_at(xs_ref, dest_ref[0, 0, 2 * t + 1]), sem).start(priority=1)
        return carry

    lax.fori_loop(0, TMD, issue, 0, unroll=DMA_UNROLL)

    def drain(t, carry):
        _row_copy(_tile_at(hn_ref, 0), _tile_at(xs_ref, 0), sem).wait()
        _row_copy(_tile_at(hn_ref, 0), _tile_at(xs_ref, 0), sem).wait()
        return carry

    lax.fori_loop(0, TMD, drain, 0, unroll=DMA_UNROLL)


def _dispatch(dest, hn, xs_zero):
    t = hn.shape[0] // ROW_TILE
    return pl.pallas_call(
        _dispatch_kernel,
        grid=(t // TMD,),
        in_specs=[pl.BlockSpec((1, 1, 2 * TMD), lambda i: (i, 0, 0), memory_space=pltpu.SMEM),
                  pl.BlockSpec((TMD * ROW_TILE, LANES), lambda i: (i, 0)),
                  pl.BlockSpec(memory_space=pl.ANY)],
        out_specs=pl.BlockSpec(memory_space=pl.ANY),
        out_shape=jax.ShapeDtypeStruct(xs_zero.shape, F32),
        scratch_shapes=[pltpu.SemaphoreType.DMA(())],
        input_output_aliases={2: 0},
        compiler_params=_cparams(("arbitrary",)),
        name="dispatch",
    )(dest, hn, xs_zero)


def _ple_math(h, p_blk, g_ref, wg_ref, wu_ref, fg_ref, final):
    hn = _rms(h, g_ref[...]).astype(BF16)
    gate = _sigmoid(jnp.dot(hn, wg_ref[0], preferred_element_type=F32))
    up = jnp.dot(p_blk.astype(BF16), wu_ref[0], preferred_element_type=F32)
    out = h + gate * up
    if final:
        out = _rms(out, fg_ref[...])
    return out


def _combine_kernel(dcur_ref, dnext_ref, h_ref, info_ref, p_ref, g_ref, wg_ref, wu_ref, fg_ref, ys_ref,
                    *rest, final, fuse_next):
    next_in, o_ref, next_out, (buf, sems) = _split_fused_refs(rest, fuse_next)
    i = pl.program_id(0)
    slot = i % 2

    def gather(dref, slot_):
        def issue(t, carry):
            for pick in range(2):
                _row_copy(_tile_at(ys_ref, dref[0, 0, 2 * t + pick]),
                          _tile_at(buf.at[slot_, pick], t * ROW_TILE), sems.at[slot_]).start(priority=pick)
            return carry

        lax.fori_loop(0, TMC, issue, 0, unroll=DMA_UNROLL)

    @pl.when(i == 0)
    def _():
        gather(dcur_ref, 0)

    @pl.when(i + 1 < pl.num_programs(0))
    def _():
        gather(dnext_ref, 1 - slot)

    def drain(t, carry):
        for pick in range(2):
            _row_copy(_tile_at(ys_ref, 0), _tile_at(buf.at[slot, pick], 0), sems.at[slot]).wait()
        return carry

    lax.fori_loop(0, TMC, drain, 0, unroll=DMA_UNROLL)
    g1 = info_ref[:, 4:5]
    g2 = info_ref[:, 5:6]
    y1 = _tile_rows_load(buf.at[slot, 0], TMC)
    y2 = _tile_rows_load(buf.at[slot, 1], TMC)
    h2 = h_ref[...] + (g1 * y1 + g2 * y2)
    out = _ple_math(h2, p_ref[0], g_ref, wg_ref, wu_ref, fg_ref, final)
    o_ref[...] = out
    if fuse_next:
        _inproj_math(out, next_in, next_out)


def _split_fused_refs(rest, fuse_next):
    if not fuse_next:
        return (), rest[0], (), rest[1:]
    n_out = N_INPROJ_IN + 1 + N_INPROJ_OUT
    return rest[:N_INPROJ_IN], rest[N_INPROJ_IN], rest[N_INPROJ_IN + 1:n_out], rest[n_out:]


def _combine(dest, h, info, ys, ple_args, next_args, seq):
    p_stack, g, wg_stack, wu_stack, layer, fg, final = ple_args
    t = h.shape[0]
    n = t // TMC
    dest = dest.reshape(n, 1, 2 * TMC)
    row = lambda w_: pl.BlockSpec((TMC, w_), lambda i: (i, 0))
    const = lambda a, c: pl.BlockSpec((a, c), lambda i: (0, 0))
    in_specs = [pl.BlockSpec((1, 1, 2 * TMC), lambda i: (i, 0, 0), memory_space=pltpu.SMEM),
                pl.BlockSpec((1, 1, 2 * TMC), lambda i: (jnp.minimum(i + 1, n - 1), 0, 0),
                             memory_space=pltpu.SMEM),
                row(D_MODEL), row(LANES),
                pl.BlockSpec((1, TMC, PLE_DIM), lambda i: (layer, i, 0)),
                const(1, D_MODEL), _layer_spec(layer, D_MODEL, D_MODEL),
                _layer_spec(layer, PLE_DIM, D_MODEL), const(1, D_MODEL),
                pl.BlockSpec(memory_space=pl.ANY)]
    out_specs = [row(D_MODEL)]
    out_shape = [jax.ShapeDtypeStruct(h.shape, F32)]
    operands = [dest, dest, h, info, p_stack, g, wg_stack, wu_stack, fg, ys]
    if next_args is not None:
        nin, nout, nshape = _inproj_specs(next_args[0], TMC, t, seq)
        in_specs, out_specs, out_shape = in_specs + nin, out_specs + nout, out_shape + nshape
        operands += list(next_args[1:])
    res = pl.pallas_call(
        functools.partial(_combine_kernel, final=final, fuse_next=next_args is not None),
        grid=(n,),
        in_specs=in_specs,
        out_specs=out_specs,
        out_shape=out_shape,
        scratch_shapes=[pltpu.VMEM((2, 2, TMC * ROW_TILE, LANES), F32), pltpu.SemaphoreType.DMA((2,))],
        compiler_params=_cparams(("arbitrary",)),
        name="combine",
    )(*operands)
    return res[0], res[1:]


def _moe(h, yc, yd, yg, wo_stack, layer, g, rw, w_gu, w_down, moe_layer, tri_strict, ple_args,
         next_args, seq):
    t = h.shape[0]
    h1, hn, info, cnt = _router(h, yc, yd, yg, wo_stack, layer, g, rw, tri_strict)
    counts = cnt[0, :N_EXPERTS].astype(jnp.int32)
    padded = ((counts + MOE_BLK - 1) // MOE_BLK) * MOE_BLK
    pend = jnp.cumsum(padded)
    pstart = pend - padded
    e = info[:, 0:2].astype(jnp.int32)
    r = info[:, 2:4].astype(jnp.int32)
    onehot = e[:, :, None] == jnp.arange(N_EXPERTS, dtype=jnp.int32)[None, None, :]
    dest = r + jnp.sum(jnp.where(onehot, pstart[None, None, :], 0), axis=-1)
    dest = (dest * ROW_TILE).reshape(t // TMD, 1, 2 * TMD)
    m_rows = 2 * t + N_EXPERTS * MOE_BLK
    nb = m_rows // MOE_BLK
    blk_start = jnp.arange(nb, dtype=jnp.int32) * MOE_BLK
    block_e = jnp.minimum(jnp.sum(blk_start[:, None] >= pend[None, :], axis=-1), N_EXPERTS - 1)
    n_used = (pend[-1:] // MOE_BLK).astype(jnp.int32)

    xs = _dispatch(dest, hn, jnp.zeros((m_rows * ROW_TILE, LANES), F32))
    ys = _experts(xs, w_gu, w_down, moe_layer, block_e.astype(jnp.int32), n_used)
    return _combine(dest, h1, info, ys, ple_args, next_args, seq)


def _ple_kernel(h_ref, p_ref, g_ref, wg_ref, wu_ref, fg_ref, *rest, final, fuse_next):
    next_in, o_ref, next_out, _ = _split_fused_refs(rest, fuse_next)
    out = _ple_math(h_ref[...], p_ref[0], g_ref, wg_ref, wu_ref, fg_ref, final)
    o_ref[...] = out
    if fuse_next:
        _inproj_math(out, next_in, next_out)


def _ple(h, ple_args, next_args, seq):
    p_stack, g, wg_stack, wu_stack, layer, fg, final = ple_args
    t = h.shape[0]
    row = lambda w_: pl.BlockSpec((TM, w_), lambda i: (i, 0))
    const = lambda a, c: pl.BlockSpec((a, c), lambda i: (0, 0))
    in_specs = [row(D_MODEL), pl.BlockSpec((1, TM, PLE_DIM), lambda i: (layer, i, 0)),
                const(1, D_MODEL), _layer_spec(layer, D_MODEL, D_MODEL),
                _layer_spec(layer, PLE_DIM, D_MODEL), const(1, D_MODEL)]
    out_specs = [row(D_MODEL)]
    out_shape = [jax.ShapeDtypeStruct(h.shape, F32)]
    operands = [h, p_stack, g, wg_stack, wu_stack, fg]
    if next_args is not None:
        nin, nout, nshape = _inproj_specs(next_args[0], TM, t, seq)
        in_specs, out_specs, out_shape = in_specs + nin, out_specs + nout, out_shape + nshape
        operands += list(next_args[1:])
    res = pl.pallas_call(
        functools.partial(_ple_kernel, final=final, fuse_next=next_args is not None),
        grid=(t // TM,),
        in_specs=in_specs,
        out_specs=out_specs,
        out_shape=out_shape,
        compiler_params=_cparams(("parallel",)),
        name="ple",
    )(*operands)
    return res[0], res[1:]


def _rope_tables(seq):
    half = ROT_DIMS // 2
    pos = jnp.arange(seq, dtype=F32)
    inv_freq = ROPE_THETA ** (-jnp.arange(0, ROT_DIMS, 2, dtype=F32) / ROT_DIMS)
    ang = pos[:, None] * inv_freq[None, :]
    cos, sin = jnp.cos(ang), jnp.sin(ang)
    lane = np.arange(LANES) % DIFF_DH
    fidx = lane % half
    first = jnp.asarray(lane < half)[None, :]
    second = jnp.asarray((lane >= half) & (lane < ROT_DIMS))[None, :]
    ra = jnp.where(first | second, cos[:, fidx], 1.0)
    rb = jnp.where(second, sin[:, fidx], 0.0)
    rc = jnp.where(first, -sin[:, fidx], 0.0)
    return ra, rb, rc


def kernel(x, p, norm_mix_g, w_in, conv_w, conv_b, conv_ln_g, conv_ln_b, diff_lambda, diff_subln_g,
           gla_w_gate2, gla_b_gate, gla_norm_g, w_out, norm_ffn_g, ffn_w_gu, ffn_w_down, router_w,
           moe_w_gu, moe_w_down, ple_w_up, ple_w_gate, ple_norm_g, final_norm_g):
    bsz, seq, d = x.shape
    depth = w_in.shape[0]
    t = bsz * seq
    assert d == D_MODEL and seq % TM == 0 and seq % TQ == 0 and t % TMD == 0

    ropes = _rope_tables(seq)
    idx = np.arange(GLA_TRI)
    tri_chunk = jnp.asarray(((idx[:, None] >= idx[None, :])
                             & (idx[:, None] // CHUNK == idx[None, :] // CHUNK)).astype(np.float32)).astype(BF16)
    vi = np.arange(GLA_V)
    ki = np.arange(GLA_QK)
    gmat = jnp.asarray((vi[:, None] // GLA_DV == vi[None, :] // GLA_DV).astype(np.float32)
                       / GLA_DV).astype(BF16)
    head_mask = jnp.asarray((vi[:, None] // GLA_DV == ki[None, :] // GLA_DK).astype(np.float32))
    ti = np.arange(ROUTER_SUB)
    tri_strict = jnp.asarray((ti[:, None] > ti[None, :]).astype(np.float32)).astype(BF16)

    w_in_b = jnp.pad(w_in, ((0, 0), (0, 0), (0, D_IN_PAD - D_IN))).astype(BF16)
    w_out_b = w_out.astype(BF16)
    ffn_gu_b = ffn_w_gu.astype(BF16)[:, None]
    ffn_down_b = ffn_w_down.astype(BF16)[:, None]
    moe_gu_b = moe_w_gu.astype(BF16)
    moe_down_b = moe_w_down.astype(BF16)
    ple_gate_b = ple_w_gate.astype(BF16)
    ple_up_b = ple_w_up.astype(BF16)
    p_rows = p.reshape(depth, t, PLE_DIM)

    def inproj_args(i):
        wg2 = jnp.pad(gla_w_gate2[i], ((0, LANES - GLA_GATE_RANK), (0, 0)))
        return (i, norm_mix_g[i][None, :], w_in_b, *ropes, wg2, gla_b_gate[i][None, :])

    h = x.reshape(t, D_MODEL)
    mixed = _inproj(h, inproj_args(0), seq)
    for i in range(depth):
        lam_init = 0.8 - 0.6 * math.exp(-0.3 * i)
        u, dq, dk, dv, gq, gk, gv, gr, la = mixed
        next_args = inproj_args(i + 1) if i + 1 < depth else None
        y_conv = _conv(u, conv_w[i], conv_b[i][None, :], conv_ln_g[i][None, :],
                       conv_ln_b[i][None, :], bsz, seq)
        y_diff = _attn(dq, dk, dv, diff_lambda[i], diff_subln_g[i][None, :], bsz, seq, lam_init)
        y_gla = _gla(gq, gk, gv, gr, la, tri_chunk, gmat, head_mask,
                     jnp.tile(gla_norm_g[i], GLA_HEADS)[None, :], bsz, seq)
        g_ffn = norm_ffn_g[i][None, :]
        j = i // 2
        ple_args = (p_rows, ple_norm_g[i][None, :], ple_gate_b, ple_up_b, i, final_norm_g[None, :],
                    i == depth - 1)
        if i % 2 == 0:
            h = _ffn(h, y_conv, y_diff, y_gla, w_out_b, i, g_ffn, ffn_gu_b, ffn_down_b, j)
            h, mixed = _ple(h, ple_args, next_args, seq)
        else:
            rw = jnp.pad(router_w[j], ((0, 0), (0, LANES - N_EXPERTS)))
            h, mixed = _moe(h, y_conv, y_diff, y_gla, w_out_b, i, g_ffn, rw, moe_gu_b, moe_down_b, j,
                            tri_strict, ple_args, next_args, seq)
    return h.reshape(bsz, seq, D_MODEL)
```

```python
import functools
import math

import jax
import jax.numpy as jnp
import numpy as np
from jax import lax
from jax.experimental import pallas as pl
from jax.experimental.pallas import tpu as pltpu

F32 = jnp.float32
BF16 = jnp.bfloat16

D_MODEL = 1024
CHUNK = 64
CONV_CH = 256
CONV_WIDTH = 31
DIFF_HEADS = 4
DIFF_DV = 128
DIFF_DH = 64
GLA_HEADS = 4
GLA_DV = 64
GLA_DK = 32
GLA_GATE_RANK = 16
GLA_TAU = 16.0
ROPE_THETA = 500000.0
ROT_DIMS = 16
D_FF = 2816
N_EXPERTS = 8
D_FF_EXPERT = 3584
PLE_DIM = 256
EPS = 1e-6
DIFF_W = 512
GLA_QK = GLA_HEADS * GLA_DK
GLA_V = GLA_HEADS * GLA_DV
D_IN = 2832
D_IN_PAD = 2944

LANES = 128
SUBLANES = 8
VMEM_LIMIT = 56 * 1024 * 1024
LOG2E = math.log2(math.e)

TM = 512
TK = 1024
TQ = TK
Q_SUB = 512
Q_PIECE = 256
CONV_TS = 512
CONV_HALO = 32
CONV_SH_ROWS = CONV_TS + CONV_HALO - SUBLANES
GLA_TG = 1024
GLA_TRI = 512
FFN_FB = 1408
MOE_BLK = 512
MOE_FB = 1792
N_PAD_ROWS = N_EXPERTS * MOE_BLK
ROW_TILE = D_MODEL // LANES
ROUTER_TM = 1024
ROUTER_SUB = 512
TMD = 512
TMC = 256
DMA_UNROLL = 8


def _cparams(sem):
    return pltpu.CompilerParams(dimension_semantics=sem, vmem_limit_bytes=VMEM_LIMIT)


def _rms(x, g):
    ms = jnp.mean(x * x, axis=-1, keepdims=True)
    return x * lax.rsqrt(ms + EPS) * g


def _sigmoid(x):
    return 1.0 / (1.0 + jnp.exp(-x))


def _silu(x):
    return x * _sigmoid(x)


def _split_bf16(x):
    hi = x.astype(BF16)
    return hi, (x - hi.astype(F32)).astype(BF16)


def _dot_f32(a, b):
    a_hi, a_lo = _split_bf16(a)
    b_hi, b_lo = _split_bf16(b)
    dot = functools.partial(jnp.dot, preferred_element_type=F32)
    return dot(a_hi, b_hi) + (dot(a_hi, b_lo) + dot(a_lo, b_hi))


def _rope(x, ra, rb, rc):
    outs = []
    for c in range(x.shape[1] // LANES):
        xc = x[:, c * LANES:(c + 1) * LANES]
        outs.append(xc * ra + pltpu.roll(xc, ROT_DIMS // 2, 1) * rb
                    + pltpu.roll(xc, LANES - ROT_DIMS // 2, 1) * rc)
    return jnp.concatenate(outs, axis=1)


N_INPROJ_IN = 7
N_INPROJ_OUT = 9


def _inproj_math(x, in_refs, out_refs):
    g_ref, w_ref, ra_ref, rb_ref, rc_ref, wg2_ref, bg_ref = in_refs
    u_ref, q_ref, k_ref, v_ref, gq_ref, gk_ref, gv_ref, gr_ref, la_ref = out_refs
    hn = _rms(x, g_ref[...]).astype(BF16)

    def proj(a, b):
        return jnp.dot(hn, w_ref[0, :, a:b], preferred_element_type=F32)

    ra, rb, rc = ra_ref[...], rb_ref[...], rc_ref[...]
    u_ref[...] = proj(0, 256) * _sigmoid(proj(256, 512))
    q_ref[...] = (_rope(proj(512, 1024), ra, rb, rc) * (LOG2E * DIFF_DH ** -0.5)).astype(BF16)
    k_ref[...] = _rope(proj(1024, 1536), ra, rb, rc).astype(BF16)
    v_ref[...] = proj(1536, 2048).astype(BF16)
    gq_ref[...] = proj(2048, 2176) * (GLA_DK ** -0.5)
    gk_ref[...] = proj(2176, 2304)
    gv_ref[...] = proj(2304, 2560)
    gr_ref[...] = _silu(proj(2560, 2816))
    gz = proj(2816, D_IN_PAD)
    ga = _dot_f32(gz, wg2_ref[...]) + bg_ref[...]
    la_ref[...] = (jnp.minimum(ga, 0.0) - jnp.log(1.0 + jnp.exp(-jnp.abs(ga)))) * (1.0 / GLA_TAU)


def _inproj_kernel(h_ref, *refs):
    _inproj_math(h_ref[...], refs[:N_INPROJ_IN], refs[N_INPROJ_IN:])


def _layer_spec(layer, a, b):
    return pl.BlockSpec((1, a, b), lambda i: (layer, 0, 0))


def _inproj_specs(layer, tile, t, seq):
    nseq = seq // tile
    const = lambda a, b: pl.BlockSpec((a, b), lambda i: (0, 0))
    rope_spec = pl.BlockSpec((tile, LANES), lambda i: (i % nseq, 0))
    outs = [(CONV_CH, F32), (DIFF_W, BF16), (DIFF_W, BF16), (DIFF_W, BF16),
            (GLA_QK, F32), (GLA_QK, F32), (GLA_V, F32), (GLA_V, F32), (GLA_QK, F32)]
    in_specs = [const(1, D_MODEL), _layer_spec(layer, D_MODEL, D_IN_PAD),
                rope_spec, rope_spec, rope_spec, const(LANES, LANES), const(1, LANES)]
    out_specs = [pl.BlockSpec((tile, w_), lambda i: (i, 0)) for w_, _ in outs]
    out_shape = [jax.ShapeDtypeStruct((t, w_), dt) for w_, dt in outs]
    return in_specs, out_specs, out_shape


def _inproj(h, inproj_args, seq):
    t = h.shape[0]
    in_specs, out_specs, out_shape = _inproj_specs(inproj_args[0], TM, t, seq)
    return pl.pallas_call(
        _inproj_kernel,
        grid=(t // TM,),
        in_specs=[pl.BlockSpec((TM, D_MODEL), lambda i: (i, 0))] + in_specs,
        out_specs=out_specs,
        out_shape=out_shape,
        compiler_params=_cparams(("parallel",)),
        name="inproj",
    )(h, *inproj_args[1:])


def _conv_kernel(u_ref, w_ref, b_ref, lg_ref, lb_ref, o_ref, win, shifted):
    @pl.when(pl.program_id(1) == 0)
    def _():
        win[0:CONV_HALO, :] = jnp.zeros((CONV_HALO, CONV_CH), F32)

    win[CONV_HALO:, :] = u_ref[...]
    for r in range(1, SUBLANES):
        shifted[r - 1] = win[r:r + CONV_SH_ROWS, :]
    rows = 64
    base = CONV_HALO - (CONV_WIDTH - 1)
    for r0 in range(0, CONV_TS, rows):
        acc = jnp.zeros((rows, CONV_CH), F32) + b_ref[...]
        for j in range(CONV_WIDTH):
            res = (base + j) % SUBLANES
            start = r0 + (base + j) - res
            if res == 0:
                tap = win[start:start + rows, :]
            else:
                tap = shifted[res - 1, start:start + rows, :]
            acc = acc + tap * w_ref[j:j + 1, :]
        mu = jnp.mean(acc, axis=-1, keepdims=True)
        xc = acc - mu
        var = jnp.mean(xc * xc, axis=-1, keepdims=True)
        y = xc * lax.rsqrt(var + EPS) * lg_ref[...] + lb_ref[...]
        o_ref[r0:r0 + rows, :] = _silu(y).astype(BF16)
    win[0:CONV_HALO, :] = win[CONV_TS:CONV_TS + CONV_HALO, :]


def _conv(u, w, b, lg, lb, bsz, seq):
    nt = seq // CONV_TS
    const = lambda a, c: pl.BlockSpec((a, c), lambda bi, ti: (0, 0))
    return pl.pallas_call(
        _conv_kernel,
        grid=(bsz, nt),
        in_specs=[pl.BlockSpec((CONV_TS, CONV_CH), lambda bi, ti: (bi * nt + ti, 0)),
                  const(CONV_WIDTH, CONV_CH), const(1, CONV_CH), const(1, CONV_CH), const(1, CONV_CH)],
        out_specs=pl.BlockSpec((CONV_TS, CONV_CH), lambda bi, ti: (bi * nt + ti, 0)),
        out_shape=jax.ShapeDtypeStruct(u.shape, BF16),
        scratch_shapes=[pltpu.VMEM((CONV_TS + CONV_HALO, CONV_CH), F32),
                        pltpu.VMEM((SUBLANES - 1, CONV_SH_ROWS, CONV_CH), F32)],
        compiler_params=_cparams(("arbitrary", "arbitrary")),
        name="conv",
    )(u, w, b, lg, lb)


def _attn_kernel(qt_ref, kt_ref, lam_ref, sg_ref, q_ref, k_ref, v_ref, o_ref, m_scr, acc_scr,
                 *, lam_init):
    step = pl.program_id(1)
    qi = qt_ref[step]
    kj = kt_ref[step]

    @pl.when(kj == 0)
    def _():
        m_scr[...] = jnp.full(m_scr.shape, -jnp.inf, F32)
        acc_scr[...] = jnp.zeros(acc_scr.shape, F32)

    def update(sub, masked):
        if masked:
            pieces = [(sub * Q_SUB + r, Q_PIECE, sub * Q_SUB + r + Q_PIECE) for r in range(0, Q_SUB, Q_PIECE)]
        else:
            pieces = [(sub * Q_SUB, Q_SUB, TK)]
        for r0, nr, nk in pieces:
            rows = slice(r0, r0 + nr)
            if masked:
                rq = (lax.broadcasted_iota(jnp.int32, (nr, nk), 0) + r0) // CHUNK
                ck = lax.broadcasted_iota(jnp.int32, (nr, nk), 1) // CHUNK
                allowed = ck <= rq
            for h in range(DIFF_HEADS):
                hs = slice(h * LANES, (h + 1) * LANES)
                q = q_ref[rows, hs]
                k = k_ref[0:nk, hs]
                v = v_ref[0:nk, hs]
                vext = jnp.concatenate([v, jnp.ones_like(v)], axis=1)
                lane = lax.broadcasted_iota(jnp.int32, q.shape, 1)
                for c in range(2):
                    sel = (lane < DIFF_DH) if c == 0 else (lane >= DIFF_DH)
                    qc = jnp.where(sel, q, jnp.zeros_like(q))
                    s = lax.dot_general(qc, k, (((1,), (1,)), ((), ())), preferred_element_type=F32)
                    if masked:
                        s = jnp.where(allowed, s, -jnp.inf)
                    m_old = m_scr[2 * h + c, rows]
                    m_new = jnp.maximum(m_old, jnp.max(s, axis=-1, keepdims=True))
                    alpha = jnp.exp2(m_old - m_new)
                    p = jnp.exp2(s - jnp.tile(m_new, (1, nk // LANES)))
                    pv = jnp.dot(p.astype(BF16), vext, preferred_element_type=F32)
                    acc_scr[2 * h + c, rows] = jnp.tile(alpha, (1, 2)) * acc_scr[2 * h + c, rows] + pv
                    m_scr[2 * h + c, rows] = m_new

    @pl.when(kj < qi)
    def _():
        for sub in range(TQ // Q_SUB):
            update(sub, False)

    @pl.when(kj == qi)
    def _():
        for sub in range(TQ // Q_SUB):
            update(sub, True)
        lp = lam_ref[...]
        lam = (jnp.exp(jnp.sum(lp[0:1] * lp[1:2], axis=-1, keepdims=True))
               - jnp.exp(jnp.sum(lp[2:3] * lp[3:4], axis=-1, keepdims=True)) + lam_init)
        for h in range(DIFF_HEADS):
            a0 = acc_scr[2 * h]
            a1 = acc_scr[2 * h + 1]
            o = a0[:, :DIFF_DV] / a0[:, DIFF_DV:] - lam * (a1[:, :DIFF_DV] / a1[:, DIFF_DV:])
            o_ref[:, h * LANES:(h + 1) * LANES] = (_rms(o, sg_ref[...]) * (1.0 - lam_init)).astype(BF16)


def _attn(q, k, v, lam_p, sg, bsz, seq, lam_init):
    nq = seq // TQ
    nk = seq // TK
    pairs = [(a, b) for a in range(nq) for b in range(a + 1)]
    qt = jnp.asarray(np.array([a for a, _ in pairs], np.int32))
    kt = jnp.asarray(np.array([b for _, b in pairs], np.int32))
    qspec = pl.BlockSpec((TQ, DIFF_W), lambda b, s, qt_, kt_: (b * nq + qt_[s], 0))
    kspec = pl.BlockSpec((TK, DIFF_W), lambda b, s, qt_, kt_: (b * nk + kt_[s], 0))
    const = lambda a, c: pl.BlockSpec((a, c), lambda b, s, qt_, kt_: (0, 0))
    grid_spec = pltpu.PrefetchScalarGridSpec(
        num_scalar_prefetch=2,
        grid=(bsz, len(pairs)),
        in_specs=[const(4, DIFF_DH), const(1, DIFF_DV), qspec, kspec, kspec],
        out_specs=qspec,
        scratch_shapes=[pltpu.VMEM((2 * DIFF_HEADS, TQ, LANES), F32),
                        pltpu.VMEM((2 * DIFF_HEADS, TQ, 2 * DIFF_DV), F32)],
    )
    return pl.pallas_call(
        functools.partial(_attn_kernel, lam_init=lam_init),
        grid_spec=grid_spec,
        out_shape=jax.ShapeDtypeStruct(q.shape, BF16),
        compiler_params=_cparams(("parallel", "arbitrary")),
        name="diffattn",
    )(qt, kt, lam_p, sg, q, k, v)


def _gla_kernel(q_ref, k_ref, v_ref, r_ref, la_ref, tri_ref, gmat_ref, mask_ref, ng_ref, o_ref,
                st, kv_scr, st_scr, o_scr):
    @pl.when(pl.program_id(1) == 0)
    def _():
        st[...] = jnp.zeros(st.shape, F32)

    cums = []
    for r0 in range(0, GLA_TG, GLA_TRI):
        la_hi, la_lo = _split_bf16(la_ref[r0:r0 + GLA_TRI, :])
        cums.append(jnp.dot(tri_ref[...], la_hi, preferred_element_type=F32)
                    + jnp.dot(tri_ref[...], la_lo, preferred_element_type=F32))
    cum = jnp.concatenate(cums, axis=0)
    same_head = mask_ref[...] > 0.0
    nch = GLA_TG // CHUNK
    tots = []
    for c in range(nch):
        sl = slice(c * CHUNK, (c + 1) * CHUNK)
        cum_c = cum[sl]
        tot = cum_c[CHUNK - 1:CHUNK]
        tots.append(tot)
        kdec = (k_ref[sl, :] * jnp.exp(tot - cum_c)).astype(BF16)
        v_t = v_ref[sl, :].T.astype(BF16)
        kv_scr[c] = jnp.dot(v_t, kdec, preferred_element_type=F32)
    state = st[...]
    for c in range(nch):
        state = state * jnp.exp(tots[c]) + jnp.where(same_head, kv_scr[c], 0.0)
        st_scr[c] = state.astype(BF16)
    st[...] = state
    for c in range(nch):
        sl = slice(c * CHUNK, (c + 1) * CHUNK)
        o_scr[sl, :] = lax.dot_general(q_ref[sl, :].astype(BF16), st_scr[c], (((1,), (1,)), ((), ())),
                                       preferred_element_type=F32)
    o = o_scr[...]
    ms = jnp.dot((o * o).astype(BF16), gmat_ref[...], preferred_element_type=F32)
    o_ref[...] = (o * lax.rsqrt(ms + EPS) * ng_ref[...] * r_ref[...]).astype(BF16)


def _gla(gq, gk, gv, gr, la, tri, gmat, head_mask, ng, bsz, seq):
    nt = seq // GLA_TG
    row = lambda w_: pl.BlockSpec((GLA_TG, w_), lambda bi, ti: (bi * nt + ti, 0))
    const = lambda a, c: pl.BlockSpec((a, c), lambda bi, ti: (0, 0))
    return pl.pallas_call(
        _gla_kernel,
        grid=(bsz, nt),
        in_specs=[row(GLA_QK), row(GLA_QK), row(GLA_V), row(GLA_V), row(GLA_QK),
                  const(GLA_TRI, GLA_TRI), const(GLA_V, GLA_V), const(GLA_V, GLA_QK), const(1, GLA_V)],
        out_specs=row(GLA_V),
        out_shape=jax.ShapeDtypeStruct(gv.shape, BF16),
        scratch_shapes=[pltpu.VMEM((GLA_V, GLA_QK), F32),
                        pltpu.VMEM((GLA_TG // CHUNK, GLA_V, GLA_QK), F32),
                        pltpu.VMEM((GLA_TG // CHUNK, GLA_V, GLA_QK), BF16),
                        pltpu.VMEM((GLA_TG, GLA_V), F32)],
        compiler_params=_cparams(("arbitrary", "arbitrary")),
        name="gla",
    )(gq, gk, gv, gr, la, tri, gmat, head_mask, ng)


def _mix_residual(h_ref, yc_ref, yd_ref, yg_ref, wo_ref, rows=slice(None)):
    acc = jnp.dot(yc_ref[rows, :], wo_ref[0, 0:256, :], preferred_element_type=F32)
    acc = acc + jnp.dot(yd_ref[rows, :], wo_ref[0, 256:768, :], preferred_element_type=F32)
    acc = acc + jnp.dot(yg_ref[rows, :], wo_ref[0, 768:1024, :], preferred_element_type=F32)
    return h_ref[rows, :] + acc


def _swiglu_step(xb, wg_ref, wu_ref, wd_ref, acc):
    x = xb[...]
    g = jnp.dot(x, wg_ref[0, 0], preferred_element_type=F32)
    u = jnp.dot(x, wu_ref[0, 0], preferred_element_type=F32)
    a = (_silu(g) * u).astype(BF16)
    acc[...] += jnp.dot(a, wd_ref[0, 0], preferred_element_type=F32)


def _ffn_kernel(h_ref, yc_ref, yd_ref, yg_ref, wo_ref, g_ref, wg_ref, wu_ref, wd_ref, o_ref,
                h1, xb, acc):
    j = pl.program_id(1)

    @pl.when(j == 0)
    def _():
        x = _mix_residual(h_ref, yc_ref, yd_ref, yg_ref, wo_ref)
        h1[...] = x
        xb[...] = _rms(x, g_ref[...]).astype(BF16)
        acc[...] = jnp.zeros(acc.shape, F32)

    _swiglu_step(xb, wg_ref, wu_ref, wd_ref, acc)

    @pl.when(j == pl.num_programs(1) - 1)
    def _():
        o_ref[...] = h1[...] + acc[...]


def _ffn(h, yc, yd, yg, wo_stack, layer, g, w_gu, w_down, ffn_layer):
    t = h.shape[0]
    nj = D_FF // FFN_FB
    row = lambda w_: pl.BlockSpec((TM, w_), lambda i, j: (i, 0))
    return pl.pallas_call(
        _ffn_kernel,
        grid=(t // TM, nj),
        in_specs=[row(D_MODEL), row(CONV_CH), row(DIFF_W), row(GLA_V),
                  pl.BlockSpec((1, D_MODEL, D_MODEL), lambda i, j: (layer, 0, 0)),
                  pl.BlockSpec((1, D_MODEL), lambda i, j: (0, 0)),
                  pl.BlockSpec((1, 1, D_MODEL, FFN_FB), lambda i, j: (ffn_layer, 0, 0, j)),
                  pl.BlockSpec((1, 1, D_MODEL, FFN_FB), lambda i, j: (ffn_layer, 0, 0, j + nj)),
                  pl.BlockSpec((1, 1, FFN_FB, D_MODEL), lambda i, j: (ffn_layer, 0, j, 0))],
        out_specs=row(D_MODEL),
        out_shape=jax.ShapeDtypeStruct((t, D_MODEL), F32),
        scratch_shapes=[pltpu.VMEM((TM, D_MODEL), F32), pltpu.VMEM((TM, D_MODEL), BF16),
                        pltpu.VMEM((TM, D_MODEL), F32)],
        compiler_params=_cparams(("parallel", "arbitrary")),
        name="ffn",
    )(h, yc, yd, yg, wo_stack, g, w_gu, w_gu, w_down)


def _tile_rows_load(ref, n):
    return jnp.concatenate([ref[pl.ds(s, n, stride=ROW_TILE), :] for s in range(ROW_TILE)], axis=1)


def _tile_rows_store(ref, x, n, row0=0):
    for s in range(ROW_TILE):
        ref[pl.ds(row0 * ROW_TILE + s, n, stride=ROW_TILE), :] = x[:, s * LANES:(s + 1) * LANES]


def _experts_kernel(be_ref, nu_ref, x_ref, wg_ref, wu_ref, wd_ref, o_ref, xb, acc):
    i = pl.program_id(0)
    j = pl.program_id(1)
    used = i < nu_ref[0]

    @pl.when(used)
    def _():
        @pl.when(j == 0)
        def _():
            xb[...] = _tile_rows_load(x_ref, MOE_BLK).astype(BF16)
            acc[...] = jnp.zeros(acc.shape, F32)

        _swiglu_step(xb, wg_ref, wu_ref, wd_ref, acc)

    @pl.when(j == pl.num_programs(1) - 1)
    def _():
        _tile_rows_store(o_ref, jnp.where(used, acc[...], 0.0), MOE_BLK)


def _experts(xs, w_gu, w_down, layer, block_e, n_used):
    nb = xs.shape[0] // (MOE_BLK * ROW_TILE)
    nj = D_FF_EXPERT // MOE_FB

    def last(i, nu):
        return jnp.minimum(i, nu[0] - 1)

    def hid(i, j, nu):
        return jnp.where(i < nu[0], j, nj - 1)

    xspec = pl.BlockSpec((MOE_BLK * ROW_TILE, LANES), lambda i, j, be, nu: (last(i, nu), 0))
    wg = pl.BlockSpec((1, 1, D_MODEL, MOE_FB),
                      lambda i, j, be, nu: (layer, be[last(i, nu)], 0, hid(i, j, nu)))
    wu = pl.BlockSpec((1, 1, D_MODEL, MOE_FB),
                      lambda i, j, be, nu: (layer, be[last(i, nu)], 0, hid(i, j, nu) + nj))
    wd = pl.BlockSpec((1, 1, MOE_FB, D_MODEL),
                      lambda i, j, be, nu: (layer, be[last(i, nu)], hid(i, j, nu), 0))
    grid_spec = pltpu.PrefetchScalarGridSpec(
        num_scalar_prefetch=2,
        grid=(nb, nj),
        in_specs=[xspec, wg, wu, wd],
        out_specs=pl.BlockSpec((MOE_BLK * ROW_TILE, LANES), lambda i, j, be, nu: (i, 0)),
        scratch_shapes=[pltpu.VMEM((MOE_BLK, D_MODEL), BF16), pltpu.VMEM((MOE_BLK, D_MODEL), F32)],
    )
    return pl.pallas_call(
        _experts_kernel,
        grid_spec=grid_spec,
        out_shape=jax.ShapeDtypeStruct(xs.shape, F32),
        compiler_params=_cparams(("arbitrary", "arbitrary")),
        name="experts",
    )(block_e, n_used, xs, w_gu, w_gu, w_down)


def _router_kernel(h_ref, yc_ref, yd_ref, yg_ref, wo_ref, g_ref, rw_ref, tri_ref,
                   h1_ref, hn_ref, info_ref, cnt_ref, carry):
    @pl.when(pl.program_id(0) == 0)
    def _():
        carry[...] = jnp.zeros(carry.shape, F32)

    subs = [slice(r0, r0 + ROUTER_SUB) for r0 in range(0, ROUTER_TM, ROUTER_SUB)]
    each = lambda f, *cols: [f(*a) for a in zip(*cols)]
    rmax = lambda x: jnp.max(x, axis=-1, keepdims=True)
    rmin = lambda x: jnp.min(x, axis=-1, keepdims=True)
    rsum = lambda x: jnp.sum(x, axis=-1, keepdims=True)
    lane = lax.broadcasted_iota(jnp.int32, (ROUTER_SUB, LANES), 1)

    h1 = each(lambda rows: _mix_residual(h_ref, yc_ref, yd_ref, yg_ref, wo_ref, rows), subs)
    for rows, x in zip(subs, h1):
        h1_ref[rows, :] = x
    hn = each(lambda x: _rms(x, g_ref[...]), h1)
    for rows, x in zip(subs, hn):
        _tile_rows_store(hn_ref, x, ROUTER_SUB, rows.start)
    logits = each(lambda x: _dot_f32(x, rw_ref[...]), hn)
    lg = each(lambda x: jnp.where(lane < N_EXPERTS, x, -jnp.inf), logits)
    v1 = each(rmax, lg)
    i1 = each(lambda x, v: rmin(jnp.where(x == v, lane, LANES)), lg, v1)
    lg2 = each(lambda x, i: jnp.where(lane == i, -jnp.inf, x), lg, i1)
    v2 = each(rmax, lg2)
    i2 = each(lambda x, v: rmin(jnp.where(x == v, lane, LANES)), lg2, v2)
    e2 = each(lambda a, b: jnp.exp(b - a), v1, v2)
    g1 = each(lambda e: 1.0 / (1.0 + e), e2)
    g2 = each(lambda e: e / (1.0 + e), e2)
    pick1 = each(lambda i: lane == i, i1)
    pick2 = each(lambda i: lane == i, i2)
    onehot = each(lambda a, b: jnp.where(a | b, 1.0, 0.0), pick1, pick2)
    local = each(lambda o: jnp.dot(tri_ref[...], o.astype(BF16), preferred_element_type=F32), onehot)
    count = carry[...]
    before = []
    for loc, o in zip(local, onehot):
        before.append(loc + count)
        count = count + jnp.sum(o, axis=0, keepdims=True)
    r1 = each(lambda p_, b: rsum(jnp.where(p_, b, 0.0)), pick1, before)
    r2 = each(lambda p_, b: rsum(jnp.where(p_, b, 0.0)), pick2, before)
    for k, rows in enumerate(subs):
        info = jnp.where(lane == 0, i1[k].astype(F32), 0.0)
        info = jnp.where(lane == 1, i2[k].astype(F32), info)
        info = jnp.where(lane == 2, r1[k], info)
        info = jnp.where(lane == 3, r2[k], info)
        info = jnp.where(lane == 4, g1[k], info)
        info = jnp.where(lane == 5, g2[k], info)
        info_ref[rows, :] = info
    carry[...] = count
    cnt_ref[...] = jnp.broadcast_to(count, cnt_ref.shape)


def _router(h, yc, yd, yg, wo_stack, layer, g, rw, tri):
    t = h.shape[0]
    tm = ROUTER_TM
    row = lambda w_: pl.BlockSpec((tm, w_), lambda i: (i, 0))
    const = lambda a, c: pl.BlockSpec((a, c), lambda i: (0, 0))
    return pl.pallas_call(
        _router_kernel,
        grid=(t // tm,),
        in_specs=[row(D_MODEL), row(CONV_CH), row(DIFF_W), row(GLA_V),
                  _layer_spec(layer, D_MODEL, D_MODEL),
                  const(1, D_MODEL), const(D_MODEL, LANES), const(ROUTER_SUB, ROUTER_SUB)],
        out_specs=[row(D_MODEL), pl.BlockSpec((tm * ROW_TILE, LANES), lambda i: (i, 0)), row(LANES),
                   const(SUBLANES, LANES)],
        out_shape=[jax.ShapeDtypeStruct((t, D_MODEL), F32),
                   jax.ShapeDtypeStruct((t * ROW_TILE, LANES), F32),
                   jax.ShapeDtypeStruct((t, LANES), F32),
                   jax.ShapeDtypeStruct((SUBLANES, LANES), F32)],
        scratch_shapes=[pltpu.VMEM((1, LANES), F32)],
        compiler_params=_cparams(("arbitrary",)),
        name="router",
    )(h, yc, yd, yg, wo_stack, g, rw, tri)


def _row_copy(src, dst, sem):
    return pltpu.make_async_copy(src, dst, sem)


def _tile_at(ref, start):
    if not isinstance(start, int):
        start = pl.multiple_of(start, ROW_TILE)
    return ref.at[pl.ds(start, ROW_TILE), :]


def _dispatch_kernel(dest_ref, pad_ref, hn_ref, xs_ref, zero_tile, sem):
    i = pl.program_id(0)
    token_steps = pl.num_programs(0) - 1

    def run(n, issue, src_tile):
        lax.fori_loop(0, n, issue, 0, unroll=DMA_UNROLL)

        def drain(t, carry):
            _row_copy(src_tile, _tile_at(xs_ref, 0), sem).wait()
            _row_copy(src_tile, _tile_at(xs_ref, 0), sem).wait()
            return carry

        lax.fori_loop(0, n, drain, 0, unroll=DMA_UNROLL)

    @pl.when(i < token_steps)
    def _():
        def issue(t, carry):
            src = _tile_at(hn_ref, t * ROW_TILE)
            _row_copy(src, _tile_at(xs_ref, dest_ref[0, 0, 2 * t]), sem).start(priority=0)
            _row_copy(src, _tile_at(xs_ref, dest_ref[0, 0, 2 * t + 1]), sem).start(priority=1)
            return carry

        run(TMD, issue, _tile_at(hn_ref, 0))

    @pl.when(i == token_steps)
    def _():
        zero_tile[...] = jnp.zeros(zero_tile.shape, F32)

        def issue(t, carry):
            _row_copy(zero_tile, _tile_at(xs_ref, pad_ref[0, 0, 2 * t]), sem).start(priority=0)
            _row_copy(zero_tile, _tile_at(xs_ref, pad_ref[0, 0, 2 * t + 1]), sem).start(priority=1)
            return carry

        run(N_PAD_ROWS // 2, issue, zero_tile)


def _dispatch(dest, pad_dest, hn, m_rows):
    t = hn.shape[0] // ROW_TILE
    n = t // TMD
    return pl.pallas_call(
        _dispatch_kernel,
        grid=(n + 1,),
        in_specs=[pl.BlockSpec((1, 1, 2 * TMD), lambda i: (jnp.minimum(i, n - 1), 0, 0),
                               memory_space=pltpu.SMEM),
                  pl.BlockSpec((1, 1, N_PAD_ROWS), lambda i: (0, 0, 0), memory_space=pltpu.SMEM),
                  pl.BlockSpec((TMD * ROW_TILE, LANES), lambda i: (jnp.minimum(i, n - 1), 0))],
        out_specs=pl.BlockSpec(memory_space=pl.ANY),
        out_shape=jax.ShapeDtypeStruct((m_rows * ROW_TILE, LANES), F32),
        scratch_shapes=[pltpu.VMEM((ROW_TILE, LANES), F32), pltpu.SemaphoreType.DMA(())],
        compiler_params=_cparams(("arbitrary",)),
        name="dispatch",
    )(dest, pad_dest, hn)


def _ple_math(h, p_blk, g_ref, wg_ref, wu_ref, fg_ref, final):
    hn = _rms(h, g_ref[...]).astype(BF16)
    gate = _sigmoid(jnp.dot(hn, wg_ref[0], preferred_element_type=F32))
    up = jnp.dot(p_blk.astype(BF16), wu_ref[0], preferred_element_type=F32)
    out = h + gate * up
    if final:
        out = _rms(out, fg_ref[...])
    return out


def _combine_kernel(dcur_ref, dnext_ref, h_ref, info_ref, p_ref, g_ref, wg_ref, wu_ref, fg_ref, ys_ref,
                    *rest, final, fuse_next):
    next_in, o_ref, next_out, (buf, sems) = _split_fused_refs(rest, fuse_next)
    i = pl.program_id(0)
    slot = i % 2

    def gather(dref, slot_):
        def issue(t, carry):
            for pick in range(2):
                _row_copy(_tile_at(ys_ref, dref[0, 0, 2 * t + pick]),
                          _tile_at(buf.at[slot_, pick], t * ROW_TILE), sems.at[slot_]).start(priority=pick)
            return carry

        lax.fori_loop(0, TMC, issue, 0, unroll=DMA_UNROLL)

    @pl.when(i == 0)
    def _():
        gather(dcur_ref, 0)

    @pl.when(i + 1 < pl.num_programs(0))
    def _():
        gather(dnext_ref, 1 - slot)

    def drain(t, carry):
        for pick in range(2):
            _row_copy(_tile_at(ys_ref, 0), _tile_at(buf.at[slot, pick], 0), sems.at[slot]).wait()
        return carry

    lax.fori_loop(0, TMC, drain, 0, unroll=DMA_UNROLL)
    g1 = info_ref[:, 4:5]
    g2 = info_ref[:, 5:6]
    y1 = _tile_rows_load(buf.at[slot, 0], TMC)
    y2 = _tile_rows_load(buf.at[slot, 1], TMC)
    h2 = h_ref[...] + (g1 * y1 + g2 * y2)
    out = _ple_math(h2, p_ref[0], g_ref, wg_ref, wu_ref, fg_ref, final)
    o_ref[...] = out
    if fuse_next:
        _inproj_math(out, next_in, next_out)


def _split_fused_refs(rest, fuse_next):
    if not fuse_next:
        return (), rest[0], (), rest[1:]
    n_out = N_INPROJ_IN + 1 + N_INPROJ_OUT
    return rest[:N_INPROJ_IN], rest[N_INPROJ_IN], rest[N_INPROJ_IN + 1:n_out], rest[n_out:]


def _combine(dest, h, info, ys, ple_args, next_args, seq):
    p_stack, g, wg_stack, wu_stack, layer, fg, final = ple_args
    t = h.shape[0]
    n = t // TMC
    dest = dest.reshape(n, 1, 2 * TMC)
    row = lambda w_: pl.BlockSpec((TMC, w_), lambda i: (i, 0))
    const = lambda a, c: pl.BlockSpec((a, c), lambda i: (0, 0))
    in_specs = [pl.BlockSpec((1, 1, 2 * TMC), lambda i: (i, 0, 0), memory_space=pltpu.SMEM),
                pl.BlockSpec((1, 1, 2 * TMC), lambda i: (jnp.minimum(i + 1, n - 1), 0, 0),
                             memory_space=pltpu.SMEM),
                row(D_MODEL), row(LANES),
                pl.BlockSpec((1, TMC, PLE_DIM), lambda i: (layer, i, 0)),
                const(1, D_MODEL), _layer_spec(layer, D_MODEL, D_MODEL),
                _layer_spec(layer, PLE_DIM, D_MODEL), const(1, D_MODEL),
                pl.BlockSpec(memory_space=pl.ANY)]
    out_specs = [row(D_MODEL)]
    out_shape = [jax.ShapeDtypeStruct(h.shape, F32)]
    operands = [dest, dest, h, info, p_stack, g, wg_stack, wu_stack, fg, ys]
    if next_args is not None:
        nin, nout, nshape = _inproj_specs(next_args[0], TMC, t, seq)
        in_specs, out_specs, out_shape = in_specs + nin, out_specs + nout, out_shape + nshape
        operands += list(next_args[1:])
    res = pl.pallas_call(
        functools.partial(_combine_kernel, final=final, fuse_next=next_args is not None),
        grid=(n,),
        in_specs=in_specs,
        out_specs=out_specs,
        out_shape=out_shape,
        scratch_shapes=[pltpu.VMEM((2, 2, TMC * ROW_TILE, LANES), F32), pltpu.SemaphoreType.DMA((2,))],
        compiler_params=_cparams(("arbitrary",)),
        name="combine",
    )(*operands)
    return res[0], res[1:]


def _moe(h, yc, yd, yg, wo_stack, layer, g, rw, w_gu, w_down, moe_layer, tri_strict, ple_args,
         next_args, seq):
    t = h.shape[0]
    h1, hn, info, cnt = _router(h, yc, yd, yg, wo_stack, layer, g, rw, tri_strict)
    counts = cnt[0, :N_EXPERTS].astype(jnp.int32)
    padded = ((counts + MOE_BLK - 1) // MOE_BLK) * MOE_BLK
    pend = jnp.cumsum(padded)
    pstart = pend - padded
    e = info[:, 0:2].astype(jnp.int32)
    r = info[:, 2:4].astype(jnp.int32)
    onehot = e[:, :, None] == jnp.arange(N_EXPERTS, dtype=jnp.int32)[None, None, :]
    dest = r + jnp.sum(jnp.where(onehot, pstart[None, None, :], 0), axis=-1)
    dest = (dest * ROW_TILE).reshape(t // TMD, 1, 2 * TMD)
    m_rows = 2 * t + N_EXPERTS * MOE_BLK
    nb = m_rows // MOE_BLK
    blk_start = jnp.arange(nb, dtype=jnp.int32) * MOE_BLK
    block_e = jnp.minimum(jnp.sum(blk_start[:, None] >= pend[None, :], axis=-1), N_EXPERTS - 1)
    n_used = (pend[-1:] // MOE_BLK).astype(jnp.int32)

    pad_len = padded - counts
    pad_end = jnp.cumsum(pad_len)
    pad_k = jnp.arange(N_PAD_ROWS, dtype=jnp.int32)
    pad_e = jnp.sum(pad_k[:, None] >= pad_end[None, :], axis=-1)
    at_e = pad_e[:, None] == jnp.arange(N_EXPERTS, dtype=jnp.int32)[None, :]
    first = jnp.sum(jnp.where(at_e, (pstart + counts - (pad_end - pad_len))[None, :], 0), axis=-1)
    pad_row = jnp.where(pad_e == N_EXPERTS, pend[-1] - pad_end[-1], first) + pad_k
    pad_dest = (pad_row * ROW_TILE).astype(jnp.int32).reshape(1, 1, N_PAD_ROWS)

    xs = _dispatch(dest, pad_dest, hn, m_rows)
    ys = _experts(xs, w_gu, w_down, moe_layer, block_e.astype(jnp.int32), n_used)
    return _combine(dest, h1, info, ys, ple_args, next_args, seq)


def _ple_kernel(h_ref, p_ref, g_ref, wg_ref, wu_ref, fg_ref, *rest, final, fuse_next):
    next_in, o_ref, next_out, _ = _split_fused_refs(rest, fuse_next)
    out = _ple_math(h_ref[...], p_ref[0], g_ref, wg_ref, wu_ref, fg_ref, final)
    o_ref[...] = out
    if fuse_next:
        _inproj_math(out, next_in, next_out)


def _ple(h, ple_args, next_args, seq):
    p_stack, g, wg_stack, wu_stack, layer, fg, final = ple_args
    t = h.shape[0]
    row = lambda w_: pl.BlockSpec((TM, w_), lambda i: (i, 0))
    const = lambda a, c: pl.BlockSpec((a, c), lambda i: (0, 0))
    in_specs = [row(D_MODEL), pl.BlockSpec((1, TM, PLE_DIM), lambda i: (layer, i, 0)),
                const(1, D_MODEL), _layer_spec(layer, D_MODEL, D_MODEL),
                _layer_spec(layer, PLE_DIM, D_MODEL), const(1, D_MODEL)]
    out_specs = [row(D_MODEL)]
    out_shape = [jax.ShapeDtypeStruct(h.shape, F32)]
    operands = [h, p_stack, g, wg_stack, wu_stack, fg]
    if next_args is not None:
        nin, nout, nshape = _inproj_specs(next_args[0], TM, t, seq)
        in_specs, out_specs, out_shape = in_specs + nin, out_specs + nout, out_shape + nshape
        operands += list(next_args[1:])
    res = pl.pallas_call(
        functools.partial(_ple_kernel, final=final, fuse_next=next_args is not None),
        grid=(t // TM,),
        in_specs=in_specs,
        out_specs=out_specs,
        out_shape=out_shape,
        compiler_params=_cparams(("parallel",)),
        name="ple",
    )(*operands)
    return res[0], res[1:]


def _rope_tables(seq):
    half = ROT_DIMS // 2
    pos = jnp.arange(seq, dtype=F32)
    inv_freq = ROPE_THETA ** (-jnp.arange(0, ROT_DIMS, 2, dtype=F32) / ROT_DIMS)
    ang = pos[:, None] * inv_freq[None, :]
    cos, sin = jnp.cos(ang), jnp.sin(ang)
    lane = np.arange(LANES) % DIFF_DH
    fidx = lane % half
    first = jnp.asarray(lane < half)[None, :]
    second = jnp.asarray((lane >= half) & (lane < ROT_DIMS))[None, :]
    ra = jnp.where(first | second, cos[:, fidx], 1.0)
    rb = jnp.where(second, sin[:, fidx], 0.0)
    rc = jnp.where(first, -sin[:, fidx], 0.0)
    return ra, rb, rc


def kernel(x, p, norm_mix_g, w_in, conv_w, conv_b, conv_ln_g, conv_ln_b, diff_lambda, diff_subln_g,
           gla_w_gate2, gla_b_gate, gla_norm_g, w_out, norm_ffn_g, ffn_w_gu, ffn_w_down, router_w,
           moe_w_gu, moe_w_down, ple_w_up, ple_w_gate, ple_norm_g, final_norm_g):
    bsz, seq, d = x.shape
    depth = w_in.shape[0]
    t = bsz * seq
    assert d == D_MODEL and seq % TM == 0 and seq % TQ == 0 and t % TMD == 0

    ropes = _rope_tables(seq)
    idx = np.arange(GLA_TRI)
    tri_chunk = jnp.asarray(((idx[:, None] >= idx[None, :])
                             & (idx[:, None] // CHUNK == idx[None, :] // CHUNK)).astype(np.float32)).astype(BF16)
    vi = np.arange(GLA_V)
    ki = np.arange(GLA_QK)
    gmat = jnp.asarray((vi[:, None] // GLA_DV == vi[None, :] // GLA_DV).astype(np.float32)
                       / GLA_DV).astype(BF16)
    head_mask = jnp.asarray((vi[:, None] // GLA_DV == ki[None, :] // GLA_DK).astype(np.float32))
    ti = np.arange(ROUTER_SUB)
    tri_strict = jnp.asarray((ti[:, None] > ti[None, :]).astype(np.float32)).astype(BF16)

    w_in_b = jnp.pad(w_in, ((0, 0), (0, 0), (0, D_IN_PAD - D_IN))).astype(BF16)
    w_out_b = w_out.astype(BF16)
    ffn_gu_b = ffn_w_gu.astype(BF16)[:, None]
    ffn_down_b = ffn_w_down.astype(BF16)[:, None]
    moe_gu_b = moe_w_gu.astype(BF16)
    moe_down_b = moe_w_down.astype(BF16)
    ple_gate_b = ple_w_gate.astype(BF16)
    ple_up_b = ple_w_up.astype(BF16)
    p_rows = p.reshape(depth, t, PLE_DIM)

    def inproj_args(i):
        wg2 = jnp.pad(gla_w_gate2[i], ((0, LANES - GLA_GATE_RANK), (0, 0)))
        return (i, norm_mix_g[i][None, :], w_in_b, *ropes, wg2, gla_b_gate[i][None, :])

    h = x.reshape(t, D_MODEL)
    mixed = _inproj(h, inproj_args(0), seq)
    for i in range(depth):
        lam_init = 0.8 - 0.6 * math.exp(-0.3 * i)
        u, dq, dk, dv, gq, gk, gv, gr, la = mixed
        next_args = inproj_args(i + 1) if i + 1 < depth else None
        y_conv = _conv(u, conv_w[i], conv_b[i][None, :], conv_ln_g[i][None, :],
                       conv_ln_b[i][None, :], bsz, seq)
        y_diff = _attn(dq, dk, dv, diff_lambda[i], diff_subln_g[i][None, :], bsz, seq, lam_init)
        y_gla = _gla(gq, gk, gv, gr, la, tri_chunk, gmat, head_mask,
                     jnp.tile(gla_norm_g[i], GLA_HEADS)[None, :], bsz, seq)
        g_ffn = norm_ffn_g[i][None, :]
        j = i // 2
        ple_args = (p_rows, ple_norm_g[i][None, :], ple_gate_b, ple_up_b, i, final_norm_g[None, :],
                    i == depth - 1)
        if i % 2 == 0:
            h = _ffn(h, y_conv, y_diff, y_gla, w_out_b, i, g_ffn, ffn_gu_b, ffn_down_b, j)
            h, mixed = _ple(h, ple_args, next_args, seq)
        else:
            rw = jnp.pad(router_w[j], ((0, 0), (0, LANES - N_EXPERTS)))
            h, mixed = _moe(h, y_conv, y_diff, y_gla, w_out_b, i, g_ffn, rw, moe_gu_b, moe_down_b, j,
                            tri_strict, ple_args, next_args, seq)
    return h.reshape(bsz, seq, D_MODEL)
```
